```python
import functools
import jax, jax.numpy as jnp
from jax import lax
import numpy as np

D_MODEL = 2048
BATCH = 4
SEQ = 2048
DEPTH = 1
DEC_BATCH = 128
DEC_SEQ = 8
PAST_LEN = 16384
PAGE_SIZE = 128

DN_HEADS = 8
DN_DK = 128
DN_DV = 128
DN_CONV = 4
DN_QK = DN_HEADS * DN_DK
DN_VW = DN_HEADS * DN_DV
DN_QKV = 2 * DN_QK + DN_VW
ML_HEADS = 4
ML_DK = 128
ML_DV = 256
ML_QK = ML_HEADS * ML_DK
ML_VW = ML_HEADS * ML_DV
MIX_WIDTH = DN_VW + ML_VW
D_FF = 5632
FFN_CONV = 3
CHUNK = 64
EPS = 1e-6
PROJ_SIZES = (DN_QKV, DN_VW, DN_HEADS, DN_HEADS, ML_QK, ML_QK, ML_VW, ML_VW, ML_HEADS, ML_HEADS)
PROJ_COLS = DN_QKV + DN_VW + 2 * DN_HEADS + 2 * ML_QK + 2 * ML_VW + 2 * ML_HEADS

kernel_name = "hymba_gdn_mlstm_convffn_step"


def _rmsnorm(x, g):
    xf = x.astype(jnp.float32)
    y = xf * lax.rsqrt(jnp.mean(xf * xf, axis=-1, keepdims=True) + EPS)
    return (y * g.astype(jnp.float32)).astype(x.dtype)


def _l2norm(x):
    return x * lax.rsqrt(jnp.sum(x * x, axis=-1, keepdims=True) + EPS)


def _causal_dwconv(x, buf, w):
    width, L = w.shape[0], x.shape[1]
    xp = jnp.concatenate([buf.astype(jnp.float32), x.astype(jnp.float32)], axis=1)
    wf = w.astype(jnp.float32)
    out = sum(xp[:, j:j + L] * wf[j] for j in range(width))
    return out, xp[:, L:]


def _chunk_size(L):
    return CHUNK if L % CHUNK == 0 else L


def _to_chunks(t, n, c):
    t = t.reshape((t.shape[0], n, c) + t.shape[2:])
    return jnp.swapaxes(jnp.swapaxes(t, 0, 1), 2, 3)


def _from_chunks(o):
    n, b, h, c, d = o.shape
    return o.transpose(1, 0, 3, 2, 4).reshape(b, n * c, h, d)


def _gated_delta_rule(q, k, v, g, beta, S0):
    L = q.shape[1]
    c = _chunk_size(L)
    n = L // c
    q, k, v, g, beta = (_to_chunks(t, n, c) for t in (q, k, v, g, beta))
    G = jnp.cumsum(g, axis=-1)
    tril = jnp.tril(jnp.ones((c, c), dtype=bool))
    strict = jnp.tril(jnp.ones((c, c), dtype=bool), -1)
    decay = jnp.exp(jnp.where(tril, G[..., :, None] - G[..., None, :], -jnp.inf))
    kb = k * beta[..., None]
    lower = jnp.where(strict, jnp.einsum('nbhik,nbhjk->nbhij', kb, k) * decay, 0.0)
    a_mat = lower + jnp.eye(c, dtype=lower.dtype)
    solve = functools.partial(lax.linalg.triangular_solve, left_side=True, lower=True, unit_diagonal=True)
    u = solve(a_mat, v * beta[..., None])
    w = solve(a_mat, kb * jnp.exp(G)[..., None])
    qk = jnp.einsum('nbhik,nbhjk->nbhij', q, k) * decay
    qe = q * jnp.exp(G)[..., None]
    G_last = G[..., -1:]
    kd = k * jnp.exp(G_last - G)[..., None]
    gl = jnp.exp(G_last[..., 0])

    def step(S, xs):
        qe_c, qk_c, u_c, w_c, kd_c, gl_c = xs
        v_new = u_c - jnp.einsum('bhck,bhkv->bhcv', w_c, S)
        o = jnp.einsum('bhck,bhkv->bhcv', qe_c, S) + jnp.einsum('bhij,bhjv->bhiv', qk_c, v_new)
        S = S * gl_c[..., None, None] + jnp.einsum('bhck,bhcv->bhkv', kd_c, v_new)
        return S, o

    S, o = lax.scan(step, S0, (qe, qk, u, w, kd, gl))
    return _from_chunks(o), S


def _mlstm(q, k, v, ig, logf, C0, n0, m0):
    L = q.shape[1]
    c = _chunk_size(L)
    n = L // c
    q, k, v, ig, logf = (_to_chunks(t, n, c) for t in (q, k, v, ig, logf))
    tril = jnp.tril(jnp.ones((c, c), dtype=bool))

    def step(carry, xs):
        Cm, nv, m = carry
        q_c, k_c, v_c, i_c, f_c = xs
        b = jnp.cumsum(f_c, axis=-1)
        D = jnp.where(tril, b[..., :, None] - b[..., None, :] + i_c[..., None, :], -jnp.inf)
        m_new = jnp.maximum(b + m[..., None], jnp.max(D, axis=-1))
        inter = jnp.exp(b + m[..., None] - m_new)
        s = jnp.einsum('bhik,bhjk->bhij', q_c, k_c) * jnp.exp(D - m_new[..., None])
        num = inter[..., None] * jnp.einsum('bhck,bhkv->bhcv', q_c, Cm) + jnp.einsum('bhij,bhjv->bhiv', s, v_c)
        den = inter * jnp.einsum('bhck,bhk->bhc', q_c, nv) + jnp.sum(s, axis=-1)
        h = num / jnp.maximum(jnp.abs(den), jnp.exp(-m_new))[..., None]
        m_end = m_new[..., -1]
        w_end = jnp.exp(b[..., -1:] - b + i_c - m_end[..., None])
        carry_scale = jnp.exp(b[..., -1] + m - m_end)
        Cm = carry_scale[..., None, None] * Cm + jnp.einsum('bhck,bhcv->bhkv', k_c * w_end[..., None], v_c)
        nv = carry_scale[..., None] * nv + jnp.einsum('bhck,bhc->bhk', k_c, w_end)
        return (Cm, nv, m_end), h

    (Cm, nv, m), h = lax.scan(step, (C0, n0, m0), (q, k, v, ig, logf))
    return _from_chunks(h), Cm, nv, m


def _layer(x, st, p):
    conv_buf, S0, C0, n0, m0, ffn_buf = st
    (norm_mix_g, w_in, dn_conv_w, dn_A_log, dn_dt_bias, dn_norm_g, ml_i_bias, ml_f_bias,
     ml_norm_g, w_out, norm_ffn_g, w_up, ffn_conv_w, ffn_conv_b, w_down) = p
    f32 = jnp.float32
    B, L, _ = x.shape
    h = _rmsnorm(x, norm_mix_g)
    proj = jnp.einsum('bld,de->ble', h, w_in)
    idx = np.cumsum(PROJ_SIZES)[:-1].tolist()
    dn_qkv, dn_z, dn_b, dn_a, ml_q, ml_k, ml_v, ml_o, ml_i, ml_f = jnp.split(proj, idx, axis=-1)

    qkv_c, conv_new = _causal_dwconv(dn_qkv, conv_buf, dn_conv_w)
    qkv_c = jax.nn.silu(qkv_c)
    dq, dkk, dvv = jnp.split(qkv_c, [DN_QK, 2 * DN_QK], axis=-1)
    dq = _l2norm(dq.reshape(B, L, DN_HEADS, DN_DK)) * (DN_DK ** -0.5)
    dkk = _l2norm(dkk.reshape(B, L, DN_HEADS, DN_DK))
    dvv = dvv.reshape(B, L, DN_HEADS, DN_DV)
    beta = jax.nn.sigmoid(dn_b.astype(f32))
    g = -jnp.exp(dn_A_log.astype(f32)) * jax.nn.softplus(dn_a.astype(f32) + dn_dt_bias.astype(f32))
    o_dn, S_new = _gated_delta_rule(dq, dkk, dvv, g, beta, S0.astype(f32))
    o_dn = _rmsnorm(o_dn, dn_norm_g) * jax.nn.silu(dn_z.astype(f32).reshape(B, L, DN_HEADS, DN_DV))

    mq = ml_q.astype(f32).reshape(B, L, ML_HEADS, ML_DK) * (ML_DK ** -0.5)
    mk = ml_k.astype(f32).reshape(B, L, ML_HEADS, ML_DK)
    mv = ml_v.astype(f32).reshape(B, L, ML_HEADS, ML_DV)
    ig = ml_i.astype(f32) + ml_i_bias.astype(f32)
    logf = jax.nn.log_sigmoid(ml_f.astype(f32) + ml_f_bias.astype(f32))
    h_ml, C_new, n_new, m_new = _mlstm(mq, mk, mv, ig, logf, C0.astype(f32), n0.astype(f32), m0.astype(f32))
    o_ml = _rmsnorm(h_ml, ml_norm_g.reshape(ML_HEADS, ML_DV)) * jax.nn.sigmoid(ml_o.astype(f32).reshape(B, L, ML_HEADS, ML_DV))

    mix = jnp.concatenate([o_dn.reshape(B, L, DN_VW), o_ml.reshape(B, L, ML_VW)], axis=-1).astype(x.dtype)
    x = x + jnp.einsum('ble,ed->bld', mix, w_out)

    h = _rmsnorm(x, norm_ffn_g)
    u = jnp.einsum('bld,df->blf', h, w_up)
    uc, ffn_new = _causal_dwconv(u, ffn_buf, ffn_conv_w)
    uc = uc + ffn_conv_b.astype(f32)
    gate, up = jnp.split(uc, 2, axis=-1)
    x = x + jnp.einsum('blf,fd->bld', (jax.nn.silu(gate) * up).astype(x.dtype), w_down)
    new_st = (conv_new.astype(conv_buf.dtype), S_new.astype(S0.dtype), C_new.astype(C0.dtype),
              n_new.astype(n0.dtype), m_new.astype(m0.dtype), ffn_new.astype(ffn_buf.dtype))
    return x, new_st


def setup_inputs(seed: int = 0) -> dict:
    key = jax.random.key(seed)
    ks = jax.random.split(key, 32)
    nrm = lambda k, s, sc=1.0: jax.random.normal(k, s, jnp.float32) * sc
    dt = jnp.exp(jax.random.uniform(ks[10], (DEPTH, DN_HEADS), jnp.float32, np.log(1e-3), np.log(1e-1)))
    return {
        "x_prompt": nrm(ks[0], (BATCH, SEQ, D_MODEL)),
        "x_sample": nrm(ks[1], (DEC_BATCH, DEC_SEQ, D_MODEL)),
        "state_dn_conv": nrm(ks[2], (DEPTH, DEC_BATCH, DN_CONV - 1, DN_QKV)),
        "state_dn_S": nrm(ks[3], (DEPTH, DEC_BATCH, DN_HEADS, DN_DK, DN_DV), 0.1),
        "state_ml_C": nrm(ks[4], (DEPTH, DEC_BATCH, ML_HEADS, ML_DK, ML_DV), 0.1),
        "state_ml_n": jnp.abs(nrm(ks[5], (DEPTH, DEC_BATCH, ML_HEADS, ML_DK), 0.1)),
        "state_ml_m": nrm(ks[6], (DEPTH, DEC_BATCH, ML_HEADS)),
        "state_ffn_conv": nrm(ks[7], (DEPTH, DEC_BATCH, FFN_CONV - 1, 2 * D_FF)),
        "norm_mix_g": 1.0 + nrm(ks[8], (DEPTH, D_MODEL), 0.02),
        "w_in": nrm(ks[9], (DEPTH, D_MODEL, PROJ_COLS), D_MODEL ** -0.5),
        "dn_conv_w": nrm(ks[11], (DEPTH, DN_CONV, DN_QKV), DN_CONV ** -0.5),
        "dn_A_log": jnp.log(jax.random.uniform(ks[12], (DEPTH, DN_HEADS), jnp.float32, 1.0, 16.0)),
        "dn_dt_bias": dt + jnp.log(-jnp.expm1(-dt)),
        "dn_norm_g": 1.0 + nrm(ks[13], (DEPTH, DN_DV), 0.02),
        "ml_i_bias": -3.0 + nrm(ks[14], (DEPTH, ML_HEADS), 0.1),
        "ml_f_bias": jnp.linspace(3.0, 6.0, ML_HEADS, dtype=jnp.float32) + nrm(ks[15], (DEPTH, ML_HEADS), 0.1),
        "ml_norm_g": 1.0 + nrm(ks[16], (DEPTH, ML_VW), 0.02),
        "w_out": nrm(ks[17], (DEPTH, MIX_WIDTH, D_MODEL), MIX_WIDTH ** -0.5),
        "norm_ffn_g": 1.0 + nrm(ks[18], (DEPTH, D_MODEL), 0.02),
        "w_up": nrm(ks[19], (DEPTH, D_MODEL, 2 * D_FF), D_MODEL ** -0.5),
        "ffn_conv_w": nrm(ks[20], (DEPTH, FFN_CONV, 2 * D_FF), FFN_CONV ** -0.5),
        "ffn_conv_b": nrm(ks[21], (DEPTH, 2 * D_FF), 0.02),
        "w_down": nrm(ks[22], (DEPTH, D_FF, D_MODEL), D_FF ** -0.5),
        "norm_final_g": 1.0 + nrm(ks[23], (D_MODEL,), 0.02),
    }


def reference(x_prompt, x_sample, state_dn_conv, state_dn_S, state_ml_C, state_ml_n, state_ml_m,
              state_ffn_conv, norm_mix_g, w_in, dn_conv_w, dn_A_log, dn_dt_bias, dn_norm_g,
              ml_i_bias, ml_f_bias, ml_norm_g, w_out, norm_ffn_g, w_up, ffn_conv_w, ffn_conv_b,
              w_down, norm_final_g):
    weights = (norm_mix_g, w_in, dn_conv_w, dn_A_log, dn_dt_bias, dn_norm_g, ml_i_bias, ml_f_bias,
               ml_norm_g, w_out, norm_ffn_g, w_up, ffn_conv_w, ffn_conv_b, w_down)
    states = (state_dn_conv, state_dn_S, state_ml_C, state_ml_n, state_ml_m, state_ffn_conv)
    xp, xs = x_prompt, x_sample
    p_new = [[] for _ in states]
    s_new = [[] for _ in states]
    for l in range(DEPTH):
        p = tuple(w[l] for w in weights)
        st_p = tuple(jnp.zeros((BATCH,) + s.shape[2:], s.dtype) for s in states)
        st_s = tuple(s[l] for s in states)
        xp, np_st = _layer(xp, st_p, p)
        xs, ns_st = _layer(xs, st_s, p)
        for i in range(len(states)):
            p_new[i].append(np_st[i])
            s_new[i].append(ns_st[i])
    y_prompt = _rmsnorm(xp, norm_final_g)
    y_sample = _rmsnorm(xs, norm_final_g)
    p_dn_conv, p_dn_S, p_ml_C, p_ml_n, p_ml_m, p_ffn_conv = (jnp.stack(a, axis=0) for a in p_new)
    s_dn_conv, s_dn_S, s_ml_C, s_ml_n, s_ml_m, s_ffn_conv = (jnp.stack(a, axis=0) for a in s_new)
    return (y_prompt, y_sample, p_dn_conv, p_dn_S, p_ml_C, p_ml_n, p_ml_m, p_ffn_conv,
            s_dn_conv, s_dn_S, s_ml_C, s_ml_n, s_ml_m, s_ffn_conv)
```

```python
import functools

import jax
import jax.numpy as jnp
from jax import lax
from jax.experimental import pallas as pl
from jax.experimental.pallas import tpu as pltpu

EPS = 1e-6
CHUNK = 64
F32 = jnp.float32
BF16 = jnp.bfloat16
LANES = 128
SUBLANES = 8
VMEM_LIMIT = 56 * 1024 * 1024


def _cparams(sem, vmem=VMEM_LIMIT):
    return pltpu.CompilerParams(dimension_semantics=sem, vmem_limit_bytes=vmem)


def _dot(a, b):
    return jnp.dot(a.astype(BF16), b.astype(BF16), preferred_element_type=F32)


def _dot_nt(a, b):
    return lax.dot_general(a.astype(BF16), b.astype(BF16), (((1,), (1,)), ((), ())),
                           preferred_element_type=F32)


def _dot_tn(a, b):
    return lax.dot_general(a.astype(BF16), b.astype(BF16), (((0,), (0,)), ((), ())),
                           preferred_element_type=F32)


def _mask_dot_exact(mask_bf16, x):
    hi = x.astype(BF16)
    r1 = x - hi.astype(F32)
    mid = r1.astype(BF16)
    lo = (r1 - mid.astype(F32)).astype(BF16)
    d = functools.partial(jnp.dot, preferred_element_type=F32)
    return (d(mask_bf16, hi) + d(mask_bf16, mid)) + d(mask_bf16, lo)


def _softplus(x):
    return jnp.maximum(x, 0.0) + jnp.log1p(jnp.exp(-jnp.abs(x)))


def _sigmoid(x):
    return 1.0 / (1.0 + jnp.exp(-x))


def _rms(x, g):
    return x * lax.rsqrt(jnp.mean(x * x, axis=-1, keepdims=True) + EPS) * g


def _log2(n):
    assert n & (n - 1) == 0 and n > 0, n
    return n.bit_length() - 1


def _pick(n, cands):
    for c in cands:
        if n % c == 0:
            return c
    return n


def _inproj_kernel(x_ref, g_ref, wb_ref, ws_ref, ob_ref, os_ref, h_ref):
    @pl.when(pl.program_id(1) == 0)
    def _():
        h = _rms(x_ref[...], g_ref[...]).astype(BF16)
        h_ref[...] = h
        os_ref[...] = jnp.dot(h, ws_ref[...], preferred_element_type=F32)

    ob_ref[...] = jnp.dot(h_ref[...], wb_ref[...], preferred_element_type=F32)


def _inproj(x, g, w_big, w_small):
    m, d = x.shape
    nb = w_big.shape[1]
    tm = _pick(m, (512, 256, 128, 64, 32, 16, 8))
    tn = _pick(nb, (1024, 512, 256, 128))
    return pl.pallas_call(
        _inproj_kernel,
        grid=(m // tm, nb // tn),
        in_specs=[
            pl.BlockSpec((tm, d), lambda i, j: (i, 0)),
            pl.BlockSpec((1, d), lambda i, j: (0, 0)),
            pl.BlockSpec((d, tn), lambda i, j: (0, j)),
            pl.BlockSpec((d, LANES), lambda i, j: (0, 0)),
        ],
        out_specs=[
            pl.BlockSpec((tm, tn), lambda i, j: (i, j)),
            pl.BlockSpec((tm, LANES), lambda i, j: (i, 0)),
        ],
        out_shape=[jax.ShapeDtypeStruct((m, nb), F32), jax.ShapeDtypeStruct((m, LANES), F32)],
        scratch_shapes=[pltpu.VMEM((tm, d), BF16)],
        compiler_params=_cparams(("parallel", "arbitrary")),
        name="inproj",
    )(x, g.reshape(1, d), w_big, w_small)


def _seq_masks(R, T):
    RT = R * T
    sh = _log2(T)
    ri = lax.broadcasted_iota(jnp.int32, (RT, RT), 0)
    ci = lax.broadcasted_iota(jnp.int32, (RT, RT), 1)
    same = (ri >> sh) == (ci >> sh)
    tril = same & (ci <= ri)
    strict = same & (ci < ri)
    eye = ri == ci
    lastsel = same & ((ci & (T - 1)) == T - 1)
    rowseq = lax.broadcasted_iota(jnp.int32, (RT, 1), 0) >> sh
    return ri, ci, tril, strict, eye, lastsel, rowseq


def _row_form(col, eye):
    return jnp.sum(jnp.where(eye, col, 0.0), axis=0, keepdims=True)


def _at_last(row, lastsel):
    return jnp.sum(jnp.where(lastsel, row, 0.0), axis=1, keepdims=True)


def _unit_lower_inverse(lm, ri, ci, T):
    d = None
    s = 1
    while s < T:
        sh = _log2(2 * s)
        blk = ((ri >> sh) == (ci >> sh)) & ((ri & (2 * s - 1)) >= s) & ((ci & (2 * s - 1)) < s)
        m = jnp.where(blk, lm, 0.0)
        if d is None:
            d = jnp.where(ri == ci, 1.0, 0.0) - m
        else:
            d = d - _dot(_dot(d, m), d)
        s *= 2
    return d


def _causal_conv(ext_ref, x_ref, st_ref, cw_ref, first, T, width):
    lo = SUBLANES - (width - 1)

    @pl.when(first)
    def _():
        ext_ref[:, lo:SUBLANES, :] = st_ref[...]

    ext_ref[:, SUBLANES:SUBLANES + T, :] = x_ref[...]
    cw = cw_ref[...]
    acc = None
    for j in range(width):
        term = ext_ref[:, lo + j:lo + j + T, :] * cw[j:j + 1, :]
        acc = term if acc is None else acc + term
    ext_ref[:, lo:SUBLANES, :] = ext_ref[:, lo + T:SUBLANES + T, :]
    return acc


def _gdn_kernel(qkv_ref, z_ref, sm_ref, cst_ref, s0_ref, cw_ref, par_ref, ng_ref,
                o_ref, s_ref, ext_ref, *, R, T, H, DK, DV, width):
    RT = R * T
    QK = H * DK
    first = pl.program_id(1) == 0

    @pl.when(first)
    def _():
        s_ref[...] = s0_ref[...]

    y = _causal_conv(ext_ref, qkv_ref, cst_ref, cw_ref, first, T, width)
    y = y.reshape(RT, y.shape[-1])
    y = y * _sigmoid(y)
    z = z_ref[...].reshape(RT, H * DV)

    sm = sm_ref[...].reshape(RT, LANES)
    beta_all = _sigmoid(sm)
    g_all = -jnp.exp(par_ref[1:2, :]) * _softplus(sm + par_ref[0:1, :])

    ri, ci, tril, strict, eye, lastsel, rowseq = _seq_masks(R, T)
    g_cum = _mask_dot_exact(jnp.where(tril, 1.0, 0.0).astype(BF16), g_all)

    for h in range(H):
        q = y[:, h * DK:(h + 1) * DK]
        k = y[:, QK + h * DK:QK + (h + 1) * DK]
        v = y[:, 2 * QK + h * DV:2 * QK + (h + 1) * DV]
        q = q * lax.rsqrt(jnp.sum(q * q, axis=-1, keepdims=True) + EPS) * (DK ** -0.5)
        k = k * lax.rsqrt(jnp.sum(k * k, axis=-1, keepdims=True) + EPS)
        beta = beta_all[:, h:h + 1]
        gc = g_cum[:, H + h:H + h + 1]
        gr = _row_form(gc, eye)
        glast = _at_last(gr, lastsel)
        decay = jnp.exp(jnp.where(tril, gc - gr, -jnp.inf))
        kb = k * beta
        lm = jnp.where(strict, _dot_nt(kb, k) * decay, 0.0)
        tinv = _unit_lower_inverse(lm, ri, ci, T)
        eg = jnp.exp(gc)
        uw = _dot(tinv, jnp.concatenate([v * beta, kb * eg], axis=1))
        u, w = uw[:, :DV], uw[:, DV:]
        qk = _dot_nt(q, k) * decay
        qe = q * eg
        kd = k * jnp.exp(glast - gc)
        gl = jnp.exp(glast)

        if R == 1:
            s_old = s_ref[0, h]
            ws = _dot(jnp.concatenate([w, qe], axis=0), s_old)
            v_new = u - ws[:RT]
            o = ws[RT:] + _dot(qk, v_new)
            s_ref[0, h] = s_old * gl[0:1, :] + _dot_tn(kd, v_new)
        else:
            w_parts, q_parts = [], []
            for s in range(R):
                rows = slice(s * T, (s + 1) * T)
                ws = _dot(jnp.concatenate([w[rows], qe[rows]], axis=0), s_ref[s, h])
                w_parts.append(ws[:T])
                q_parts.append(ws[T:])
            v_new = u - jnp.concatenate(w_parts, axis=0)
            o = jnp.concatenate(q_parts, axis=0) + _dot(qk, v_new)
            for s in range(R):
                kd_s = jnp.where(rowseq == s, kd, 0.0)
                s_ref[s, h] = s_ref[s, h] * gl[s * T:s * T + 1, :] + _dot_tn(kd_s, v_new)

        zh = z[:, h * DV:(h + 1) * DV]
        out = _rms(o, ng_ref[...]) * (zh * _sigmoid(zh))
        o_ref[:, h * DV:(h + 1) * DV] = out.astype(o_ref.dtype)


def _gdn(proj3, small3, conv_state, s0, conv_w, par, norm_g, *, R, T, N):
    B, H, DK, DV = s0.shape
    QKV = conv_state.shape[-1]
    width = conv_w.shape[0]
    assert QKV == 2 * H * DK + H * DV and QKV % LANES == 0 and (H * DV) % LANES == 0
    assert QKV % (H * DV) == 0
    zblk = QKV // (H * DV)
    rows = proj3.shape[0] * T
    kern = functools.partial(_gdn_kernel, R=R, T=T, H=H, DK=DK, DV=DV, width=width)
    return pl.pallas_call(
        kern,
        grid=(B // R, N),
        in_specs=[
            pl.BlockSpec((R, T, QKV), lambda b, n: (b * N + n, 0, 0)),
            pl.BlockSpec((R, T, H * DV), lambda b, n: (b * N + n, 0, zblk)),
            pl.BlockSpec((R, T, LANES), lambda b, n: (b * N + n, 0, 0)),
            pl.BlockSpec((R, width - 1, QKV), lambda b, n: (b, 0, 0)),
            pl.BlockSpec((R, H, DK, DV), lambda b, n: (b, 0, 0, 0)),
            pl.BlockSpec((width, QKV), lambda b, n: (0, 0)),
            pl.BlockSpec((SUBLANES, LANES), lambda b, n: (0, 0)),
            pl.BlockSpec((1, DV), lambda b, n: (0, 0)),
        ],
        out_specs=[
            pl.BlockSpec((R * T, H * DV), lambda b, n: (b * N + n, 0)),
            pl.BlockSpec((R, H, DK, DV), lambda b, n: (b, 0, 0, 0)),
        ],
        out_shape=[jax.ShapeDtypeStruct((rows, H * DV), BF16),
                   jax.ShapeDtypeStruct((B, H, DK, DV), F32)],
        scratch_shapes=[pltpu.VMEM((R, SUBLANES + T, QKV), F32)],
        compiler_params=_cparams(("parallel", "arbitrary")),
        name="gdn",
    )(proj3, proj3, small3, conv_state, s0, conv_w, par, norm_g.reshape(1, DV))


def _mlstm_kernel(q_ref, k_ref, v_ref, og_ref, sm_ref, c0_ref, n0_ref, m0_ref, par_ref, ng_ref,
                  o_ref, c_ref, n_ref, m_ref, *, R, T, H, DK, DV, off_i, off_f):
    RT = R * T

    @pl.when(pl.program_id(1) == 0)
    def _():
        c_ref[...] = c0_ref[...]
        n_ref[...] = n0_ref[...]
        m_ref[...] = m0_ref[...]

    pre = sm_ref[...].reshape(RT, LANES) + par_ref[0:1, :]
    logf_all = -_softplus(-pre)
    ri, ci, tril, strict, eye, lastsel, rowseq = _seq_masks(R, T)
    b_all = _mask_dot_exact(jnp.where(tril, 1.0, 0.0).astype(BF16), logf_all)

    qa = q_ref[...].reshape(RT, H * DK)
    ka = k_ref[...].reshape(RT, H * DK)
    va = v_ref[...].reshape(RT, H * DV)
    oa = og_ref[...].reshape(RT, H * DV)

    for h in range(H):
        q = qa[:, h * DK:(h + 1) * DK] * (DK ** -0.5)
        k = ka[:, h * DK:(h + 1) * DK]
        v = va[:, h * DV:(h + 1) * DV]
        bc = b_all[:, off_f + h:off_f + h + 1]
        ic = pre[:, off_i + h:off_i + h + 1]
        dm = jnp.where(tril, bc + _row_form(ic - bc, eye), -jnp.inf)

        if R == 1:
            m_rows = m_ref[0, :, h:h + 1]
        else:
            m_rows = jnp.zeros((RT, 1), F32)
            for s in range(R):
                m_rows = jnp.where(rowseq == s, m_ref[s, :, h:h + 1], m_rows)
        m_new = jnp.maximum(bc + m_rows, jnp.max(dm, axis=1, keepdims=True))
        inter = jnp.exp(bc + m_rows - m_new)
        smat = _dot_nt(q, k) * jnp.exp(dm - m_new)

        if R == 1:
            qc = _dot(q, c_ref[0, h])
            qn = jnp.sum(q * n_ref[0, h:h + 1, :], axis=1, keepdims=True)
        else:
            qc_parts, qn_parts = [], []
            for s in range(R):
                rows = slice(s * T, (s + 1) * T)
                qc_parts.append(_dot(q[rows], c_ref[s, h]))
                qn_parts.append(jnp.sum(q[rows] * n_ref[s, h:h + 1, :], axis=1, keepdims=True))
            qc = jnp.concatenate(qc_parts, axis=0)
            qn = jnp.concatenate(qn_parts, axis=0)
        num = inter * qc + _dot(smat, v)
        den = inter * qn + jnp.sum(smat, axis=1, keepdims=True)
        hh = num / jnp.maximum(jnp.abs(den), jnp.exp(-m_new))

        b_last = _at_last(_row_form(bc, eye), lastsel)
        m_end = _at_last(_row_form(m_new, eye), lastsel)
        kw = k * jnp.exp(b_last - bc + ic - m_end)
        cs = jnp.exp(b_last + m_rows - m_end)
        for s in range(R):
            kw_s = kw if R == 1 else jnp.where(rowseq == s, kw, 0.0)
            cs_s = cs[s * T:s * T + 1, :]
            c_ref[s, h] = cs_s * c_ref[s, h] + _dot_tn(kw_s, v)
            n_ref[s, h:h + 1, :] = cs_s * n_ref[s, h:h + 1, :] + jnp.sum(kw_s, axis=0, keepdims=True)
            m_ref[s, :, h:h + 1] = m_end[s * T:s * T + 1, :]

        og = oa[:, h * DV:(h + 1) * DV]
        out = _rms(hh, ng_ref[:, h * DV:(h + 1) * DV]) * _sigmoid(og)
        o_ref[:, h * DV:(h + 1) * DV] = out.astype(o_ref.dtype)


def _mlstm(proj3, small3, c0, n0, m0, par, norm_g, *, R, T, N, col0, off_i, off_f):
    B, H, DK, DV = c0.shape
    QK, VW = H * DK, H * DV
    assert col0 % QK == 0 and (col0 + 2 * QK) % VW == 0 and QK % LANES == 0
    rows = proj3.shape[0] * T
    kern = functools.partial(_mlstm_kernel, R=R, T=T, H=H, DK=DK, DV=DV, off_i=off_i, off_f=off_f)
    qb = col0 // QK
    vb = (col0 + 2 * QK) // VW
    state_specs = [
        pl.BlockSpec((R, H, DK, DV), lambda b, n: (b, 0, 0, 0)),
        pl.BlockSpec((R, H, DK), lambda b, n: (b, 0, 0)),
        pl.BlockSpec((R, 1, H), lambda b, n: (b, 0, 0)),
    ]
    return pl.pallas_call(
        kern,
        grid=(B // R, N),
        in_specs=[
            pl.BlockSpec((R, T, QK), lambda b, n: (b * N + n, 0, qb)),
            pl.BlockSpec((R, T, QK), lambda b, n: (b * N + n, 0, qb + 1)),
            pl.BlockSpec((R, T, VW), lambda b, n: (b * N + n, 0, vb)),
            pl.BlockSpec((R, T, VW), lambda b, n: (b * N + n, 0, vb + 1)),
            pl.BlockSpec((R, T, LANES), lambda b, n: (b * N + n, 0, 0)),
            *state_specs,
            pl.BlockSpec((SUBLANES, LANES), lambda b, n: (0, 0)),
            pl.BlockSpec((1, VW), lambda b, n: (0, 0)),
        ],
        out_specs=[pl.BlockSpec((R * T, VW), lambda b, n: (b * N + n, 0)), *state_specs],
        out_shape=[jax.ShapeDtypeStruct((rows, VW), BF16),
                   jax.ShapeDtypeStruct(c0.shape, F32),
                   jax.ShapeDtypeStruct(n0.shape, F32),
                   jax.ShapeDtypeStruct(m0.shape, F32)],
        compiler_params=_cparams(("parallel", "arbitrary")),
        name="mlstm",
    )(proj3, proj3, proj3, proj3, small3, c0, n0, m0, par, norm_g.reshape(1, VW))


def _outproj_kernel(x_ref, a_ref, b_ref, wa_ref, wb_ref, o_ref):
    o_ref[...] = (x_ref[...] + jnp.dot(a_ref[...], wa_ref[...], preferred_element_type=F32)
                  + jnp.dot(b_ref[...], wb_ref[...], preferred_element_type=F32))


def _outproj(x, a, b, w_out):
    m, d = x.shape
    ka, kb = a.shape[1], b.shape[1]
    assert ka == kb
    tm = _pick(m, (512, 256, 128, 64, 32, 16, 8))
    tn = _pick(d, (1024, 512, 256, 128))
    return pl.pallas_call(
        _outproj_kernel,
        grid=(m // tm, d // tn),
        in_specs=[
            pl.BlockSpec((tm, tn), lambda i, j: (i, j)),
            pl.BlockSpec((tm, ka), lambda i, j: (i, 0)),
            pl.BlockSpec((tm, kb), lambda i, j: (i, 0)),
            pl.BlockSpec((ka, tn), lambda i, j: (0, j)),
            pl.BlockSpec((kb, tn), lambda i, j: (1, j)),
        ],
        out_specs=pl.BlockSpec((tm, tn), lambda i, j: (i, j)),
        out_shape=jax.ShapeDtypeStruct((m, d), F32),
        compiler_params=_cparams(("parallel", "parallel")),
        name="outproj",
    )(x, a, b, w_out, w_out)


def _ffn_kernel(x_ref, g_ref, wg_ref, wu_ref, cwg_ref, cwu_ref, cbg_ref, cbu_ref, wd_ref,
                stg_ref, stu_ref, gf_ref, y_ref, nsg_ref, nsu_ref, h_ref, ext_ref, *carry,
                tm, shift, pad, tps, width, final_norm):
    i = pl.program_id(0)
    j = pl.program_id(1)
    hist = (width - 1) * shift

    @pl.when(j == 0)
    def _():
        h_ref[...] = _rms(x_ref[...], g_ref[...]).astype(BF16)
        y_ref[...] = jnp.zeros_like(y_ref)

    h = h_ref[...]
    conv = []
    for idx, (w_ref, cw_ref, cb_ref, st_ref, ns_ref) in enumerate(
            ((wg_ref, cwg_ref, cbg_ref, stg_ref, nsg_ref), (wu_ref, cwu_ref, cbu_ref, stu_ref, nsu_ref))):
        u = jnp.dot(h, w_ref[...], preferred_element_type=F32)
        ext_ref[idx, pad:pad + tm, :] = u
        if tps == 1:
            ext_ref[idx, pad - hist:pad, :] = st_ref[0]
        else:
            @pl.when(i % tps == 0)
            def _():
                ext_ref[idx, pad - hist:pad, :] = st_ref[0]

            @pl.when(i % tps != 0)
            def _():
                ext_ref[idx, pad - hist:pad, :] = carry[0][j, idx]
        cw = cw_ref[...]
        c = u * cw[width - 1:width, :] + cb_ref[...]
        for t in range(width - 1):
            off = pad - hist + t * shift
            c = c + ext_ref[idx, off:off + tm, :] * cw[t:t + 1, :]
        new = u[tm - hist:, :]
        ns_ref[0] = new
        if tps != 1:
            carry[0][j, idx] = new
        conv.append(c)

    act = (conv[0] * _sigmoid(conv[0]) * conv[1]).astype(BF16)
    y_ref[...] += jnp.dot(act, wd_ref[...], preferred_element_type=F32)

    @pl.when(j == pl.num_programs(1) - 1)
    def _():
        xo = x_ref[...] + y_ref[...]
        y_ref[...] = _rms(xo, gf_ref[...]) if final_norm else xo


def _ffn(x, norm_g, w_up, conv_w, conv_b, w_down, state, final_g, *, tm, shift, tps, final_norm):
    m, d = x.shape
    ff = w_down.shape[0]
    width = conv_w.shape[0]
    hist = (width - 1) * shift
    pad = -(-hist // SUBLANES) * SUBLANES
    tf = _pick(ff, (512, 256, 128))
    nj = ff // tf
    conv_b = conv_b.reshape(1, 2 * ff)
    kern = functools.partial(_ffn_kernel, tm=tm, shift=shift, pad=pad, tps=tps, width=width,
                             final_norm=final_norm)
    scratch = [pltpu.VMEM((tm, d), BF16), pltpu.VMEM((2, pad + tm, tf), F32)]
    if tps != 1:
        scratch.append(pltpu.VMEM((nj, 2, hist, tf), F32))
    st_shape = jax.ShapeDtypeStruct((m // tm, hist, ff), F32)
    y, nsg, nsu = pl.pallas_call(
        kern,
        grid=(m // tm, nj),
        in_specs=[
            pl.BlockSpec((tm, d), lambda i, j: (i, 0)),
            pl.BlockSpec((1, d), lambda i, j: (0, 0)),
            pl.BlockSpec((d, tf), lambda i, j: (0, j)),
            pl.BlockSpec((d, tf), lambda i, j: (0, nj + j)),
            pl.BlockSpec((width, tf), lambda i, j: (0, j)),
            pl.BlockSpec((width, tf), lambda i, j: (0, nj + j)),
            pl.BlockSpec((1, tf), lambda i, j: (0, j)),
            pl.BlockSpec((1, tf), lambda i, j: (0, nj + j)),
            pl.BlockSpec((tf, d), lambda i, j: (j, 0)),
            pl.BlockSpec((1, hist, tf), lambda i, j: (i // tps, 0, j)),
            pl.BlockSpec((1, hist, tf), lambda i, j: (i // tps, 0, nj + j)),
            pl.BlockSpec((1, d), lambda i, j: (0, 0)),
        ],
        out_specs=[
            pl.BlockSpec((tm, d), lambda i, j: (i, 0)),
            pl.BlockSpec((1, hist, tf), lambda i, j: (i, 0, j)),
            pl.BlockSpec((1, hist, tf), lambda i, j: (i, 0, j)),
        ],
        out_shape=[jax.ShapeDtypeStruct((m, d), F32), st_shape, st_shape],
        scratch_shapes=scratch,
        compiler_params=_cparams(("arbitrary", "arbitrary")),
        name="convffn",
    )(x, norm_g.reshape(1, d), w_up, w_up, conv_w, conv_w, conv_b, conv_b, w_down, state, state,
      final_g.reshape(1, d))
    return y, nsg[tps - 1::tps], nsu[tps - 1::tps]


def _layer(x, st, p, final_g, final_norm, decode):
    conv_buf, s0, c0, n0, m0, ffn_buf = st
    B, L, D = x.shape
    Hd, Hm = s0.shape[1], c0.shape[1]
    dn_qkv = conv_buf.shape[-1]
    dn_vw = Hd * s0.shape[3]
    ff = p["w_down"].shape[0]

    T = CHUNK if L % CHUNK == 0 else L
    N = L // T
    R = _pick(B, (8, 4, 2, 1)) if decode else 1
    if R * T > 128:
        R = 1

    x2 = x.reshape(B * L, D)
    big, small = _inproj(x2, p["norm_mix_g"], p["w_big"], p["w_small"])
    big3 = big.reshape(B * N, T, big.shape[1])
    small3 = small.reshape(B * N, T, LANES)

    o_dn, s_new = _gdn(big3, small3, conv_buf, s0, p["dn_conv_w"], p["par"], p["dn_norm_g"],
                       R=R, T=T, N=N)
    o_ml, c_new, n_new, m_new = _mlstm(big3, small3, c0, n0, m0.reshape(B, 1, Hm), p["par"],
                                       p["ml_norm_g"], R=R, T=T, N=N, col0=dn_qkv + dn_vw,
                                       off_i=2 * Hd, off_f=2 * Hd + Hm)
    width = p["dn_conv_w"].shape[0]
    big4 = big.reshape(B, L, big.shape[1])
    if L >= width - 1:
        conv_new = big4[:, L - (width - 1):, :dn_qkv]
    else:
        conv_new = jnp.concatenate([conv_buf, big4[:, :, :dn_qkv]], axis=1)[:, L:]

    x1 = _outproj(x2, o_dn, o_ml, p["w_out"])

    fw = p["ffn_conv_w"].shape[0]
    if decode:
        x1t = x1.reshape(B, L, D).transpose(1, 0, 2).reshape(L * B, D)
        state = ffn_buf.transpose(1, 0, 2).reshape(1, (fw - 1) * B, 2 * ff)
        y, nsg, nsu = _ffn(x1t, p["norm_ffn_g"], p["w_up"], p["ffn_conv_w"], p["ffn_conv_b"],
                           p["w_down"], state, final_g, tm=L * B, shift=B, tps=1,
                           final_norm=final_norm)
        y = y.reshape(L, B, D).transpose(1, 0, 2)
        ffn_new = jnp.concatenate([nsg, nsu], axis=-1).reshape(fw - 1, B, 2 * ff).transpose(1, 0, 2)
    else:
        tm = _pick(L, (512, 256, 128, 64, 32, 16, 8))
        y, nsg, nsu = _ffn(x1, p["norm_ffn_g"], p["w_up"], p["ffn_conv_w"], p["ffn_conv_b"],
                           p["w_down"], ffn_buf, final_g, tm=tm, shift=1, tps=L // tm,
                           final_norm=final_norm)
        y = y.reshape(B, L, D)
        ffn_new = jnp.concatenate([nsg, nsu], axis=-1)
    return y, (conv_new, s_new, c_new, n_new, m_new.reshape(B, Hm), ffn_new)


def _prep_params(l, Hd, Hm, dn_qkv, dn_vw, ml_qk, ml_vw, norm_mix_g, w_in, dn_conv_w, dn_A_log,
                 dn_dt_bias, dn_norm_g, ml_i_bias, ml_f_bias, ml_norm_g, w_out, norm_ffn_g, w_up,
                 ffn_conv_w, ffn_conv_b, w_down):
    w = w_in[l]
    c0 = dn_qkv + dn_vw
    c1 = c0 + 2 * Hd
    c2 = c1 + 2 * ml_qk + 2 * ml_vw
    nsmall = 2 * Hd + 2 * Hm
    assert nsmall <= LANES and w.shape[1] == c2 + 2 * Hm
    w_big = jnp.concatenate([w[:, :c0], w[:, c1:c2]], axis=1).astype(BF16)
    w_small = jnp.concatenate([w[:, c0:c1], w[:, c2:]], axis=1)
    w_small = jnp.pad(w_small, ((0, 0), (0, LANES - nsmall))).astype(BF16)
    z = lambda n: jnp.zeros((n,), F32)
    bias = jnp.concatenate([z(Hd), dn_dt_bias[l], ml_i_bias[l], ml_f_bias[l], z(LANES - nsmall)])
    alog = jnp.concatenate([z(Hd), dn_A_log[l], z(LANES - 2 * Hd)])
    par = jnp.zeros((SUBLANES, LANES), F32).at[0].set(bias).at[1].set(alog)
    return dict(norm_mix_g=norm_mix_g[l], w_big=w_big, w_small=w_small, dn_conv_w=dn_conv_w[l],
                par=par, dn_norm_g=dn_norm_g[l], ml_norm_g=ml_norm_g[l], w_out=w_out[l].astype(BF16),
                norm_ffn_g=norm_ffn_g[l], w_up=w_up[l].astype(BF16), ffn_conv_w=ffn_conv_w[l],
                ffn_conv_b=ffn_conv_b[l], w_down=w_down[l].astype(BF16))


def kernel(x_prompt, x_sample, state_dn_conv, state_dn_S, state_ml_C, state_ml_n, state_ml_m,
           state_ffn_conv, norm_mix_g, w_in, dn_conv_w, dn_A_log, dn_dt_bias, dn_norm_g,
           ml_i_bias, ml_f_bias, ml_norm_g, w_out, norm_ffn_g, w_up, ffn_conv_w, ffn_conv_b,
           w_down, norm_final_g):
    states = (state_dn_conv, state_dn_S, state_ml_C, state_ml_n, state_ml_m, state_ffn_conv)
    depth = w_in.shape[0]
    batch = x_prompt.shape[0]
    Hd, dk, dv = state_dn_S.shape[2:]
    Hm, mdk, mdv = state_ml_C.shape[2:]
    xp, xs = x_prompt, x_sample
    p_new = [[] for _ in states]
    s_new = [[] for _ in states]
    for l in range(depth):
        p = _prep_params(l, Hd, Hm, state_dn_conv.shape[-1], Hd * dv, Hm * mdk, Hm * mdv,
                         norm_mix_g, w_in, dn_conv_w, dn_A_log, dn_dt_bias, dn_norm_g, ml_i_bias,
                         ml_f_bias, ml_norm_g, w_out, norm_ffn_g, w_up, ffn_conv_w, ffn_conv_b,
                         w_down)
        last = l == depth - 1
        st_p = tuple(jnp.zeros((batch,) + s.shape[2:], s.dtype) for s in states)
        st_s = tuple(s[l] for s in states)
        xp, np_st = _layer(xp, st_p, p, norm_final_g, last, decode=False)
        xs, ns_st = _layer(xs, st_s, p, norm_final_g, last, decode=True)
        for i in range(len(states)):
            p_new[i].append(np_st[i])
            s_new[i].append(ns_st[i])
    outs_p = tuple(jnp.stack(a, axis=0) for a in p_new)
    outs_s = tuple(jnp.stack(a, axis=0) for a in s_new)
    return (xp, xs) + outs_p + outs_s
```

```python
import functools

import jax
import jax.numpy as jnp
from jax import lax
from jax.experimental import pallas as pl
from jax.experimental.pallas import tpu as pltpu

EPS = 1e-6
CHUNK = 64
MIX_CHUNK = 128
F32 = jnp.float32
BF16 = jnp.bfloat16
LANES = 128
SUBLANES = 8
VMEM_LIMIT = 56 * 1024 * 1024


def _cparams(sem, vmem=VMEM_LIMIT):
    return pltpu.CompilerParams(dimension_semantics=sem, vmem_limit_bytes=vmem)


def _dot(a, b):
    return jnp.dot(a.astype(BF16), b.astype(BF16), preferred_element_type=F32)


def _dot_nt(a, b):
    return lax.dot_general(a.astype(BF16), b.astype(BF16), (((1,), (1,)), ((), ())),
                           preferred_element_type=F32)


def _dot_tn(a, b):
    return lax.dot_general(a.astype(BF16), b.astype(BF16), (((0,), (0,)), ((), ())),
                           preferred_element_type=F32)


def _mask_dot_exact(mask_bf16, x):
    hi = x.astype(BF16)
    r1 = x - hi.astype(F32)
    mid = r1.astype(BF16)
    lo = (r1 - mid.astype(F32)).astype(BF16)
    d = functools.partial(jnp.dot, preferred_element_type=F32)
    return (d(mask_bf16, hi) + d(mask_bf16, mid)) + d(mask_bf16, lo)


def _softplus(x):
    return jnp.maximum(x, 0.0) + jnp.log1p(jnp.exp(-jnp.abs(x)))


def _sigmoid(x):
    return 1.0 / (1.0 + jnp.exp(-x))


def _rms(x, g):
    return x * lax.rsqrt(jnp.mean(x * x, axis=-1, keepdims=True) + EPS) * g


def _log2(n):
    assert n & (n - 1) == 0 and n > 0, n
    return n.bit_length() - 1


def _pick(n, cands):
    for c in cands:
        if n % c == 0:
            return c
    return n


def _inproj_kernel(x_ref, g_ref, wb_ref, ws_ref, ob_ref, os_ref, h_ref):
    @pl.when(pl.program_id(1) == 0)
    def _():
        h = _rms(x_ref[...], g_ref[...]).astype(BF16)
        h_ref[...] = h
        os_ref[...] = jnp.dot(h, ws_ref[...], preferred_element_type=F32)

    ob_ref[...] = jnp.dot(h_ref[...], wb_ref[...], preferred_element_type=F32)


def _inproj(x, g, w_big, w_small):
    m, d = x.shape
    nb = w_big.shape[1]
    tm = _pick(m, (512, 256, 128, 64, 32, 16, 8))
    tn = _pick(nb, (1024, 512, 256, 128))
    return pl.pallas_call(
        _inproj_kernel,
        grid=(m // tm, nb // tn),
        in_specs=[
            pl.BlockSpec((tm, d), lambda i, j: (i, 0)),
            pl.BlockSpec((1, d), lambda i, j: (0, 0)),
            pl.BlockSpec((d, tn), lambda i, j: (0, j)),
            pl.BlockSpec((d, LANES), lambda i, j: (0, 0)),
        ],
        out_specs=[
            pl.BlockSpec((tm, tn), lambda i, j: (i, j)),
            pl.BlockSpec((tm, LANES), lambda i, j: (i, 0)),
        ],
        out_shape=[jax.ShapeDtypeStruct((m, nb), F32), jax.ShapeDtypeStruct((m, LANES), F32)],
        scratch_shapes=[pltpu.VMEM((tm, d), BF16)],
        compiler_params=_cparams(("parallel", "arbitrary")),
        name="inproj",
    )(x, g.reshape(1, d), w_big, w_small)


def _seq_masks(R, T):
    RT = R * T
    sh = _log2(T)
    ri = lax.broadcasted_iota(jnp.int32, (RT, RT), 0)
    ci = lax.broadcasted_iota(jnp.int32, (RT, RT), 1)
    same = (ri >> sh) == (ci >> sh)
    tril = same & (ci <= ri)
    strict = same & (ci < ri)
    eye = ri == ci
    lastsel = same & ((ci & (T - 1)) == T - 1)
    rowseq = lax.broadcasted_iota(jnp.int32, (RT, 1), 0) >> sh
    return ri, ci, tril, strict, eye, lastsel, rowseq


def _row_form(col, eye):
    return jnp.sum(jnp.where(eye, col, 0.0), axis=0, keepdims=True)


def _at_last(row, lastsel):
    return jnp.sum(jnp.where(lastsel, row, 0.0), axis=1, keepdims=True)


def _unit_lower_inverses(lms, ri, ci, T):
    ds = None
    s = 1
    while s < T:
        sh = _log2(2 * s)
        blk = ((ri >> sh) == (ci >> sh)) & ((ri & (2 * s - 1)) >= s) & ((ci & (2 * s - 1)) < s)
        ms = [jnp.where(blk, lm, 0.0) for lm in lms]
        if ds is None:
            eye = jnp.where(ri == ci, 1.0, 0.0)
            ds = [eye - m for m in ms]
        else:
            dms = [_dot(d, m) for d, m in zip(ds, ms)]
            ds = [d - _dot(dm, d) for d, dm in zip(ds, dms)]
        s *= 2
    return ds


def _causal_conv(ext_ref, x_ref, st_ref, cw_ref, first, T, width):
    lo = SUBLANES - (width - 1)

    @pl.when(first)
    def _():
        ext_ref[:, lo:SUBLANES, :] = st_ref[...]

    ext_ref[:, SUBLANES:SUBLANES + T, :] = x_ref[...]
    cw = cw_ref[...]
    acc = None
    for j in range(width):
        term = ext_ref[:, lo + j:lo + j + T, :] * cw[j:j + 1, :]
        acc = term if acc is None else acc + term
    ext_ref[:, lo:SUBLANES, :] = ext_ref[:, lo + T:SUBLANES + T, :]
    return acc


def _gdn_kernel(qkv_ref, z_ref, sm_ref, cst_ref, s0_ref, cw_ref, par_ref, ng_ref,
                o_ref, s_ref, ext_ref, *, R, T, H, DK, DV, width):
    RT = R * T
    QK = H * DK
    first = pl.program_id(1) == 0

    @pl.when(first)
    def _():
        s_ref[...] = s0_ref[...]

    y = _causal_conv(ext_ref, qkv_ref, cst_ref, cw_ref, first, T, width)
    y = y.reshape(RT, y.shape[-1])
    y = y * _sigmoid(y)
    z = z_ref[...].reshape(RT, H * DV)

    sm = sm_ref[...].reshape(RT, LANES)
    beta_all = _sigmoid(sm)
    g_all = -jnp.exp(par_ref[1:2, :]) * _softplus(sm + par_ref[0:1, :])

    ri, ci, tril, strict, eye, lastsel, rowseq = _seq_masks(R, T)
    g_cum = _mask_dot_exact(jnp.where(tril, 1.0, 0.0).astype(BF16), g_all)

    hs = range(H)
    q = [y[:, h * DK:(h + 1) * DK] for h in hs]
    k = [y[:, QK + h * DK:QK + (h + 1) * DK] for h in hs]
    v = [y[:, 2 * QK + h * DV:2 * QK + (h + 1) * DV] for h in hs]
    q = [a * lax.rsqrt(jnp.sum(a * a, axis=-1, keepdims=True) + EPS) * (DK ** -0.5) for a in q]
    k = [a * lax.rsqrt(jnp.sum(a * a, axis=-1, keepdims=True) + EPS) for a in k]
    beta = [beta_all[:, h:h + 1] for h in hs]
    gc = [g_cum[:, H + h:H + h + 1] for h in hs]
    gr = [_row_form(a, eye) for a in gc]
    glast = [_at_last(a, lastsel) for a in gr]
    decay = [jnp.exp(jnp.where(tril, c - r, -jnp.inf)) for c, r in zip(gc, gr)]
    kb = [a * b for a, b in zip(k, beta)]
    lm = [jnp.where(strict, _dot_nt(a, b) * d, 0.0) for a, b, d in zip(kb, k, decay)]
    tinv = _unit_lower_inverses(lm, ri, ci, T)
    eg = [jnp.exp(a) for a in gc]
    uw = [_dot(t, jnp.concatenate([a * b, c * e], axis=1))
          for t, a, b, c, e in zip(tinv, v, beta, kb, eg)]
    u = [a[:, :DV] for a in uw]
    w = [a[:, DV:] for a in uw]
    qk = [_dot_nt(a, b) * d for a, b, d in zip(q, k, decay)]
    qe = [a * e for a, e in zip(q, eg)]
    kd = [a * jnp.exp(l - c) for a, l, c in zip(k, glast, gc)]
    gl = [jnp.exp(a) for a in glast]

    if R == 1:
        s_old = [s_ref[0, h] for h in hs]
        ws = [_dot(jnp.concatenate([a, b], axis=0), c) for a, b, c in zip(w, qe, s_old)]
        v_new = [a - b[:RT] for a, b in zip(u, ws)]
        o = [a[RT:] + _dot(b, c) for a, b, c in zip(ws, qk, v_new)]
        for h in hs:
            s_ref[0, h] = s_old[h] * gl[h][0:1, :] + _dot_tn(kd[h], v_new[h])
    else:
        ws = [[_dot(jnp.concatenate([w[h][s * T:(s + 1) * T], qe[h][s * T:(s + 1) * T]], axis=0),
                    s_ref[s, h]) for s in range(R)] for h in hs]
        v_new = [u[h] - jnp.concatenate([a[:T] for a in ws[h]], axis=0) for h in hs]
        o = [jnp.concatenate([a[T:] for a in ws[h]], axis=0) + _dot(qk[h], v_new[h]) for h in hs]
        for h in hs:
            for s in range(R):
                kd_s = jnp.where(rowseq == s, kd[h], 0.0)
                s_ref[s, h] = s_ref[s, h] * gl[h][s * T:s * T + 1, :] + _dot_tn(kd_s, v_new[h])

    for h in hs:
        zh = z[:, h * DV:(h + 1) * DV]
        out = _rms(o[h], ng_ref[...]) * (zh * _sigmoid(zh))
        o_ref[:, h * DV:(h + 1) * DV] = out.astype(o_ref.dtype)


def _gdn(proj3, small3, conv_state, s0, conv_w, par, norm_g, *, R, T, N):
    B, H, DK, DV = s0.shape
    QKV = conv_state.shape[-1]
    width = conv_w.shape[0]
    assert QKV == 2 * H * DK + H * DV and QKV % LANES == 0 and (H * DV) % LANES == 0
    assert QKV % (H * DV) == 0
    zblk = QKV // (H * DV)
    rows = proj3.shape[0] * T
    kern = functools.partial(_gdn_kernel, R=R, T=T, H=H, DK=DK, DV=DV, width=width)
    return pl.pallas_call(
        kern,
        grid=(B // R, N),
        in_specs=[
            pl.BlockSpec((R, T, QKV), lambda b, n: (b * N + n, 0, 0)),
            pl.BlockSpec((R, T, H * DV), lambda b, n: (b * N + n, 0, zblk)),
            pl.BlockSpec((R, T, LANES), lambda b, n: (b * N + n, 0, 0)),
            pl.BlockSpec((R, width - 1, QKV), lambda b, n: (b, 0, 0)),
            pl.BlockSpec((R, H, DK, DV), lambda b, n: (b, 0, 0, 0)),
            pl.BlockSpec((width, QKV), lambda b, n: (0, 0)),
            pl.BlockSpec((SUBLANES, LANES), lambda b, n: (0, 0)),
            pl.BlockSpec((1, DV), lambda b, n: (0, 0)),
        ],
        out_specs=[
            pl.BlockSpec((R * T, H * DV), lambda b, n: (b * N + n, 0)),
            pl.BlockSpec((R, H, DK, DV), lambda b, n: (b, 0, 0, 0)),
        ],
        out_shape=[jax.ShapeDtypeStruct((rows, H * DV), BF16),
                   jax.ShapeDtypeStruct((B, H, DK, DV), F32)],
        scratch_shapes=[pltpu.VMEM((R, SUBLANES + T, QKV), F32)],
        compiler_params=_cparams(("parallel", "arbitrary")),
        name="gdn",
    )(proj3, proj3, small3, conv_state, s0, conv_w, par, norm_g.reshape(1, DV))


def _mlstm_kernel(q_ref, k_ref, v_ref, og_ref, sm_ref, c0_ref, n0_ref, m0_ref, par_ref, ng_ref,
                  o_ref, c_ref, n_ref, m_ref, *, R, T, H, DK, DV, off_i, off_f):
    RT = R * T

    @pl.when(pl.program_id(1) == 0)
    def _():
        c_ref[...] = c0_ref[...]
        n_ref[...] = n0_ref[...]
        m_ref[...] = m0_ref[...]

    pre = sm_ref[...].reshape(RT, LANES) + par_ref[0:1, :]
    logf_all = -_softplus(-pre)
    ri, ci, tril, strict, eye, lastsel, rowseq = _seq_masks(R, T)
    b_all = _mask_dot_exact(jnp.where(tril, 1.0, 0.0).astype(BF16), logf_all)

    qa = q_ref[...].reshape(RT, H * DK)
    ka = k_ref[...].reshape(RT, H * DK)
    va = v_ref[...].reshape(RT, H * DV)
    oa = og_ref[...].reshape(RT, H * DV)

    hs = range(H)
    q = [qa[:, h * DK:(h + 1) * DK] * (DK ** -0.5) for h in hs]
    k = [ka[:, h * DK:(h + 1) * DK] for h in hs]
    v = [va[:, h * DV:(h + 1) * DV] for h in hs]
    bc = [b_all[:, off_f + h:off_f + h + 1] for h in hs]
    ic = [pre[:, off_i + h:off_i + h + 1] for h in hs]
    dm = [jnp.where(tril, b + _row_form(i - b, eye), -jnp.inf) for b, i in zip(bc, ic)]

    if R == 1:
        m_rows = [m_ref[0, :, h:h + 1] for h in hs]
    else:
        m_rows = []
        for h in hs:
            mr = jnp.zeros((RT, 1), F32)
            for s in range(R):
                mr = jnp.where(rowseq == s, m_ref[s, :, h:h + 1], mr)
            m_rows.append(mr)
    m_new = [jnp.maximum(b + m, jnp.max(d, axis=1, keepdims=True)) for b, m, d in zip(bc, m_rows, dm)]
    inter = [jnp.exp(b + m - mn) for b, m, mn in zip(bc, m_rows, m_new)]
    smat = [_dot_nt(a, b) * jnp.exp(d - mn) for a, b, d, mn in zip(q, k, dm, m_new)]

    if R == 1:
        qc = [_dot(q[h], c_ref[0, h]) for h in hs]
        qn = [jnp.sum(q[h] * n_ref[0, h:h + 1, :], axis=1, keepdims=True) for h in hs]
    else:
        qc, qn = [], []
        for h in hs:
            rows = [slice(s * T, (s + 1) * T) for s in range(R)]
            qc.append(jnp.concatenate([_dot(q[h][r], c_ref[s, h]) for s, r in enumerate(rows)], axis=0))
            qn.append(jnp.concatenate(
                [jnp.sum(q[h][r] * n_ref[s, h:h + 1, :], axis=1, keepdims=True)
                 for s, r in enumerate(rows)], axis=0))
    num = [i * c + _dot(s, a) for i, c, s, a in zip(inter, qc, smat, v)]
    den = [i * n + jnp.sum(s, axis=1, keepdims=True) for i, n, s in zip(inter, qn, smat)]
    hh = [a / jnp.maximum(jnp.abs(b), jnp.exp(-mn)) for a, b, mn in zip(num, den, m_new)]

    b_last = [_at_last(_row_form(b, eye), lastsel) for b in bc]
    m_end = [_at_last(_row_form(mn, eye), lastsel) for mn in m_new]
    kw = [a * jnp.exp(bl - b + i - me) for a, bl, b, i, me in zip(k, b_last, bc, ic, m_end)]
    cs = [jnp.exp(bl + m - me) for bl, m, me in zip(b_last, m_rows, m_end)]
    for h in hs:
        for s in range(R):
            kw_s = kw[h] if R == 1 else jnp.where(rowseq == s, kw[h], 0.0)
            cs_s = cs[h][s * T:s * T + 1, :]
            c_ref[s, h] = cs_s * c_ref[s, h] + _dot_tn(kw_s, v[h])
            n_ref[s, h:h + 1, :] = cs_s * n_ref[s, h:h + 1, :] + jnp.sum(kw_s, axis=0, keepdims=True)
            m_ref[s, :, h:h + 1] = m_end[h][s * T:s * T + 1, :]

    for h in hs:
        og = oa[:, h * DV:(h + 1) * DV]
        out = _rms(hh[h], ng_ref[:, h * DV:(h + 1) * DV]) * _sigmoid(og)
        o_ref[:, h * DV:(h + 1) * DV] = out.astype(o_ref.dtype)


def _mlstm(proj3, small3, c0, n0, m0, par, norm_g, *, R, T, N, col0, off_i, off_f):
    B, H, DK, DV = c0.shape
    QK, VW = H * DK, H * DV
    assert col0 % QK == 0 and (col0 + 2 * QK) % VW == 0 and QK % LANES == 0
    rows = proj3.shape[0] * T
    kern = functools.partial(_mlstm_kernel, R=R, T=T, H=H, DK=DK, DV=DV, off_i=off_i, off_f=off_f)
    qb = col0 // QK
    vb = (col0 + 2 * QK) // VW
    state_specs = [
        pl.BlockSpec((R, H, DK, DV), lambda b, n: (b, 0, 0, 0)),
        pl.BlockSpec((R, H, DK), lambda b, n: (b, 0, 0)),
        pl.BlockSpec((R, 1, H), lambda b, n: (b, 0, 0)),
    ]
    return pl.pallas_call(
        kern,
        grid=(B // R, N),
        in_specs=[
            pl.BlockSpec((R, T, QK), lambda b, n: (b * N + n, 0, qb)),
            pl.BlockSpec((R, T, QK), lambda b, n: (b * N + n, 0, qb + 1)),
            pl.BlockSpec((R, T, VW), lambda b, n: (b * N + n, 0, vb)),
            pl.BlockSpec((R, T, VW), lambda b, n: (b * N + n, 0, vb + 1)),
            pl.BlockSpec((R, T, LANES), lambda b, n: (b * N + n, 0, 0)),
            *state_specs,
            pl.BlockSpec((SUBLANES, LANES), lambda b, n: (0, 0)),
            pl.BlockSpec((1, VW), lambda b, n: (0, 0)),
        ],
        out_specs=[pl.BlockSpec((R * T, VW), lambda b, n: (b * N + n, 0)), *state_specs],
        out_shape=[jax.ShapeDtypeStruct((rows, VW), BF16),
                   jax.ShapeDtypeStruct(c0.shape, F32),
                   jax.ShapeDtypeStruct(n0.shape, F32),
                   jax.ShapeDtypeStruct(m0.shape, F32)],
        compiler_params=_cparams(("parallel", "arbitrary")),
        name="mlstm",
    )(proj3, proj3, proj3, proj3, small3, c0, n0, m0, par, norm_g.reshape(1, VW))


def _outproj_kernel(x_ref, a_ref, b_ref, wa_ref, wb_ref, o_ref):
    o_ref[...] = (x_ref[...] + jnp.dot(a_ref[...], wa_ref[...], preferred_element_type=F32)
                  + jnp.dot(b_ref[...], wb_ref[...], preferred_element_type=F32))


def _outproj(x, a, b, w_out):
    m, d = x.shape
    ka, kb = a.shape[1], b.shape[1]
    assert ka == kb
    tm = _pick(m, (512, 256, 128, 64, 32, 16, 8))
    tn = _pick(d, (1024, 512, 256, 128))
    return pl.pallas_call(
        _outproj_kernel,
        grid=(m // tm, d // tn),
        in_specs=[
            pl.BlockSpec((tm, tn), lambda i, j: (i, j)),
            pl.BlockSpec((tm, ka), lambda i, j: (i, 0)),
            pl.BlockSpec((tm, kb), lambda i, j: (i, 0)),
            pl.BlockSpec((ka, tn), lambda i, j: (0, j)),
            pl.BlockSpec((kb, tn), lambda i, j: (1, j)),
        ],
        out_specs=pl.BlockSpec((tm, tn), lambda i, j: (i, j)),
        out_shape=jax.ShapeDtypeStruct((m, d), F32),
        compiler_params=_cparams(("parallel", "parallel")),
        name="outproj",
    )(x, a, b, w_out, w_out)


def _ffn_kernel(x_ref, g_ref, wg_ref, wu_ref, cwg_ref, cwu_ref, cbg_ref, cbu_ref, wd_ref,
                stg_ref, stu_ref, gf_ref, y_ref, nsg_ref, nsu_ref, h_ref, ext_ref, *carry,
                tm, shift, pad, tps, width, final_norm):
    i = pl.program_id(0)
    j = pl.program_id(1)
    hist = (width - 1) * shift

    @pl.when(j == 0)
    def _():
        h_ref[...] = _rms(x_ref[...], g_ref[...]).astype(BF16)
        y_ref[...] = jnp.zeros_like(y_ref)

    h = h_ref[...]
    conv = []
    for idx, (w_ref, cw_ref, cb_ref, st_ref, ns_ref) in enumerate(
            ((wg_ref, cwg_ref, cbg_ref, stg_ref, nsg_ref), (wu_ref, cwu_ref, cbu_ref, stu_ref, nsu_ref))):
        u = jnp.dot(h, w_ref[...], preferred_element_type=F32)
        ext_ref[idx, pad:pad + tm, :] = u
        if tps == 1:
            ext_ref[idx, pad - hist:pad, :] = st_ref[0]
        else:
            @pl.when(i % tps == 0)
            def _():
                ext_ref[idx, pad - hist:pad, :] = st_ref[0]

            @pl.when(i % tps != 0)
            def _():
                ext_ref[idx, pad - hist:pad, :] = carry[0][j, idx]
        cw = cw_ref[...]
        c = u * cw[width - 1:width, :] + cb_ref[...]
        for t in range(width - 1):
            off = pad - hist + t * shift
            c = c + ext_ref[idx, off:off + tm, :] * cw[t:t + 1, :]
        new = u[tm - hist:, :]
        ns_ref[0] = new
        if tps != 1:
            carry[0][j, idx] = new
        conv.append(c)

    act = (conv[0] * _sigmoid(conv[0]) * conv[1]).astype(BF16)
    y_ref[...] += jnp.dot(act, wd_ref[...], preferred_element_type=F32)

    @pl.when(j == pl.num_programs(1) - 1)
    def _():
        xo = x_ref[...] + y_ref[...]
        y_ref[...] = _rms(xo, gf_ref[...]) if final_norm else xo


def _ffn(x, norm_g, w_up, conv_w, conv_b, w_down, state, final_g, *, tm, shift, tps, final_norm):
    m, d = x.shape
    ff = w_down.shape[0]
    width = conv_w.shape[0]
    hist = (width - 1) * shift
    pad = -(-hist // SUBLANES) * SUBLANES
    tf = _pick(ff, (512, 256, 128))
    nj = ff // tf
    conv_b = conv_b.reshape(1, 2 * ff)
    kern = functools.partial(_ffn_kernel, tm=tm, shift=shift, pad=pad, tps=tps, width=width,
                             final_norm=final_norm)
    scratch = [pltpu.VMEM((tm, d), BF16), pltpu.VMEM((2, pad + tm, tf), F32)]
    if tps != 1:
        scratch.append(pltpu.VMEM((nj, 2, hist, tf), F32))
    st_shape = jax.ShapeDtypeStruct((m // tm, hist, ff), F32)
    y, nsg, nsu = pl.pallas_call(
        kern,
        grid=(m // tm, nj),
        in_specs=[
            pl.BlockSpec((tm, d), lambda i, j: (i, 0)),
            pl.BlockSpec((1, d), lambda i, j: (0, 0)),
            pl.BlockSpec((d, tf), lambda i, j: (0, j)),
            pl.BlockSpec((d, tf), lambda i, j: (0, nj + j)),
            pl.BlockSpec((width, tf), lambda i, j: (0, j)),
            pl.BlockSpec((width, tf), lambda i, j: (0, nj + j)),
            pl.BlockSpec((1, tf), lambda i, j: (0, j)),
            pl.BlockSpec((1, tf), lambda i, j: (0, nj + j)),
            pl.BlockSpec((tf, d), lambda i, j: (j, 0)),
            pl.BlockSpec((1, hist, tf), lambda i, j: (i // tps, 0, j)),
            pl.BlockSpec((1, hist, tf), lambda i, j: (i // tps, 0, nj + j)),
            pl.BlockSpec((1, d), lambda i, j: (0, 0)),
        ],
        out_specs=[
            pl.BlockSpec((tm, d), lambda i, j: (i, 0)),
            pl.BlockSpec((1, hist, tf), lambda i, j: (i, 0, j)),
            pl.BlockSpec((1, hist, tf), lambda i, j: (i, 0, j)),
        ],
        out_shape=[jax.ShapeDtypeStruct((m, d), F32), st_shape, st_shape],
        scratch_shapes=scratch,
        compiler_params=_cparams(("arbitrary", "arbitrary")),
        name="convffn",
    )(x, norm_g.reshape(1, d), w_up, w_up, conv_w, conv_w, conv_b, conv_b, w_down, state, state,
      final_g.reshape(1, d))
    return y, nsg[tps - 1::tps], nsu[tps - 1::tps]


def _layer(x, st, p, final_g, final_norm, decode):
    conv_buf, s0, c0, n0, m0, ffn_buf = st
    B, L, D = x.shape
    Hd, Hm = s0.shape[1], c0.shape[1]
    dn_qkv = conv_buf.shape[-1]
    dn_vw = Hd * s0.shape[3]
    ff = p["w_down"].shape[0]

    T = next((c for c in (MIX_CHUNK, CHUNK) if L % c == 0), L)
    N = L // T
    R = _pick(B, (8, 4, 2, 1)) if decode else 1
    if R * T > 128:
        R = 1

    x2 = x.reshape(B * L, D)
    big, small = _inproj(x2, p["norm_mix_g"], p["w_big"], p["w_small"])
    big3 = big.reshape(B * N, T, big.shape[1])
    small3 = small.reshape(B * N, T, LANES)

    o_dn, s_new = _gdn(big3, small3, conv_buf, s0, p["dn_conv_w"], p["par"], p["dn_norm_g"],
                       R=R, T=T, N=N)
    o_ml, c_new, n_new, m_new = _mlstm(big3, small3, c0, n0, m0.reshape(B, 1, Hm), p["par"],
                                       p["ml_norm_g"], R=R, T=T, N=N, col0=dn_qkv + dn_vw,
                                       off_i=2 * Hd, off_f=2 * Hd + Hm)
    width = p["dn_conv_w"].shape[0]
    big4 = big.reshape(B, L, big.shape[1])
    if L >= width - 1:
        conv_new = big4[:, L - (width - 1):, :dn_qkv]
    else:
        conv_new = jnp.concatenate([conv_buf, big4[:, :, :dn_qkv]], axis=1)[:, L:]

    x1 = _outproj(x2, o_dn, o_ml, p["w_out"])

    fw = p["ffn_conv_w"].shape[0]
    if decode:
        x1t = x1.reshape(B, L, D).transpose(1, 0, 2).reshape(L * B, D)
        state = ffn_buf.transpose(1, 0, 2).reshape(1, (fw - 1) * B, 2 * ff)
        y, nsg, nsu = _ffn(x1t, p["norm_ffn_g"], p["w_up"], p["ffn_conv_w"], p["ffn_conv_b"],
                           p["w_down"], state, final_g, tm=L * B, shift=B, tps=1,
                           final_norm=final_norm)
        y = y.reshape(L, B, D).transpose(1, 0, 2)
        ffn_new = jnp.concatenate([nsg, nsu], axis=-1).reshape(fw - 1, B, 2 * ff).transpose(1, 0, 2)
    else:
        tm = _pick(L, (512, 256, 128, 64, 32, 16, 8))
        y, nsg, nsu = _ffn(x1, p["norm_ffn_g"], p["w_up"], p["ffn_conv_w"], p["ffn_conv_b"],
                           p["w_down"], ffn_buf, final_g, tm=tm, shift=1, tps=L // tm,
                           final_norm=final_norm)
        y = y.reshape(B, L, D)
        ffn_new = jnp.concatenate([nsg, nsu], axis=-1)
    return y, (conv_new, s_new, c_new, n_new, m_new.reshape(B, Hm), ffn_new)


def _prep_params(l, Hd, Hm, dn_qkv, dn_vw, ml_qk, ml_vw, norm_mix_g, w_in, dn_conv_w, dn_A_log,
                 dn_dt_bias, dn_norm_g, ml_i_bias, ml_f_bias, ml_norm_g, w_out, norm_ffn_g, w_up,
                 ffn_conv_w, ffn_conv_b, w_down):
    w = w_in[l]
    c0 = dn_qkv + dn_vw
    c1 = c0 + 2 * Hd
    c2 = c1 + 2 * ml_qk + 2 * ml_vw
    nsmall = 2 * Hd + 2 * Hm
    assert nsmall <= LANES and w.shape[1] == c2 + 2 * Hm
    w_big = jnp.concatenate([w[:, :c0], w[:, c1:c2]], axis=1).astype(BF16)
    w_small = jnp.concatenate([w[:, c0:c1], w[:, c2:]], axis=1)
    w_small = jnp.pad(w_small, ((0, 0), (0, LANES - nsmall))).astype(BF16)
    z = lambda n: jnp.zeros((n,), F32)
    bias = jnp.concatenate([z(Hd), dn_dt_bias[l], ml_i_bias[l], ml_f_bias[l], z(LANES - nsmall)])
    alog = jnp.concatenate([z(Hd), dn_A_log[l], z(LANES - 2 * Hd)])
    par = jnp.zeros((SUBLANES, LANES), F32).at[0].set(bias).at[1].set(alog)
    return dict(norm_mix_g=norm_mix_g[l], w_big=w_big, w_small=w_small, dn_conv_w=dn_conv_w[l],
                par=par, dn_norm_g=dn_norm_g[l], ml_norm_g=ml_norm_g[l], w_out=w_out[l].astype(BF16),
                norm_ffn_g=norm_ffn_g[l], w_up=w_up[l].astype(BF16), ffn_conv_w=ffn_conv_w[l],
                ffn_conv_b=ffn_conv_b[l], w_down=w_down[l].astype(BF16))


def kernel(x_prompt, x_sample, state_dn_conv, state_dn_S, state_ml_C, state_ml_n, state_ml_m,
           state_ffn_conv, norm_mix_g, w_in, dn_conv_w, dn_A_log, dn_dt_bias, dn_norm_g,
           ml_i_bias, ml_f_bias, ml_norm_g, w_out, norm_ffn_g, w_up, ffn_conv_w, ffn_conv_b,
           w_down, norm_final_g):
    states = (state_dn_conv, state_dn_S, state_ml_C, state_ml_n, state_ml_m, state_ffn_conv)
    depth = w_in.shape[0]
    batch = x_prompt.shape[0]
    Hd, dk, dv = state_dn_S.shape[2:]
    Hm, mdk, mdv = state_ml_C.shape[2:]
    xp, xs = x_prompt, x_sample
    p_new = [[] for _ in states]
    s_new = [[] for _ in states]
    for l in range(depth):
        p = _prep_params(l, Hd, Hm, state_dn_conv.shape[-1], Hd * dv, Hm * mdk, Hm * mdv,
                         norm_mix_g, w_in, dn_conv_w, dn_A_log, dn_dt_bias, dn_norm_g, ml_i_bias,
                         ml_f_bias, ml_norm_g, w_out, norm_ffn_g, w_up, ffn_conv_w, ffn_conv_b,
                         w_down)
        last = l == depth - 1
        st_p = tuple(jnp.zeros((batch,) + s.shape[2:], s.dtype) for s in states)
        st_s = tuple(s[l] for s in states)
        xp, np_st = _layer(xp, st_p, p, norm_final_g, last, decode=False)
        xs, ns_st = _layer(xs, st_s, p, norm_final_g, last, decode=True)
        for i in range(len(states)):
            p_new[i].append(np_st[i])
            s_new[i].append(ns_st[i])
    outs_p = tuple(jnp.stack(a, axis=0) for a in p_new)
    outs_s = tuple(jnp.stack(a, axis=0) for a in s_new)
    return (xp, xs) + outs_p + outs_s
```

```python
import functools

import jax
import jax.numpy as jnp
from jax import lax
from jax.experimental import pallas as pl
from jax.experimental.pallas import tpu as pltpu

EPS = 1e-6
CHUNK = 64
MIX_CHUNK = 128
F32 = jnp.float32
BF16 = jnp.bfloat16
LANES = 128
SUBLANES = 8
VMEM_LIMIT = 56 * 1024 * 1024


def _cparams(sem, vmem=VMEM_LIMIT):
    return pltpu.CompilerParams(dimension_semantics=sem, vmem_limit_bytes=vmem)


def _dot(a, b):
    return jnp.dot(a.astype(BF16), b.astype(BF16), preferred_element_type=F32)


def _dot_nt(a, b):
    return lax.dot_general(a.astype(BF16), b.astype(BF16), (((1,), (1,)), ((), ())),
                           preferred_element_type=F32)


def _dot_tn(a, b):
    return lax.dot_general(a.astype(BF16), b.astype(BF16), (((0,), (0,)), ((), ())),
                           preferred_element_type=F32)


def _mask_dot_exact(mask_bf16, x):
    hi = x.astype(BF16)
    r1 = x - hi.astype(F32)
    mid = r1.astype(BF16)
    lo = (r1 - mid.astype(F32)).astype(BF16)
    d = functools.partial(jnp.dot, preferred_element_type=F32)
    return (d(mask_bf16, hi) + d(mask_bf16, mid)) + d(mask_bf16, lo)


def _softplus(x):
    return jnp.maximum(x, 0.0) + jnp.log1p(jnp.exp(-jnp.abs(x)))


def _sigmoid(x):
    return 1.0 / (1.0 + jnp.exp(-x))


def _rms(x, g):
    return x * lax.rsqrt(jnp.mean(x * x, axis=-1, keepdims=True) + EPS) * g


def _log2(n):
    assert n & (n - 1) == 0 and n > 0, n
    return n.bit_length() - 1


def _pick(n, cands):
    for c in cands:
        if n % c == 0:
            return c
    return n


def _inproj_kernel(x_ref, g_ref, wb_ref, ws_ref, ob_ref, os_ref, h_ref):
    @pl.when(pl.program_id(1) == 0)
    def _():
        h = _rms(x_ref[...], g_ref[...]).astype(BF16)
        h_ref[...] = h
        os_ref[...] = jnp.dot(h, ws_ref[...], preferred_element_type=F32)

    ob_ref[...] = jnp.dot(h_ref[...], wb_ref[...], preferred_element_type=F32)


def _inproj(x, g, w_big, w_small):
    m, d = x.shape
    nb = w_big.shape[1]
    tm = _pick(m, (1024, 512, 256, 128, 64, 32, 16, 8))
    tn = _pick(nb, (1024, 512, 256, 128))
    return pl.pallas_call(
        _inproj_kernel,
        grid=(m // tm, nb // tn),
        in_specs=[
            pl.BlockSpec((tm, d), lambda i, j: (i, 0)),
            pl.BlockSpec((1, d), lambda i, j: (0, 0)),
            pl.BlockSpec((d, tn), lambda i, j: (0, j)),
            pl.BlockSpec((d, LANES), lambda i, j: (0, 0)),
        ],
        out_specs=[
            pl.BlockSpec((tm, tn), lambda i, j: (i, j)),
            pl.BlockSpec((tm, LANES), lambda i, j: (i, 0)),
        ],
        out_shape=[jax.ShapeDtypeStruct((m, nb), F32), jax.ShapeDtypeStruct((m, LANES), F32)],
        scratch_shapes=[pltpu.VMEM((tm, d), BF16)],
        compiler_params=_cparams(("parallel", "arbitrary")),
        name="inproj",
    )(x, g.reshape(1, d), w_big, w_small)


def _seq_masks(R, T):
    RT = R * T
    sh = _log2(T)
    ri = lax.broadcasted_iota(jnp.int32, (RT, RT), 0)
    ci = lax.broadcasted_iota(jnp.int32, (RT, RT), 1)
    same = (ri >> sh) == (ci >> sh)
    tril = same & (ci <= ri)
    strict = same & (ci < ri)
    eye = ri == ci
    lastsel = same & ((ci & (T - 1)) == T - 1)
    rowseq = lax.broadcasted_iota(jnp.int32, (RT, 1), 0) >> sh
    return ri, ci, tril, strict, eye, lastsel, rowseq


def _row_form(col, eye):
    return jnp.sum(jnp.where(eye, col, 0.0), axis=0, keepdims=True)


def _at_last(row, lastsel):
    return jnp.sum(jnp.where(lastsel, row, 0.0), axis=1, keepdims=True)


def _unit_lower_inverses(lms, ri, ci, T):
    ds = None
    s = 1
    while s < T:
        sh = _log2(2 * s)
        blk = ((ri >> sh) == (ci >> sh)) & ((ri & (2 * s - 1)) >= s) & ((ci & (2 * s - 1)) < s)
        ms = [jnp.where(blk, lm, 0.0) for lm in lms]
        if ds is None:
            eye = jnp.where(ri == ci, 1.0, 0.0)
            ds = [eye - m for m in ms]
        else:
            dms = [_dot(d, m) for d, m in zip(ds, ms)]
            ds = [d - _dot(dm, d) for d, dm in zip(ds, dms)]
        s *= 2
    return ds


def _causal_conv(ext_ref, x_ref, st_ref, cw_ref, first, T, width):
    lo = SUBLANES - (width - 1)

    @pl.when(first)
    def _():
        ext_ref[:, lo:SUBLANES, :] = st_ref[...]

    ext_ref[:, SUBLANES:SUBLANES + T, :] = x_ref[...]
    cw = cw_ref[...]
    acc = None
    for j in range(width):
        term = ext_ref[:, lo + j:lo + j + T, :] * cw[j:j + 1, :]
        acc = term if acc is None else acc + term
    ext_ref[:, lo:SUBLANES, :] = ext_ref[:, lo + T:SUBLANES + T, :]
    return acc


def _gdn_kernel(qkv_ref, z_ref, sm_ref, cst_ref, s0_ref, cw_ref, par_ref, ng_ref,
                o_ref, s_ref, ext_ref, *, R, T, H, DK, DV, width):
    RT = R * T
    QK = H * DK
    first = pl.program_id(1) == 0

    @pl.when(first)
    def _():
        s_ref[...] = s0_ref[...]

    y = _causal_conv(ext_ref, qkv_ref, cst_ref, cw_ref, first, T, width)
    y = y.reshape(RT, y.shape[-1])
    y = y * _sigmoid(y)
    z = z_ref[...].reshape(RT, H * DV)

    sm = sm_ref[...].reshape(RT, LANES)
    beta_all = _sigmoid(sm)
    g_all = -jnp.exp(par_ref[1:2, :]) * _softplus(sm + par_ref[0:1, :])

    ri, ci, tril, strict, eye, lastsel, rowseq = _seq_masks(R, T)
    g_cum = _mask_dot_exact(jnp.where(tril, 1.0, 0.0).astype(BF16), g_all)

    hs = range(H)
    q = [y[:, h * DK:(h + 1) * DK] for h in hs]
    k = [y[:, QK + h * DK:QK + (h + 1) * DK] for h in hs]
    v = [y[:, 2 * QK + h * DV:2 * QK + (h + 1) * DV] for h in hs]
    q = [a * lax.rsqrt(jnp.sum(a * a, axis=-1, keepdims=True) + EPS) * (DK ** -0.5) for a in q]
    k = [a * lax.rsqrt(jnp.sum(a * a, axis=-1, keepdims=True) + EPS) for a in k]
    beta = [beta_all[:, h:h + 1] for h in hs]
    gc = [g_cum[:, H + h:H + h + 1] for h in hs]
    gr = [_row_form(a, eye) for a in gc]
    glast = [_at_last(a, lastsel) for a in gr]
    decay = [jnp.exp(jnp.where(tril, c - r, -jnp.inf)) for c, r in zip(gc, gr)]
    kb = [a * b for a, b in zip(k, beta)]
    lm = [jnp.where(strict, _dot_nt(a, b) * d, 0.0) for a, b, d in zip(kb, k, decay)]
    tinv = _unit_lower_inverses(lm, ri, ci, T)
    eg = [jnp.exp(a) for a in gc]
    uw = [_dot(t, jnp.concatenate([a * b, c * e], axis=1))
          for t, a, b, c, e in zip(tinv, v, beta, kb, eg)]
    u = [a[:, :DV] for a in uw]
    w = [a[:, DV:] for a in uw]
    qk = [_dot_nt(a, b) * d for a, b, d in zip(q, k, decay)]
    qe = [a * e for a, e in zip(q, eg)]
    kd = [a * jnp.exp(l - c) for a, l, c in zip(k, glast, gc)]
    gl = [jnp.exp(a) for a in glast]

    if R == 1:
        s_old = [s_ref[0, h] for h in hs]
        ws = [_dot(jnp.concatenate([a, b], axis=0), c) for a, b, c in zip(w, qe, s_old)]
        v_new = [a - b[:RT] for a, b in zip(u, ws)]
        o = [a[RT:] + _dot(b, c) for a, b, c in zip(ws, qk, v_new)]
        for h in hs:
            s_ref[0, h] = s_old[h] * gl[h][0:1, :] + _dot_tn(kd[h], v_new[h])
    else:
        ws = [[_dot(jnp.concatenate([w[h][s * T:(s + 1) * T], qe[h][s * T:(s + 1) * T]], axis=0),
                    s_ref[s, h]) for s in range(R)] for h in hs]
        v_new = [u[h] - jnp.concatenate([a[:T] for a in ws[h]], axis=0) for h in hs]
        o = [jnp.concatenate([a[T:] for a in ws[h]], axis=0) + _dot(qk[h], v_new[h]) for h in hs]
        for h in hs:
            for s in range(R):
                kd_s = jnp.where(rowseq == s, kd[h], 0.0)
                s_ref[s, h] = s_ref[s, h] * gl[h][s * T:s * T + 1, :] + _dot_tn(kd_s, v_new[h])

    for h in hs:
        zh = z[:, h * DV:(h + 1) * DV]
        out = _rms(o[h], ng_ref[...]) * (zh * _sigmoid(zh))
        o_ref[:, h * DV:(h + 1) * DV] = out.astype(o_ref.dtype)


def _gdn(proj3, small3, conv_state, s0, conv_w, par, norm_g, *, R, T, N):
    B, H, DK, DV = s0.shape
    QKV = conv_state.shape[-1]
    width = conv_w.shape[0]
    assert QKV == 2 * H * DK + H * DV and QKV % LANES == 0 and (H * DV) % LANES == 0
    assert QKV % (H * DV) == 0
    zblk = QKV // (H * DV)
    rows = proj3.shape[0] * T
    kern = functools.partial(_gdn_kernel, R=R, T=T, H=H, DK=DK, DV=DV, width=width)
    return pl.pallas_call(
        kern,
        grid=(B // R, N),
        in_specs=[
            pl.BlockSpec((R, T, QKV), lambda b, n: (b * N + n, 0, 0)),
            pl.BlockSpec((R, T, H * DV), lambda b, n: (b * N + n, 0, zblk)),
            pl.BlockSpec((R, T, LANES), lambda b, n: (b * N + n, 0, 0)),
            pl.BlockSpec((R, width - 1, QKV), lambda b, n: (b, 0, 0)),
            pl.BlockSpec((R, H, DK, DV), lambda b, n: (b, 0, 0, 0)),
            pl.BlockSpec((width, QKV), lambda b, n: (0, 0)),
            pl.BlockSpec((SUBLANES, LANES), lambda b, n: (0, 0)),
            pl.BlockSpec((1, DV), lambda b, n: (0, 0)),
        ],
        out_specs=[
            pl.BlockSpec((R * T, H * DV), lambda b, n: (b * N + n, 0)),
            pl.BlockSpec((R, H, DK, DV), lambda b, n: (b, 0, 0, 0)),
        ],
        out_shape=[jax.ShapeDtypeStruct((rows, H * DV), BF16),
                   jax.ShapeDtypeStruct((B, H, DK, DV), F32)],
        scratch_shapes=[pltpu.VMEM((R, SUBLANES + T, QKV), F32)],
        compiler_params=_cparams(("parallel", "arbitrary")),
        name="gdn",
    )(proj3, proj3, small3, conv_state, s0, conv_w, par, norm_g.reshape(1, DV))


def _mlstm_kernel(q_ref, k_ref, v_ref, og_ref, sm_ref, c0_ref, n0_ref, m0_ref, par_ref, ng_ref,
                  o_ref, c_ref, n_ref, m_ref, *, R, T, H, DK, DV, off_i, off_f):
    RT = R * T

    @pl.when(pl.program_id(1) == 0)
    def _():
        c_ref[...] = c0_ref[...]
        n_ref[...] = n0_ref[...]
        m_ref[...] = m0_ref[...]

    pre = sm_ref[...].reshape(RT, LANES) + par_ref[0:1, :]
    logf_all = -_softplus(-pre)
    ri, ci, tril, strict, eye, lastsel, rowseq = _seq_masks(R, T)
    b_all = _mask_dot_exact(jnp.where(tril, 1.0, 0.0).astype(BF16), logf_all)

    qa = q_ref[...].reshape(RT, H * DK)
    ka = k_ref[...].reshape(RT, H * DK)
    va = v_ref[...].reshape(RT, H * DV)
    oa = og_ref[...].reshape(RT, H * DV)

    hs = range(H)
    q = [qa[:, h * DK:(h + 1) * DK] * (DK ** -0.5) for h in hs]
    k = [ka[:, h * DK:(h + 1) * DK] for h in hs]
    v = [va[:, h * DV:(h + 1) * DV] for h in hs]
    bc = [b_all[:, off_f + h:off_f + h + 1] for h in hs]
    ic = [pre[:, off_i + h:off_i + h + 1] for h in hs]
    dm = [jnp.where(tril, b + _row_form(i - b, eye), -jnp.inf) for b, i in zip(bc, ic)]

    if R == 1:
        m_rows = [m_ref[0, :, h:h + 1] for h in hs]
    else:
        m_rows = []
        for h in hs:
            mr = jnp.zeros((RT, 1), F32)
            for s in range(R):
                mr = jnp.where(rowseq == s, m_ref[s, :, h:h + 1], mr)
            m_rows.append(mr)
    m_new = [jnp.maximum(b + m, jnp.max(d, axis=1, keepdims=True)) for b, m, d in zip(bc, m_rows, dm)]
    inter = [jnp.exp(b + m - mn) for b, m, mn in zip(bc, m_rows, m_new)]
    smat = [_dot_nt(a, b) * jnp.exp(d - mn) for a, b, d, mn in zip(q, k, dm, m_new)]

    if R == 1:
        qc = [_dot(q[h], c_ref[0, h]) for h in hs]
        qn = [jnp.sum(q[h] * n_ref[0, h:h + 1, :], axis=1, keepdims=True) for h in hs]
    else:
        qc, qn = [], []
        for h in hs:
            rows = [slice(s * T, (s + 1) * T) for s in range(R)]
            qc.append(jnp.concatenate([_dot(q[h][r], c_ref[s, h]) for s, r in enumerate(rows)], axis=0))
            qn.append(jnp.concatenate(
                [jnp.sum(q[h][r] * n_ref[s, h:h + 1, :], axis=1, keepdims=True)
                 for s, r in enumerate(rows)], axis=0))
    num = [i * c + _dot(s, a) for i, c, s, a in zip(inter, qc, smat, v)]
    den = [i * n + jnp.sum(s, axis=1, keepdims=True) for i, n, s in zip(inter, qn, smat)]
    hh = [a / jnp.maximum(jnp.abs(b), jnp.exp(-mn)) for a, b, mn in zip(num, den, m_new)]

    b_last = [_at_last(_row_form(b, eye), lastsel) for b in bc]
    m_end = [_at_last(_row_form(mn, eye), lastsel) for mn in m_new]
    kw = [a * jnp.exp(bl - b + i - me) for a, bl, b, i, me in zip(k, b_last, bc, ic, m_end)]
    cs = [jnp.exp(bl + m - me) for bl, m, me in zip(b_last, m_rows, m_end)]
    for h in hs:
        for s in range(R):
            kw_s = kw[h] if R == 1 else jnp.where(rowseq == s, kw[h], 0.0)
            cs_s = cs[h][s * T:s * T + 1, :]
            c_ref[s, h] = cs_s * c_ref[s, h] + _dot_tn(kw_s, v[h])
            n_ref[s, h:h + 1, :] = cs_s * n_ref[s, h:h + 1, :] + jnp.sum(kw_s, axis=0, keepdims=True)
            m_ref[s, :, h:h + 1] = m_end[h][s * T:s * T + 1, :]

    for h in hs:
        og = oa[:, h * DV:(h + 1) * DV]
        out = _rms(hh[h], ng_ref[:, h * DV:(h + 1) * DV]) * _sigmoid(og)
        o_ref[:, h * DV:(h + 1) * DV] = out.astype(o_ref.dtype)


def _mlstm(proj3, small3, c0, n0, m0, par, norm_g, *, R, T, N, col0, off_i, off_f):
    B, H, DK, DV = c0.shape
    QK, VW = H * DK, H * DV
    assert col0 % QK == 0 and (col0 + 2 * QK) % VW == 0 and QK % LANES == 0
    rows = proj3.shape[0] * T
    kern = functools.partial(_mlstm_kernel, R=R, T=T, H=H, DK=DK, DV=DV, off_i=off_i, off_f=off_f)
    qb = col0 // QK
    vb = (col0 + 2 * QK) // VW
    state_specs = [
        pl.BlockSpec((R, H, DK, DV), lambda b, n: (b, 0, 0, 0)),
        pl.BlockSpec((R, H, DK), lambda b, n: (b, 0, 0)),
        pl.BlockSpec((R, 1, H), lambda b, n: (b, 0, 0)),
    ]
    return pl.pallas_call(
        kern,
        grid=(B // R, N),
        in_specs=[
            pl.BlockSpec((R, T, QK), lambda b, n: (b * N + n, 0, qb)),
            pl.BlockSpec((R, T, QK), lambda b, n: (b * N + n, 0, qb + 1)),
            pl.BlockSpec((R, T, VW), lambda b, n: (b * N + n, 0, vb)),
            pl.BlockSpec((R, T, VW), lambda b, n: (b * N + n, 0, vb + 1)),
            pl.BlockSpec((R, T, LANES), lambda b, n: (b * N + n, 0, 0)),
            *state_specs,
            pl.BlockSpec((SUBLANES, LANES), lambda b, n: (0, 0)),
            pl.BlockSpec((1, VW), lambda b, n: (0, 0)),
        ],
        out_specs=[pl.BlockSpec((R * T, VW), lambda b, n: (b * N + n, 0)), *state_specs],
        out_shape=[jax.ShapeDtypeStruct((rows, VW), BF16),
                   jax.ShapeDtypeStruct(c0.shape, F32),
                   jax.ShapeDtypeStruct(n0.shape, F32),
                   jax.ShapeDtypeStruct(m0.shape, F32)],
        compiler_params=_cparams(("parallel", "arbitrary")),
        name="mlstm",
    )(proj3, proj3, proj3, proj3, small3, c0, n0, m0, par, norm_g.reshape(1, VW))


def _outproj_kernel(x_ref, a_ref, b_ref, wa_ref, wb_ref, o_ref):
    o_ref[...] = (x_ref[...] + jnp.dot(a_ref[...], wa_ref[...], preferred_element_type=F32)
                  + jnp.dot(b_ref[...], wb_ref[...], preferred_element_type=F32))


def _outproj(x, a, b, w_out):
    m, d = x.shape
    ka, kb = a.shape[1], b.shape[1]
    assert ka == kb
    tm = _pick(m, (512, 256, 128, 64, 32, 16, 8))
    return pl.pallas_call(
        _outproj_kernel,
        grid=(m // tm,),
        in_specs=[
            pl.BlockSpec((tm, d), lambda i: (i, 0)),
            pl.BlockSpec((tm, ka), lambda i: (i, 0)),
            pl.BlockSpec((tm, kb), lambda i: (i, 0)),
            pl.BlockSpec((ka, d), lambda i: (0, 0)),
            pl.BlockSpec((kb, d), lambda i: (1, 0)),
        ],
        out_specs=pl.BlockSpec((tm, d), lambda i: (i, 0)),
        out_shape=jax.ShapeDtypeStruct((m, d), F32),
        compiler_params=_cparams(("parallel",)),
        name="outproj",
    )(x, a, b, w_out, w_out)


def _ffn_kernel(x_ref, g_ref, wg_ref, wu_ref, cwg_ref, cwu_ref, cbg_ref, cbu_ref, wd_ref,
                stg_ref, stu_ref, gf_ref, y_ref, nsg_ref, nsu_ref, h_ref, act_ref, ext_ref, *carry,
                tm, shift, pad, tps, width, nj, final_norm):
    i = pl.program_id(0)
    j = pl.program_id(1)
    hist = (width - 1) * shift

    def activate_tile():
        conv = []
        for idx, (w_ref, cw_ref, cb_ref, st_ref, ns_ref) in enumerate(
                ((wg_ref, cwg_ref, cbg_ref, stg_ref, nsg_ref), (wu_ref, cwu_ref, cbu_ref, stu_ref, nsu_ref))):
            u = jnp.dot(h_ref[...], w_ref[...], preferred_element_type=F32)
            ext_ref[idx, pad:pad + tm, :] = u
            if tps == 1:
                ext_ref[idx, pad - hist:pad, :] = st_ref[0]
            else:
                ext_ref[idx, pad - hist:pad, :] = jnp.where(i % tps == 0, st_ref[0], carry[0][j, idx])
            cw = cw_ref[...]
            c = u * cw[width - 1:width, :] + cb_ref[...]
            for t in range(width - 1):
                off = pad - hist + t * shift
                c = c + ext_ref[idx, off:off + tm, :] * cw[t:t + 1, :]
            new = u[tm - hist:, :]
            ns_ref[0] = new
            if tps != 1:
                carry[0][j, idx] = new
            conv.append(c)
        act_ref[...] = (conv[0] * _sigmoid(conv[0]) * conv[1]).astype(BF16)

    def down_tile():
        return jnp.dot(act_ref[...], wd_ref[...], preferred_element_type=F32)

    @pl.when(j == 0)
    def _():
        h_ref[...] = _rms(x_ref[...], g_ref[...]).astype(BF16)
        y_ref[...] = jnp.zeros_like(y_ref)
        activate_tile()

    @pl.when((j > 0) & (j < nj))
    def _():
        y_ref[...] += down_tile()
        activate_tile()

    @pl.when(j == nj)
    def _():
        xo = x_ref[...] + y_ref[...] + down_tile()
        y_ref[...] = _rms(xo, gf_ref[...]) if final_norm else xo


def _ffn(x, norm_g, w_up, conv_w, conv_b, w_down, state, final_g, *, tm, shift, tps, final_norm):
    m, d = x.shape
    ff = w_down.shape[0]
    width = conv_w.shape[0]
    hist = (width - 1) * shift
    pad = -(-hist // SUBLANES) * SUBLANES
    tf = _pick(ff, (512, 256, 128))
    nj = ff // tf
    conv_b = conv_b.reshape(1, 2 * ff)
    kern = functools.partial(_ffn_kernel, tm=tm, shift=shift, pad=pad, tps=tps, width=width, nj=nj,
                             final_norm=final_norm)
    scratch = [pltpu.VMEM((tm, d), BF16), pltpu.VMEM((tm, tf), BF16), pltpu.VMEM((2, pad + tm, tf), F32)]
    cur = lambda j: jnp.minimum(j, nj - 1)
    prv = lambda j: jnp.maximum(j - 1, 0)
    if tps != 1:
        scratch.append(pltpu.VMEM((nj, 2, hist, tf), F32))
    st_shape = jax.ShapeDtypeStruct((m // tm, hist, ff), F32)
    y, nsg, nsu = pl.pallas_call(
        kern,
        grid=(m // tm, nj + 1),
        in_specs=[
            pl.BlockSpec((tm, d), lambda i, j: (i, 0)),
            pl.BlockSpec((1, d), lambda i, j: (0, 0)),
            pl.BlockSpec((d, tf), lambda i, j: (0, cur(j))),
            pl.BlockSpec((d, tf), lambda i, j: (0, nj + cur(j))),
            pl.BlockSpec((width, tf), lambda i, j: (0, cur(j))),
            pl.BlockSpec((width, tf), lambda i, j: (0, nj + cur(j))),
            pl.BlockSpec((1, tf), lambda i, j: (0, cur(j))),
            pl.BlockSpec((1, tf), lambda i, j: (0, nj + cur(j))),
            pl.BlockSpec((tf, d), lambda i, j: (prv(j), 0)),
            pl.BlockSpec((1, hist, tf), lambda i, j: (i // tps, 0, cur(j))),
            pl.BlockSpec((1, hist, tf), lambda i, j: (i // tps, 0, nj + cur(j))),
            pl.BlockSpec((1, d), lambda i, j: (0, 0)),
        ],
        out_specs=[
            pl.BlockSpec((tm, d), lambda i, j: (i, 0)),
            pl.BlockSpec((1, hist, tf), lambda i, j: (i, 0, cur(j))),
            pl.BlockSpec((1, hist, tf), lambda i, j: (i, 0, cur(j))),
        ],
        out_shape=[jax.ShapeDtypeStruct((m, d), F32), st_shape, st_shape],
        scratch_shapes=scratch,
        compiler_params=_cparams(("arbitrary", "arbitrary")),
        name="convffn",
    )(x, norm_g.reshape(1, d), w_up, w_up, conv_w, conv_w, conv_b, conv_b, w_down, state, state,
      final_g.reshape(1, d))
    return y, nsg[tps - 1::tps], nsu[tps - 1::tps]


def _layer(x, st, p, final_g, final_norm, decode):
    conv_buf, s0, c0, n0, m0, ffn_buf = st
    B, L, D = x.shape
    Hd, Hm = s0.shape[1], c0.shape[1]
    dn_qkv = conv_buf.shape[-1]
    dn_vw = Hd * s0.shape[3]
    ff = p["w_down"].shape[0]

    T = next((c for c in (MIX_CHUNK, CHUNK) if L % c == 0), L)
    N = L // T
    R = _pick(B, (8, 4, 2, 1)) if decode else 1
    if R * T > 128:
        R = 1

    x2 = x.reshape(B * L, D)
    big, small = _inproj(x2, p["norm_mix_g"], p["w_big"], p["w_small"])
    big3 = big.reshape(B * N, T, big.shape[1])
    small3 = small.reshape(B * N, T, LANES)

    o_dn, s_new = _gdn(big3, small3, conv_buf, s0, p["dn_conv_w"], p["par"], p["dn_norm_g"],
                       R=R, T=T, N=N)
    o_ml, c_new, n_new, m_new = _mlstm(big3, small3, c0, n0, m0.reshape(B, 1, Hm), p["par"],
                                       p["ml_norm_g"], R=R, T=T, N=N, col0=dn_qkv + dn_vw,
                                       off_i=2 * Hd, off_f=2 * Hd + Hm)
    width = p["dn_conv_w"].shape[0]
    big4 = big.reshape(B, L, big.shape[1])
    if L >= width - 1:
        conv_new = big4[:, L - (width - 1):, :dn_qkv]
    else:
        conv_new = jnp.concatenate([conv_buf, big4[:, :, :dn_qkv]], axis=1)[:, L:]

    x1 = _outproj(x2, o_dn, o_ml, p["w_out"])

    fw = p["ffn_conv_w"].shape[0]
    if decode:
        x1t = x1.reshape(B, L, D).transpose(1, 0, 2).reshape(L * B, D)
        state = ffn_buf.transpose(1, 0, 2).reshape(1, (fw - 1) * B, 2 * ff)
        y, nsg, nsu = _ffn(x1t, p["norm_ffn_g"], p["w_up"], p["ffn_conv_w"], p["ffn_conv_b"],
                           p["w_down"], state, final_g, tm=L * B, shift=B, tps=1,
                           final_norm=final_norm)
        y = y.reshape(L, B, D).transpose(1, 0, 2)
        ffn_new = jnp.concatenate([nsg, nsu], axis=-1).reshape(fw - 1, B, 2 * ff).transpose(1, 0, 2)
    else:
        tm = _pick(L, (512, 256, 128, 64, 32, 16, 8))
        y, nsg, nsu = _ffn(x1, p["norm_ffn_g"], p["w_up"], p["ffn_conv_w"], p["ffn_conv_b"],
                           p["w_down"], ffn_buf, final_g, tm=tm, shift=1, tps=L // tm,
                           final_norm=final_norm)
        y = y.reshape(B, L, D)
        ffn_new = jnp.concatenate([nsg, nsu], axis=-1)
    return y, (conv_new, s_new, c_new, n_new, m_new.reshape(B, Hm), ffn_new)


def _prep_params(l, Hd, Hm, dn_qkv, dn_vw, ml_qk, ml_vw, norm_mix_g, w_in, dn_conv_w, dn_A_log,
                 dn_dt_bias, dn_norm_g, ml_i_bias, ml_f_bias, ml_norm_g, w_out, norm_ffn_g, w_up,
                 ffn_conv_w, ffn_conv_b, w_down):
    w = w_in[l]
    c0 = dn_qkv + dn_vw
    c1 = c0 + 2 * Hd
    c2 = c1 + 2 * ml_qk + 2 * ml_vw
    nsmall = 2 * Hd + 2 * Hm
    assert nsmall <= LANES and w.shape[1] == c2 + 2 * Hm
    w_big = jnp.concatenate([w[:, :c0], w[:, c1:c2]], axis=1).astype(BF16)
    w_small = jnp.concatenate([w[:, c0:c1], w[:, c2:]], axis=1)
    w_small = jnp.pad(w_small, ((0, 0), (0, LANES - nsmall))).astype(BF16)
    z = lambda n: jnp.zeros((n,), F32)
    bias = jnp.concatenate([z(Hd), dn_dt_bias[l], ml_i_bias[l], ml_f_bias[l], z(LANES - nsmall)])
    alog = jnp.concatenate([z(Hd), dn_A_log[l], z(LANES - 2 * Hd)])
    par = jnp.zeros((SUBLANES, LANES), F32).at[0].set(bias).at[1].set(alog)
    return dict(norm_mix_g=norm_mix_g[l], w_big=w_big, w_small=w_small, dn_conv_w=dn_conv_w[l],
                par=par, dn_norm_g=dn_norm_g[l], ml_norm_g=ml_norm_g[l], w_out=w_out[l].astype(BF16),
                norm_ffn_g=norm_ffn_g[l], w_up=w_up[l].astype(BF16), ffn_conv_w=ffn_conv_w[l],
                ffn_conv_b=ffn_conv_b[l], w_down=w_down[l].astype(BF16))


def kernel(x_prompt, x_sample, state_dn_conv, state_dn_S, state_ml_C, state_ml_n, state_ml_m,
           state_ffn_conv, norm_mix_g, w_in, dn_conv_w, dn_A_log, dn_dt_bias, dn_norm_g,
           ml_i_bias, ml_f_bias, ml_norm_g, w_out, norm_ffn_g, w_up, ffn_conv_w, ffn_conv_b,
           w_down, norm_final_g):
    states = (state_dn_conv, state_dn_S, state_ml_C, state_ml_n, state_ml_m, state_ffn_conv)
    depth = w_in.shape[0]
    batch = x_prompt.shape[0]
    Hd, dk, dv = state_dn_S.shape[2:]
    Hm, mdk, mdv = state_ml_C.shape[2:]
    xp, xs = x_prompt, x_sample
    p_new = [[] for _ in states]
    s_new = [[] for _ in states]
    for l in range(depth):
        p = _prep_params(l, Hd, Hm, state_dn_conv.shape[-1], Hd * dv, Hm * mdk, Hm * mdv,
                         norm_mix_g, w_in, dn_conv_w, dn_A_log, dn_dt_bias, dn_norm_g, ml_i_bias,
                         ml_f_bias, ml_norm_g, w_out, norm_ffn_g, w_up, ffn_conv_w, ffn_conv_b,
                         w_down)
        last = l == depth - 1
        st_p = tuple(jnp.zeros((batch,) + s.shape[2:], s.dtype) for s in states)
        st_s = tuple(s[l] for s in states)
        xp, np_st = _layer(xp, st_p, p, norm_final_g, last, decode=False)
        xs, ns_st = _layer(xs, st_s, p, norm_final_g, last, decode=True)
        for i in range(len(states)):
            p_new[i].append(np_st[i])
            s_new[i].append(ns_st[i])
    outs_p = tuple(jnp.stack(a, axis=0) for a in p_new)
    outs_s = tuple(jnp.stack(a, axis=0) for a in s_new)
    return (xp, xs) + outs_p + outs_s
```

```python
import functools

import jax
import jax.numpy as jnp
from jax import lax
from jax.experimental import pallas as pl
from jax.experimental.pallas import tpu as pltpu

EPS = 1e-6
CHUNK = 64
MIX_CHUNK = 128
F32 = jnp.float32
BF16 = jnp.bfloat16
LANES = 128
SUBLANES = 8
VMEM_LIMIT = 56 * 1024 * 1024


def _cparams(sem, vmem=VMEM_LIMIT, flags=None):
    return pltpu.CompilerParams(dimension_semantics=sem, vmem_limit_bytes=vmem, flags=flags)


def _dot(a, b):
    return jnp.dot(a.astype(BF16), b.astype(BF16), preferred_element_type=F32)


def _dot_nt(a, b):
    return lax.dot_general(a.astype(BF16), b.astype(BF16), (((1,), (1,)), ((), ())),
                           preferred_element_type=F32)


def _dot_tn(a, b):
    return lax.dot_general(a.astype(BF16), b.astype(BF16), (((0,), (0,)), ((), ())),
                           preferred_element_type=F32)


def _mask_dot_exact(mask_bf16, x):
    hi = x.astype(BF16)
    r1 = x - hi.astype(F32)
    mid = r1.astype(BF16)
    lo = (r1 - mid.astype(F32)).astype(BF16)
    d = functools.partial(jnp.dot, preferred_element_type=F32)
    return (d(mask_bf16, hi) + d(mask_bf16, mid)) + d(mask_bf16, lo)


def _softplus(x):
    return jnp.maximum(x, 0.0) + jnp.log1p(jnp.exp(-jnp.abs(x)))


def _sigmoid(x):
    return 1.0 / (1.0 + jnp.exp(-x))


def _rms(x, g):
    return x * lax.rsqrt(jnp.mean(x * x, axis=-1, keepdims=True) + EPS) * g


def _log2(n):
    assert n & (n - 1) == 0 and n > 0, n
    return n.bit_length() - 1


def _pick(n, cands):
    for c in cands:
        if n % c == 0:
            return c
    return n


def _inproj_kernel(x_ref, g_ref, wb_ref, ws_ref, ob_ref, os_ref, h_ref):
    @pl.when(pl.program_id(1) == 0)
    def _():
        h = _rms(x_ref[...], g_ref[...]).astype(BF16)
        h_ref[...] = h
        os_ref[...] = jnp.dot(h, ws_ref[...], preferred_element_type=F32)

    ob_ref[...] = jnp.dot(h_ref[...], wb_ref[...], preferred_element_type=F32)


def _inproj(x, g, w_big, w_small):
    m, d = x.shape
    nb = w_big.shape[1]
    tm = _pick(m, (1024, 512, 256, 128, 64, 32, 16, 8))
    tn = _pick(nb, (1024, 512, 256, 128))
    return pl.pallas_call(
        _inproj_kernel,
        grid=(m // tm, nb // tn),
        in_specs=[
            pl.BlockSpec((tm, d), lambda i, j: (i, 0)),
            pl.BlockSpec((1, d), lambda i, j: (0, 0)),
            pl.BlockSpec((d, tn), lambda i, j: (0, j)),
            pl.BlockSpec((d, LANES), lambda i, j: (0, 0)),
        ],
        out_specs=[
            pl.BlockSpec((tm, tn), lambda i, j: (i, j)),
            pl.BlockSpec((tm, LANES), lambda i, j: (i, 0)),
        ],
        out_shape=[jax.ShapeDtypeStruct((m, nb), F32), jax.ShapeDtypeStruct((m, LANES), F32)],
        scratch_shapes=[pltpu.VMEM((tm, d), BF16)],
        compiler_params=_cparams(("parallel", "arbitrary")),
        name="inproj",
    )(x, g.reshape(1, d), w_big, w_small)


def _seq_masks(R, T):
    RT = R * T
    sh = _log2(T)
    ri = lax.broadcasted_iota(jnp.int32, (RT, RT), 0)
    ci = lax.broadcasted_iota(jnp.int32, (RT, RT), 1)
    same = (ri >> sh) == (ci >> sh)
    tril = same & (ci <= ri)
    strict = same & (ci < ri)
    eye = ri == ci
    lastsel = same & ((ci & (T - 1)) == T - 1)
    rowseq = lax.broadcasted_iota(jnp.int32, (RT, 1), 0) >> sh
    return ri, ci, tril, strict, eye, lastsel, rowseq


def _row_form(col, eye):
    return jnp.sum(jnp.where(eye, col, 0.0), axis=0, keepdims=True)


def _at_last(row, lastsel):
    return jnp.sum(jnp.where(lastsel, row, 0.0), axis=1, keepdims=True)


def _unit_lower_inverses(lms, ri, ci, T):
    ds = None
    s = 1
    while s < T:
        sh = _log2(2 * s)
        blk = ((ri >> sh) == (ci >> sh)) & ((ri & (2 * s - 1)) >= s) & ((ci & (2 * s - 1)) < s)
        ms = [jnp.where(blk, lm, 0.0) for lm in lms]
        if ds is None:
            eye = jnp.where(ri == ci, 1.0, 0.0)
            ds = [eye - m for m in ms]
        else:
            dms = [_dot(d, m) for d, m in zip(ds, ms)]
            ds = [d - _dot(dm, d) for d, dm in zip(ds, dms)]
        s *= 2
    return ds


def _causal_conv(ext_ref, x_ref, st_ref, cw_ref, first, T, width):
    lo = SUBLANES - (width - 1)

    @pl.when(first)
    def _():
        ext_ref[:, lo:SUBLANES, :] = st_ref[...]

    ext_ref[:, SUBLANES:SUBLANES + T, :] = x_ref[...]
    cw = cw_ref[...]
    acc = None
    for j in range(width):
        term = ext_ref[:, lo + j:lo + j + T, :] * cw[j:j + 1, :]
        acc = term if acc is None else acc + term
    ext_ref[:, lo:SUBLANES, :] = ext_ref[:, lo + T:SUBLANES + T, :]
    return acc


def _gdn_kernel(qkv_ref, z_ref, sm_ref, cst_ref, s0_ref, cw_ref, par_ref, ng_ref,
                o_ref, s_ref, ext_ref, *, R, T, H, DK, DV, width):
    RT = R * T
    QK = H * DK
    first = pl.program_id(1) == 0

    @pl.when(first)
    def _():
        s_ref[...] = s0_ref[...]

    y = _causal_conv(ext_ref, qkv_ref, cst_ref, cw_ref, first, T, width)
    y = y.reshape(RT, y.shape[-1])
    y = y * _sigmoid(y)
    z = z_ref[...].reshape(RT, H * DV)

    sm = sm_ref[...].reshape(RT, LANES)
    beta_all = _sigmoid(sm)
    g_all = -jnp.exp(par_ref[1:2, :]) * _softplus(sm + par_ref[0:1, :])

    ri, ci, tril, strict, eye, lastsel, rowseq = _seq_masks(R, T)
    g_cum = _mask_dot_exact(jnp.where(tril, 1.0, 0.0).astype(BF16), g_all)

    hs = range(H)
    q = [y[:, h * DK:(h + 1) * DK] for h in hs]
    k = [y[:, QK + h * DK:QK + (h + 1) * DK] for h in hs]
    v = [y[:, 2 * QK + h * DV:2 * QK + (h + 1) * DV] for h in hs]
    q = [a * lax.rsqrt(jnp.sum(a * a, axis=-1, keepdims=True) + EPS) * (DK ** -0.5) for a in q]
    k = [a * lax.rsqrt(jnp.sum(a * a, axis=-1, keepdims=True) + EPS) for a in k]
    beta = [beta_all[:, h:h + 1] for h in hs]
    gc = [g_cum[:, H + h:H + h + 1] for h in hs]
    gr = [_row_form(a, eye) for a in gc]
    glast = [_at_last(a, lastsel) for a in gr]
    decay = [jnp.exp(jnp.where(tril, c - r, -jnp.inf)) for c, r in zip(gc, gr)]
    kb = [a * b for a, b in zip(k, beta)]
    lm = [jnp.where(strict, _dot_nt(a, b) * d, 0.0) for a, b, d in zip(kb, k, decay)]
    tinv = _unit_lower_inverses(lm, ri, ci, T)
    eg = [jnp.exp(a) for a in gc]
    uw = [_dot(t, jnp.concatenate([a * b, c * e], axis=1))
          for t, a, b, c, e in zip(tinv, v, beta, kb, eg)]
    u = [a[:, :DV] for a in uw]
    w = [a[:, DV:] for a in uw]
    qk = [_dot_nt(a, b) * d for a, b, d in zip(q, k, decay)]
    qe = [a * e for a, e in zip(q, eg)]
    kd = [a * jnp.exp(l - c) for a, l, c in zip(k, glast, gc)]
    gl = [jnp.exp(a) for a in glast]

    if R == 1:
        s_old = [s_ref[0, h] for h in hs]
        ws = [_dot(jnp.concatenate([a, b], axis=0), c) for a, b, c in zip(w, qe, s_old)]
        v_new = [a - b[:RT] for a, b in zip(u, ws)]
        o = [a[RT:] + _dot(b, c) for a, b, c in zip(ws, qk, v_new)]
        for h in hs:
            s_ref[0, h] = s_old[h] * gl[h][0:1, :] + _dot_tn(kd[h], v_new[h])
    else:
        ws = [[_dot(jnp.concatenate([w[h][s * T:(s + 1) * T], qe[h][s * T:(s + 1) * T]], axis=0),
                    s_ref[s, h]) for s in range(R)] for h in hs]
        v_new = [u[h] - jnp.concatenate([a[:T] for a in ws[h]], axis=0) for h in hs]
        o = [jnp.concatenate([a[T:] for a in ws[h]], axis=0) + _dot(qk[h], v_new[h]) for h in hs]
        for h in hs:
            for s in range(R):
                kd_s = jnp.where(rowseq == s, kd[h], 0.0)
                s_ref[s, h] = s_ref[s, h] * gl[h][s * T:s * T + 1, :] + _dot_tn(kd_s, v_new[h])

    for h in hs:
        zh = z[:, h * DV:(h + 1) * DV]
        out = _rms(o[h], ng_ref[...]) * (zh * _sigmoid(zh))
        o_ref[:, h * DV:(h + 1) * DV] = out.astype(o_ref.dtype)


def _gdn(proj3, small3, conv_state, s0, conv_w, par, norm_g, *, R, T, N):
    B, H, DK, DV = s0.shape
    QKV = conv_state.shape[-1]
    width = conv_w.shape[0]
    assert QKV == 2 * H * DK + H * DV and QKV % LANES == 0 and (H * DV) % LANES == 0
    assert QKV % (H * DV) == 0
    zblk = QKV // (H * DV)
    rows = proj3.shape[0] * T
    kern = functools.partial(_gdn_kernel, R=R, T=T, H=H, DK=DK, DV=DV, width=width)
    return pl.pallas_call(
        kern,
        grid=(B // R, N),
        in_specs=[
            pl.BlockSpec((R, T, QKV), lambda b, n: (b * N + n, 0, 0)),
            pl.BlockSpec((R, T, H * DV), lambda b, n: (b * N + n, 0, zblk)),
            pl.BlockSpec((R, T, LANES), lambda b, n: (b * N + n, 0, 0)),
            pl.BlockSpec((R, width - 1, QKV), lambda b, n: (b, 0, 0)),
            pl.BlockSpec((R, H, DK, DV), lambda b, n: (b, 0, 0, 0)),
            pl.BlockSpec((width, QKV), lambda b, n: (0, 0)),
            pl.BlockSpec((SUBLANES, LANES), lambda b, n: (0, 0)),
            pl.BlockSpec((1, DV), lambda b, n: (0, 0)),
        ],
        out_specs=[
            pl.BlockSpec((R * T, H * DV), lambda b, n: (b * N + n, 0)),
            pl.BlockSpec((R, H, DK, DV), lambda b, n: (b, 0, 0, 0)),
        ],
        out_shape=[jax.ShapeDtypeStruct((rows, H * DV), BF16),
                   jax.ShapeDtypeStruct((B, H, DK, DV), F32)],
        scratch_shapes=[pltpu.VMEM((R, SUBLANES + T, QKV), F32)],
        compiler_params=_cparams(("parallel", "arbitrary")),
        name="gdn",
    )(proj3, proj3, small3, conv_state, s0, conv_w, par, norm_g.reshape(1, DV))


def _mlstm_kernel(q_ref, k_ref, v_ref, og_ref, sm_ref, c0_ref, n0_ref, m0_ref, par_ref, ng_ref,
                  o_ref, c_ref, n_ref, m_ref, *, R, T, H, DK, DV, off_i, off_f):
    RT = R * T

    @pl.when(pl.program_id(1) == 0)
    def _():
        c_ref[...] = c0_ref[...]
        n_ref[...] = n0_ref[...]
        m_ref[...] = m0_ref[...]

    pre = sm_ref[...].reshape(RT, LANES) + par_ref[0:1, :]
    logf_all = -_softplus(-pre)
    ri, ci, tril, strict, eye, lastsel, rowseq = _seq_masks(R, T)
    b_all = _mask_dot_exact(jnp.where(tril, 1.0, 0.0).astype(BF16), logf_all)

    qa = q_ref[...].reshape(RT, H * DK)
    ka = k_ref[...].reshape(RT, H * DK)
    va = v_ref[...].reshape(RT, H * DV)
    oa = og_ref[...].reshape(RT, H * DV)

    hs = range(H)
    q = [qa[:, h * DK:(h + 1) * DK] * (DK ** -0.5) for h in hs]
    k = [ka[:, h * DK:(h + 1) * DK] for h in hs]
    v = [va[:, h * DV:(h + 1) * DV] for h in hs]
    bc = [b_all[:, off_f + h:off_f + h + 1] for h in hs]
    ic = [pre[:, off_i + h:off_i + h + 1] for h in hs]
    dm = [jnp.where(tril, b + _row_form(i - b, eye), -jnp.inf) for b, i in zip(bc, ic)]

    if R == 1:
        m_rows = [m_ref[0, :, h:h + 1] for h in hs]
    else:
        m_rows = []
        for h in hs:
            mr = jnp.zeros((RT, 1), F32)
            for s in range(R):
                mr = jnp.where(rowseq == s, m_ref[s, :, h:h + 1], mr)
            m_rows.append(mr)
    m_new = [jnp.maximum(b + m, jnp.max(d, axis=1, keepdims=True)) for b, m, d in zip(bc, m_rows, dm)]
    inter = [jnp.exp(b + m - mn) for b, m, mn in zip(bc, m_rows, m_new)]
    smat = [_dot_nt(a, b) * jnp.exp(d - mn) for a, b, d, mn in zip(q, k, dm, m_new)]

    if R == 1:
        qc = [_dot(q[h], c_ref[0, h]) for h in hs]
        qn = [jnp.sum(q[h] * n_ref[0, h:h + 1, :], axis=1, keepdims=True) for h in hs]
    else:
        qc, qn = [], []
        for h in hs:
            rows = [slice(s * T, (s + 1) * T) for s in range(R)]
            qc.append(jnp.concatenate([_dot(q[h][r], c_ref[s, h]) for s, r in enumerate(rows)], axis=0))
            qn.append(jnp.concatenate(
                [jnp.sum(q[h][r] * n_ref[s, h:h + 1, :], axis=1, keepdims=True)
                 for s, r in enumerate(rows)], axis=0))
    num = [i * c + _dot(s, a) for i, c, s, a in zip(inter, qc, smat, v)]
    den = [i * n + jnp.sum(s, axis=1, keepdims=True) for i, n, s in zip(inter, qn, smat)]
    hh = [a / jnp.maximum(jnp.abs(b), jnp.exp(-mn)) for a, b, mn in zip(num, den, m_new)]

    b_last = [_at_last(_row_form(b, eye), lastsel) for b in bc]
    m_end = [_at_last(_row_form(mn, eye), lastsel) for mn in m_new]
    kw = [a * jnp.exp(bl - b + i - me) for a, bl, b, i, me in zip(k, b_last, bc, ic, m_end)]
    cs = [jnp.exp(bl + m - me) for bl, m, me in zip(b_last, m_rows, m_end)]
    for h in hs:
        for s in range(R):
            kw_s = kw[h] if R == 1 else jnp.where(rowseq == s, kw[h], 0.0)
            cs_s = cs[h][s * T:s * T + 1, :]
            c_ref[s, h] = cs_s * c_ref[s, h] + _dot_tn(kw_s, v[h])
            n_ref[s, h:h + 1, :] = cs_s * n_ref[s, h:h + 1, :] + jnp.sum(kw_s, axis=0, keepdims=True)
            m_ref[s, :, h:h + 1] = m_end[h][s * T:s * T + 1, :]

    for h in hs:
        og = oa[:, h * DV:(h + 1) * DV]
        out = _rms(hh[h], ng_ref[:, h * DV:(h + 1) * DV]) * _sigmoid(og)
        o_ref[:, h * DV:(h + 1) * DV] = out.astype(o_ref.dtype)


def _mlstm(proj3, small3, c0, n0, m0, par, norm_g, *, R, T, N, col0, off_i, off_f):
    B, H, DK, DV = c0.shape
    QK, VW = H * DK, H * DV
    assert col0 % QK == 0 and (col0 + 2 * QK) % VW == 0 and QK % LANES == 0
    rows = proj3.shape[0] * T
    kern = functools.partial(_mlstm_kernel, R=R, T=T, H=H, DK=DK, DV=DV, off_i=off_i, off_f=off_f)
    qb = col0 // QK
    vb = (col0 + 2 * QK) // VW
    state_specs = [
        pl.BlockSpec((R, H, DK, DV), lambda b, n: (b, 0, 0, 0)),
        pl.BlockSpec((R, H, DK), lambda b, n: (b, 0, 0)),
        pl.BlockSpec((R, 1, H), lambda b, n: (b, 0, 0)),
    ]
    return pl.pallas_call(
        kern,
        grid=(B // R, N),
        in_specs=[
            pl.BlockSpec((R, T, QK), lambda b, n: (b * N + n, 0, qb)),
            pl.BlockSpec((R, T, QK), lambda b, n: (b * N + n, 0, qb + 1)),
            pl.BlockSpec((R, T, VW), lambda b, n: (b * N + n, 0, vb)),
            pl.BlockSpec((R, T, VW), lambda b, n: (b * N + n, 0, vb + 1)),
            pl.BlockSpec((R, T, LANES), lambda b, n: (b * N + n, 0, 0)),
            *state_specs,
            pl.BlockSpec((SUBLANES, LANES), lambda b, n: (0, 0)),
            pl.BlockSpec((1, VW), lambda b, n: (0, 0)),
        ],
        out_specs=[pl.BlockSpec((R * T, VW), lambda b, n: (b * N + n, 0)), *state_specs],
        out_shape=[jax.ShapeDtypeStruct((rows, VW), BF16),
                   jax.ShapeDtypeStruct(c0.shape, F32),
                   jax.ShapeDtypeStruct(n0.shape, F32),
                   jax.ShapeDtypeStruct(m0.shape, F32)],
        compiler_params=_cparams(("parallel", "arbitrary")),
        name="mlstm",
    )(proj3, proj3, proj3, proj3, small3, c0, n0, m0, par, norm_g.reshape(1, VW))


def _outproj_kernel(x_ref, a_ref, b_ref, wa_ref, wb_ref, o_ref):
    o_ref[...] = (x_ref[...] + jnp.dot(a_ref[...], wa_ref[...], preferred_element_type=F32)
                  + jnp.dot(b_ref[...], wb_ref[...], preferred_element_type=F32))


def _outproj(x, a, b, w_out):
    m, d = x.shape
    ka, kb = a.shape[1], b.shape[1]
    assert ka == kb
    tm = _pick(m, (512, 256, 128, 64, 32, 16, 8))
    return pl.pallas_call(
        _outproj_kernel,
        grid=(m // tm,),
        in_specs=[
            pl.BlockSpec((tm, d), lambda i: (i, 0)),
            pl.BlockSpec((tm, ka), lambda i: (i, 0)),
            pl.BlockSpec((tm, kb), lambda i: (i, 0)),
            pl.BlockSpec((ka, d), lambda i: (0, 0)),
            pl.BlockSpec((kb, d), lambda i: (1, 0)),
        ],
        out_specs=pl.BlockSpec((tm, d), lambda i: (i, 0)),
        out_shape=jax.ShapeDtypeStruct((m, d), F32),
        compiler_params=_cparams(("parallel",)),
        name="outproj",
    )(x, a, b, w_out, w_out)


def _ffn_kernel(x_ref, g_ref, wg_ref, wu_ref, cwg_ref, cwu_ref, cbg_ref, cbu_ref, wd_ref,
                stg_ref, stu_ref, gf_ref, y_ref, nsg_ref, nsu_ref, h_ref, act_ref, ext_ref, *carry,
                tm, shift, pad, tps, width, nj, final_norm):
    i = pl.program_id(0)
    j = pl.program_id(1)
    hist = (width - 1) * shift

    def activate_tile(slot):
        conv = []
        for idx, (w_ref, cw_ref, cb_ref, st_ref, ns_ref) in enumerate(
                ((wg_ref, cwg_ref, cbg_ref, stg_ref, nsg_ref), (wu_ref, cwu_ref, cbu_ref, stu_ref, nsu_ref))):
            u = jnp.dot(h_ref[...], w_ref[...], preferred_element_type=F32)
            ext_ref[idx, pad:pad + tm, :] = u
            if tps == 1:
                ext_ref[idx, pad - hist:pad, :] = st_ref[0]
            else:
                ext_ref[idx, pad - hist:pad, :] = jnp.where(i % tps == 0, st_ref[0], carry[0][j, idx])
            cw = cw_ref[...]
            c = u * cw[width - 1:width, :] + cb_ref[...]
            for t in range(width - 1):
                off = pad - hist + t * shift
                c = c + ext_ref[idx, off:off + tm, :] * cw[t:t + 1, :]
            new = u[tm - hist:, :]
            ns_ref[0] = new
            if tps != 1:
                carry[0][j, idx] = new
            conv.append(c)
        act_ref[slot] = (conv[0] * _sigmoid(conv[0]) * conv[1]).astype(BF16)

    def down_tile(slot):
        return jnp.dot(act_ref[slot], wd_ref[...], preferred_element_type=F32)

    @pl.when(j == 0)
    def _():
        h_ref[...] = _rms(x_ref[...], g_ref[...]).astype(BF16)
        y_ref[...] = jnp.zeros_like(y_ref)
        activate_tile(0)

    for parity in (0, 1):
        @pl.when((j > 0) & (j < nj) & (j % 2 == parity))
        def _():
            activate_tile(parity)
            y_ref[...] += down_tile(1 - parity)

    @pl.when(j == nj)
    def _():
        xo = x_ref[...] + y_ref[...] + down_tile((nj - 1) % 2)
        y_ref[...] = _rms(xo, gf_ref[...]) if final_norm else xo


def _ffn(x, norm_g, w_up, conv_w, conv_b, w_down, state, final_g, *, tm, shift, tps, final_norm):
    m, d = x.shape
    ff = w_down.shape[0]
    width = conv_w.shape[0]
    hist = (width - 1) * shift
    pad = -(-hist // SUBLANES) * SUBLANES
    tf = _pick(ff, (512, 256, 128))
    nj = ff // tf
    conv_b = conv_b.reshape(1, 2 * ff)
    kern = functools.partial(_ffn_kernel, tm=tm, shift=shift, pad=pad, tps=tps, width=width, nj=nj,
                             final_norm=final_norm)
    scratch = [pltpu.VMEM((tm, d), BF16), pltpu.VMEM((2, tm, tf), BF16), pltpu.VMEM((2, pad + tm, tf), F32)]
    cur = lambda j: jnp.minimum(j, nj - 1)
    prv = lambda j: jnp.maximum(j - 1, 0)
    if tps != 1:
        scratch.append(pltpu.VMEM((nj, 2, hist, tf), F32))
    st_shape = jax.ShapeDtypeStruct((m // tm, hist, ff), F32)
    y, nsg, nsu = pl.pallas_call(
        kern,
        grid=(m // tm, nj + 1),
        in_specs=[
            pl.BlockSpec((tm, d), lambda i, j: (i, 0)),
            pl.BlockSpec((1, d), lambda i, j: (0, 0)),
            pl.BlockSpec((d, tf), lambda i, j: (0, cur(j))),
            pl.BlockSpec((d, tf), lambda i, j: (0, nj + cur(j))),
            pl.BlockSpec((width, tf), lambda i, j: (0, cur(j))),
            pl.BlockSpec((width, tf), lambda i, j: (0, nj + cur(j))),
            pl.BlockSpec((1, tf), lambda i, j: (0, cur(j))),
            pl.BlockSpec((1, tf), lambda i, j: (0, nj + cur(j))),
            pl.BlockSpec((tf, d), lambda i, j: (prv(j), 0)),
            pl.BlockSpec((1, hist, tf), lambda i, j: (i // tps, 0, cur(j))),
            pl.BlockSpec((1, hist, tf), lambda i, j: (i // tps, 0, nj + cur(j))),
            pl.BlockSpec((1, d), lambda i, j: (0, 0)),
        ],
        out_specs=[
            pl.BlockSpec((tm, d), lambda i, j: (i, 0)),
            pl.BlockSpec((1, hist, tf), lambda i, j: (i, 0, cur(j))),
            pl.BlockSpec((1, hist, tf), lambda i, j: (i, 0, cur(j))),
        ],
        out_shape=[jax.ShapeDtypeStruct((m, d), F32), st_shape, st_shape],
        scratch_shapes=scratch,
        compiler_params=_cparams(("arbitrary", "arbitrary")),
        name="convffn",
    )(x, norm_g.reshape(1, d), w_up, w_up, conv_w, conv_w, conv_b, conv_b, w_down, state, state,
      final_g.reshape(1, d))
    return y, nsg[tps - 1::tps], nsu[tps - 1::tps]


def _layer(x, st, p, final_g, final_norm, decode):
    conv_buf, s0, c0, n0, m0, ffn_buf = st
    B, L, D = x.shape
    Hd, Hm = s0.shape[1], c0.shape[1]
    dn_qkv = conv_buf.shape[-1]
    dn_vw = Hd * s0.shape[3]
    ff = p["w_down"].shape[0]

    T = next((c for c in (MIX_CHUNK, CHUNK) if L % c == 0), L)
    N = L // T
    R = _pick(B, (8, 4, 2, 1)) if decode else 1
    if R * T > 128:
        R = 1

    x2 = x.reshape(B * L, D)
    big, small = _inproj(x2, p["norm_mix_g"], p["w_big"], p["w_small"])
    big3 = big.reshape(B * N, T, big.shape[1])
    small3 = small.reshape(B * N, T, LANES)

    o_dn, s_new = _gdn(big3, small3, conv_buf, s0, p["dn_conv_w"], p["par"], p["dn_norm_g"],
                       R=R, T=T, N=N)
    o_ml, c_new, n_new, m_new = _mlstm(big3, small3, c0, n0, m0.reshape(B, 1, Hm), p["par"],
                                       p["ml_norm_g"], R=R, T=T, N=N, col0=dn_qkv + dn_vw,
                                       off_i=2 * Hd, off_f=2 * Hd + Hm)
    width = p["dn_conv_w"].shape[0]
    big4 = big.reshape(B, L, big.shape[1])
    if L >= width - 1:
        conv_new = big4[:, L - (width - 1):, :dn_qkv]
    else:
        conv_new = jnp.concatenate([conv_buf, big4[:, :, :dn_qkv]], axis=1)[:, L:]

    x1 = _outproj(x2, o_dn, o_ml, p["w_out"])

    fw = p["ffn_conv_w"].shape[0]
    if decode:
        x1t = x1.reshape(B, L, D).transpose(1, 0, 2).reshape(L * B, D)
        state = ffn_buf.transpose(1, 0, 2).reshape(1, (fw - 1) * B, 2 * ff)
        y, nsg, nsu = _ffn(x1t, p["norm_ffn_g"], p["w_up"], p["ffn_conv_w"], p["ffn_conv_b"],
                           p["w_down"], state, final_g, tm=L * B, shift=B, tps=1,
                           final_norm=final_norm)
        y = y.reshape(L, B, D).transpose(1, 0, 2)
        ffn_new = jnp.concatenate([nsg, nsu], axis=-1).reshape(fw - 1, B, 2 * ff).transpose(1, 0, 2)
    else:
        tm = _pick(L, (512, 256, 128, 64, 32, 16, 8))
        y, nsg, nsu = _ffn(x1, p["norm_ffn_g"], p["w_up"], p["ffn_conv_w"], p["ffn_conv_b"],
                           p["w_down"], ffn_buf, final_g, tm=tm, shift=1, tps=L // tm,
                           final_norm=final_norm)
        y = y.reshape(B, L, D)
        ffn_new = jnp.concatenate([nsg, nsu], axis=-1)
    return y, (conv_new, s_new, c_new, n_new, m_new.reshape(B, Hm), ffn_new)


def _regroup_kernel(w_ref, big_ref, small_ref, *, c0, c1, c2):
    n = w_ref.shape[1]
    big_ref[:, :c0] = w_ref[:, :c0].astype(BF16)
    big_ref[:, c0:] = w_ref[:, c1:c2].astype(BF16)
    narrow = jnp.concatenate(
        [w_ref[:, c0:c1], w_ref[:, c2:], jnp.zeros((w_ref.shape[0], LANES - (c1 - c0) - (n - c2)), F32)],
        axis=1)
    small_ref[...] = narrow.astype(BF16)


def _regroup(w, c0, c1, c2):
    k, n = w.shape
    tk = _pick(k, (256, 128, 64, 32, 16))
    nb = c0 + c2 - c1
    return pl.pallas_call(
        functools.partial(_regroup_kernel, c0=c0, c1=c1, c2=c2),
        grid=(k // tk,),
        in_specs=[pl.BlockSpec((tk, n), lambda i: (i, 0))],
        out_specs=[pl.BlockSpec((tk, nb), lambda i: (i, 0)), pl.BlockSpec((tk, LANES), lambda i: (i, 0))],
        out_shape=[jax.ShapeDtypeStruct((k, nb), BF16), jax.ShapeDtypeStruct((k, LANES), BF16)],
        compiler_params=_cparams(("parallel",)),
        name="regroup",
    )(w)


def _prep_params(l, Hd, Hm, dn_qkv, dn_vw, ml_qk, ml_vw, norm_mix_g, w_in, dn_conv_w, dn_A_log,
                 dn_dt_bias, dn_norm_g, ml_i_bias, ml_f_bias, ml_norm_g, w_out, norm_ffn_g, w_up,
                 ffn_conv_w, ffn_conv_b, w_down):
    w = w_in[l]
    c0 = dn_qkv + dn_vw
    c1 = c0 + 2 * Hd
    c2 = c1 + 2 * ml_qk + 2 * ml_vw
    nsmall = 2 * Hd + 2 * Hm
    assert nsmall <= LANES and w.shape[1] == c2 + 2 * Hm
    w_big, w_small = _regroup(w, c0, c1, c2)
    z = lambda n: jnp.zeros((n,), F32)
    bias = jnp.concatenate([z(Hd), dn_dt_bias[l], ml_i_bias[l], ml_f_bias[l], z(LANES - nsmall)])
    alog = jnp.concatenate([z(Hd), dn_A_log[l], z(LANES - 2 * Hd)])
    par = jnp.zeros((SUBLANES, LANES), F32).at[0].set(bias).at[1].set(alog)
    return dict(norm_mix_g=norm_mix_g[l], w_big=w_big, w_small=w_small, dn_conv_w=dn_conv_w[l],
                par=par, dn_norm_g=dn_norm_g[l], ml_norm_g=ml_norm_g[l], w_out=w_out[l].astype(BF16),
                norm_ffn_g=norm_ffn_g[l], w_up=w_up[l].astype(BF16), ffn_conv_w=ffn_conv_w[l],
                ffn_conv_b=ffn_conv_b[l], w_down=w_down[l].astype(BF16))


def kernel(x_prompt, x_sample, state_dn_conv, state_dn_S, state_ml_C, state_ml_n, state_ml_m,
           state_ffn_conv, norm_mix_g, w_in, dn_conv_w, dn_A_log, dn_dt_bias, dn_norm_g,
           ml_i_bias, ml_f_bias, ml_norm_g, w_out, norm_ffn_g, w_up, ffn_conv_w, ffn_conv_b,
           w_down, norm_final_g):
    states = (state_dn_conv, state_dn_S, state_ml_C, state_ml_n, state_ml_m, state_ffn_conv)
    depth = w_in.shape[0]
    batch = x_prompt.shape[0]
    Hd, dk, dv = state_dn_S.shape[2:]
    Hm, mdk, mdv = state_ml_C.shape[2:]
    xp, xs = x_prompt, x_sample
    p_new = [[] for _ in states]
    s_new = [[] for _ in states]
    for l in range(depth):
        p = _prep_params(l, Hd, Hm, state_dn_conv.shape[-1], Hd * dv, Hm * mdk, Hm * mdv,
                         norm_mix_g, w_in, dn_conv_w, dn_A_log, dn_dt_bias, dn_norm_g, ml_i_bias,
                         ml_f_bias, ml_norm_g, w_out, norm_ffn_g, w_up, ffn_conv_w, ffn_conv_b,
                         w_down)
        last = l == depth - 1
        st_p = tuple(jnp.zeros((batch,) + s.shape[2:], s.dtype) for s in states)
        st_s = tuple(s[l] for s in states)
        xp, np_st = _layer(xp, st_p, p, norm_final_g, last, decode=False)
        xs, ns_st = _layer(xs, st_s, p, norm_final_g, last, decode=True)
        for i in range(len(states)):
            p_new[i].append(np_st[i])
            s_new[i].append(ns_st[i])
    outs_p = tuple(jnp.stack(a, axis=0) for a in p_new)
    outs_s = tuple(jnp.stack(a, axis=0) for a in s_new)
    return (xp, xs) + outs_p + outs_s
```

```python
import functools

import jax
import jax.numpy as jnp
from jax import lax
from jax.experimental import pallas as pl
from jax.experimental.pallas import tpu as pltpu

EPS = 1e-6
CHUNK = 64
MIX_CHUNK = 128
F32 = jnp.float32
BF16 = jnp.bfloat16
LANES = 128
SUBLANES = 8
VMEM_LIMIT = 56 * 1024 * 1024


def _cparams(sem, vmem=VMEM_LIMIT, flags=None):
    return pltpu.CompilerParams(dimension_semantics=sem, vmem_limit_bytes=vmem, flags=flags)


def _dot(a, b):
    return jnp.dot(a.astype(BF16), b.astype(BF16), preferred_element_type=F32)


def _dot_nt(a, b):
    return lax.dot_general(a.astype(BF16), b.astype(BF16), (((1,), (1,)), ((), ())),
                           preferred_element_type=F32)


def _dot_tn(a, b):
    return lax.dot_general(a.astype(BF16), b.astype(BF16), (((0,), (0,)), ((), ())),
                           preferred_element_type=F32)


def _mask_dot_exact(mask_bf16, x):
    hi = x.astype(BF16)
    r1 = x - hi.astype(F32)
    mid = r1.astype(BF16)
    lo = (r1 - mid.astype(F32)).astype(BF16)
    d = functools.partial(jnp.dot, preferred_element_type=F32)
    return (d(mask_bf16, hi) + d(mask_bf16, mid)) + d(mask_bf16, lo)


def _softplus(x):
    return jnp.maximum(x, 0.0) + jnp.log1p(jnp.exp(-jnp.abs(x)))


def _sigmoid(x):
    return 1.0 / (1.0 + jnp.exp(-x))


def _rms(x, g):
    return x * lax.rsqrt(jnp.mean(x * x, axis=-1, keepdims=True) + EPS) * g


def _log2(n):
    assert n & (n - 1) == 0 and n > 0, n
    return n.bit_length() - 1


def _pick(n, cands):
    for c in cands:
        if n % c == 0:
            return c
    return n


def _inproj_kernel(x_ref, g_ref, wb_ref, ws_ref, ob_ref, os_ref, h_ref):
    nt = functools.partial(lax.dot_general, dimension_numbers=(((1,), (1,)), ((), ())),
                           preferred_element_type=F32)

    @pl.when(pl.program_id(1) == 0)
    def _():
        h = _rms(x_ref[...], g_ref[...]).astype(BF16)
        h_ref[...] = h
        os_ref[...] = nt(h, ws_ref[...])

    ob_ref[...] = nt(h_ref[...], wb_ref[...])


def _inproj(x, g, w_big_t, w_small_t):
    m, d = x.shape
    nb = w_big_t.shape[0]
    tm = _pick(m, (1024, 512, 256, 128, 64, 32, 16, 8))
    tn = _pick(nb, (1024, 512, 256, 128))
    return pl.pallas_call(
        _inproj_kernel,
        grid=(m // tm, nb // tn),
        in_specs=[
            pl.BlockSpec((tm, d), lambda i, j: (i, 0)),
            pl.BlockSpec((1, d), lambda i, j: (0, 0)),
            pl.BlockSpec((tn, d), lambda i, j: (j, 0)),
            pl.BlockSpec((LANES, d), lambda i, j: (0, 0)),
        ],
        out_specs=[
            pl.BlockSpec((tm, tn), lambda i, j: (i, j)),
            pl.BlockSpec((tm, LANES), lambda i, j: (i, 0)),
        ],
        out_shape=[jax.ShapeDtypeStruct((m, nb), F32), jax.ShapeDtypeStruct((m, LANES), F32)],
        scratch_shapes=[pltpu.VMEM((tm, d), BF16)],
        compiler_params=_cparams(("parallel", "arbitrary")),
        name="inproj",
    )(x, g.reshape(1, d), w_big_t, w_small_t)


def _seq_masks(R, T):
    RT = R * T
    sh = _log2(T)
    ri = lax.broadcasted_iota(jnp.int32, (RT, RT), 0)
    ci = lax.broadcasted_iota(jnp.int32, (RT, RT), 1)
    same = (ri >> sh) == (ci >> sh)
    tril = same & (ci <= ri)
    strict = same & (ci < ri)
    eye = ri == ci
    lastsel = same & ((ci & (T - 1)) == T - 1)
    rowseq = lax.broadcasted_iota(jnp.int32, (RT, 1), 0) >> sh
    return ri, ci, tril, strict, eye, lastsel, rowseq


def _row_form(col, eye):
    return jnp.sum(jnp.where(eye, col, 0.0), axis=0, keepdims=True)


def _at_last(row, lastsel):
    return jnp.sum(jnp.where(lastsel, row, 0.0), axis=1, keepdims=True)


def _unit_lower_inverses(lms, ri, ci, T):
    ds = None
    s = 1
    while s < T:
        sh = _log2(2 * s)
        blk = ((ri >> sh) == (ci >> sh)) & ((ri & (2 * s - 1)) >= s) & ((ci & (2 * s - 1)) < s)
        ms = [jnp.where(blk, lm, 0.0) for lm in lms]
        if ds is None:
            eye = jnp.where(ri == ci, 1.0, 0.0)
            ds = [eye - m for m in ms]
        else:
            dms = [_dot(d, m) for d, m in zip(ds, ms)]
            ds = [d - _dot(dm, d) for d, dm in zip(ds, dms)]
        s *= 2
    return ds


def _causal_conv(ext_ref, x_ref, st_ref, cw_ref, first, T, width):
    lo = SUBLANES - (width - 1)

    @pl.when(first)
    def _():
        ext_ref[:, lo:SUBLANES, :] = st_ref[...]

    ext_ref[:, SUBLANES:SUBLANES + T, :] = x_ref[...]
    cw = cw_ref[...]
    acc = None
    for j in range(width):
        term = ext_ref[:, lo + j:lo + j + T, :] * cw[j:j + 1, :]
        acc = term if acc is None else acc + term
    ext_ref[:, lo:SUBLANES, :] = ext_ref[:, lo + T:SUBLANES + T, :]
    return acc


def _gdn_kernel(qkv_ref, z_ref, sm_ref, cst_ref, s0_ref, cw_ref, par_ref, ng_ref,
                o_ref, s_ref, ext_ref, *, R, T, H, DK, DV, width):
    RT = R * T
    QK = H * DK
    first = pl.program_id(1) == 0

    @pl.when(first)
    def _():
        s_ref[...] = s0_ref[...]

    y = _causal_conv(ext_ref, qkv_ref, cst_ref, cw_ref, first, T, width)
    y = y.reshape(RT, y.shape[-1])
    y = y * _sigmoid(y)
    z = z_ref[...].reshape(RT, H * DV)

    sm = sm_ref[...].reshape(RT, LANES)
    beta_all = _sigmoid(sm)
    g_all = -jnp.exp(par_ref[1:2, :]) * _softplus(sm + par_ref[0:1, :])

    ri, ci, tril, strict, eye, lastsel, rowseq = _seq_masks(R, T)
    g_cum = _mask_dot_exact(jnp.where(tril, 1.0, 0.0).astype(BF16), g_all)

    hs = range(H)
    q = [y[:, h * DK:(h + 1) * DK] for h in hs]
    k = [y[:, QK + h * DK:QK + (h + 1) * DK] for h in hs]
    v = [y[:, 2 * QK + h * DV:2 * QK + (h + 1) * DV] for h in hs]
    q = [a * lax.rsqrt(jnp.sum(a * a, axis=-1, keepdims=True) + EPS) * (DK ** -0.5) for a in q]
    k = [a * lax.rsqrt(jnp.sum(a * a, axis=-1, keepdims=True) + EPS) for a in k]
    beta = [beta_all[:, h:h + 1] for h in hs]
    gc = [g_cum[:, H + h:H + h + 1] for h in hs]
    gr = [_row_form(a, eye) for a in gc]
    glast = [_at_last(a, lastsel) for a in gr]
    decay = [jnp.exp(jnp.where(tril, c - r, -jnp.inf)) for c, r in zip(gc, gr)]
    kb = [a * b for a, b in zip(k, beta)]
    lm = [jnp.where(strict, _dot_nt(a, b) * d, 0.0) for a, b, d in zip(kb, k, decay)]
    tinv = _unit_lower_inverses(lm, ri, ci, T)
    eg = [jnp.exp(a) for a in gc]
    uw = [_dot(t, jnp.concatenate([a * b, c * e], axis=1))
          for t, a, b, c, e in zip(tinv, v, beta, kb, eg)]
    u = [a[:, :DV] for a in uw]
    w = [a[:, DV:] for a in uw]
    qk = [_dot_nt(a, b) * d for a, b, d in zip(q, k, decay)]
    qe = [a * e for a, e in zip(q, eg)]
    kd = [a * jnp.exp(l - c) for a, l, c in zip(k, glast, gc)]
    gl = [jnp.exp(a) for a in glast]

    if R == 1:
        s_old = [s_ref[0, h] for h in hs]
        ws = [_dot(jnp.concatenate([a, b], axis=0), c) for a, b, c in zip(w, qe, s_old)]
        v_new = [a - b[:RT] for a, b in zip(u, ws)]
        o = [a[RT:] + _dot(b, c) for a, b, c in zip(ws, qk, v_new)]
        for h in hs:
            s_ref[0, h] = s_old[h] * gl[h][0:1, :] + _dot_tn(kd[h], v_new[h])
    else:
        ws = [[_dot(jnp.concatenate([w[h][s * T:(s + 1) * T], qe[h][s * T:(s + 1) * T]], axis=0),
                    s_ref[s, h]) for s in range(R)] for h in hs]
        v_new = [u[h] - jnp.concatenate([a[:T] for a in ws[h]], axis=0) for h in hs]
        o = [jnp.concatenate([a[T:] for a in ws[h]], axis=0) + _dot(qk[h], v_new[h]) for h in hs]
        for h in hs:
            for s in range(R):
                kd_s = jnp.where(rowseq == s, kd[h], 0.0)
                s_ref[s, h] = s_ref[s, h] * gl[h][s * T:s * T + 1, :] + _dot_tn(kd_s, v_new[h])

    for h in hs:
        zh = z[:, h * DV:(h + 1) * DV]
        out = _rms(o[h], ng_ref[...]) * (zh * _sigmoid(zh))
        o_ref[:, h * DV:(h + 1) * DV] = out.astype(o_ref.dtype)


def _gdn(proj3, small3, conv_state, s0, conv_w, par, norm_g, *, R, T, N):
    B, H, DK, DV = s0.shape
    QKV = conv_state.shape[-1]
    width = conv_w.shape[0]
    assert QKV == 2 * H * DK + H * DV and QKV % LANES == 0 and (H * DV) % LANES == 0
    assert QKV % (H * DV) == 0
    zblk = QKV // (H * DV)
    rows = proj3.shape[0] * T
    kern = functools.partial(_gdn_kernel, R=R, T=T, H=H, DK=DK, DV=DV, width=width)
    return pl.pallas_call(
        kern,
        grid=(B // R, N),
        in_specs=[
            pl.BlockSpec((R, T, QKV), lambda b, n: (b * N + n, 0, 0)),
            pl.BlockSpec((R, T, H * DV), lambda b, n: (b * N + n, 0, zblk)),
            pl.BlockSpec((R, T, LANES), lambda b, n: (b * N + n, 0, 0)),
            pl.BlockSpec((R, width - 1, QKV), lambda b, n: (b, 0, 0)),
            pl.BlockSpec((R, H, DK, DV), lambda b, n: (b, 0, 0, 0)),
            pl.BlockSpec((width, QKV), lambda b, n: (0, 0)),
            pl.BlockSpec((SUBLANES, LANES), lambda b, n: (0, 0)),
            pl.BlockSpec((1, DV), lambda b, n: (0, 0)),
        ],
        out_specs=[
            pl.BlockSpec((R * T, H * DV), lambda b, n: (b * N + n, 0)),
            pl.BlockSpec((R, H, DK, DV), lambda b, n: (b, 0, 0, 0)),
        ],
        out_shape=[jax.ShapeDtypeStruct((rows, H * DV), BF16),
                   jax.ShapeDtypeStruct((B, H, DK, DV), F32)],
        scratch_shapes=[pltpu.VMEM((R, SUBLANES + T, QKV), F32)],
        compiler_params=_cparams(("parallel", "arbitrary")),
        name="gdn",
    )(proj3, proj3, small3, conv_state, s0, conv_w, par, norm_g.reshape(1, DV))


def _mlstm_kernel(q_ref, k_ref, v_ref, og_ref, sm_ref, c0_ref, n0_ref, m0_ref, par_ref, ng_ref,
                  o_ref, c_ref, n_ref, m_ref, *, R, T, H, DK, DV, off_i, off_f):
    RT = R * T

    @pl.when(pl.program_id(1) == 0)
    def _():
        c_ref[...] = c0_ref[...]
        n_ref[...] = n0_ref[...]
        m_ref[...] = m0_ref[...]

    pre = sm_ref[...].reshape(RT, LANES) + par_ref[0:1, :]
    logf_all = -_softplus(-pre)
    ri, ci, tril, strict, eye, lastsel, rowseq = _seq_masks(R, T)
    b_all = _mask_dot_exact(jnp.where(tril, 1.0, 0.0).astype(BF16), logf_all)

    qa = q_ref[...].reshape(RT, H * DK)
    ka = k_ref[...].reshape(RT, H * DK)
    va = v_ref[...].reshape(RT, H * DV)
    oa = og_ref[...].reshape(RT, H * DV)

    hs = range(H)
    q = [qa[:, h * DK:(h + 1) * DK] * (DK ** -0.5) for h in hs]
    k = [ka[:, h * DK:(h + 1) * DK] for h in hs]
    v = [va[:, h * DV:(h + 1) * DV] for h in hs]
    bc = [b_all[:, off_f + h:off_f + h + 1] for h in hs]
    ic = [pre[:, off_i + h:off_i + h + 1] for h in hs]
    dm = [jnp.where(tril, b + _row_form(i - b, eye), -jnp.inf) for b, i in zip(bc, ic)]

    if R == 1:
        m_rows = [m_ref[0, :, h:h + 1] for h in hs]
    else:
        m_rows = []
        for h in hs:
            mr = jnp.zeros((RT, 1), F32)
            for s in range(R):
                mr = jnp.where(rowseq == s, m_ref[s, :, h:h + 1], mr)
            m_rows.append(mr)
    m_new = [jnp.maximum(b + m, jnp.max(d, axis=1, keepdims=True)) for b, m, d in zip(bc, m_rows, dm)]
    inter = [jnp.exp(b + m - mn) for b, m, mn in zip(bc, m_rows, m_new)]
    smat = [_dot_nt(a, b) * jnp.exp(d - mn) for a, b, d, mn in zip(q, k, dm, m_new)]

    if R == 1:
        qc = [_dot(q[h], c_ref[0, h]) for h in hs]
        qn = [jnp.sum(q[h] * n_ref[0, h:h + 1, :], axis=1, keepdims=True) for h in hs]
    else:
        qc, qn = [], []
        for h in hs:
            rows = [slice(s * T, (s + 1) * T) for s in range(R)]
            qc.append(jnp.concatenate([_dot(q[h][r], c_ref[s, h]) for s, r in enumerate(rows)], axis=0))
            qn.append(jnp.concatenate(
                [jnp.sum(q[h][r] * n_ref[s, h:h + 1, :], axis=1, keepdims=True)
                 for s, r in enumerate(rows)], axis=0))
    num = [i * c + _dot(s, a) for i, c, s, a in zip(inter, qc, smat, v)]
    den = [i * n + jnp.sum(s, axis=1, keepdims=True) for i, n, s in zip(inter, qn, smat)]
    hh = [a / jnp.maximum(jnp.abs(b), jnp.exp(-mn)) for a, b, mn in zip(num, den, m_new)]

    b_last = [_at_last(_row_form(b, eye), lastsel) for b in bc]
    m_end = [_at_last(_row_form(mn, eye), lastsel) for mn in m_new]
    kw = [a * jnp.exp(bl - b + i - me) for a, bl, b, i, me in zip(k, b_last, bc, ic, m_end)]
    cs = [jnp.exp(bl + m - me) for bl, m, me in zip(b_last, m_rows, m_end)]
    for h in hs:
        for s in range(R):
            kw_s = kw[h] if R == 1 else jnp.where(rowseq == s, kw[h], 0.0)
            cs_s = cs[h][s * T:s * T + 1, :]
            c_ref[s, h] = cs_s * c_ref[s, h] + _dot_tn(kw_s, v[h])
            n_ref[s, h:h + 1, :] = cs_s * n_ref[s, h:h + 1, :] + jnp.sum(kw_s, axis=0, keepdims=True)
            m_ref[s, :, h:h + 1] = m_end[h][s * T:s * T + 1, :]

    for h in hs:
        og = oa[:, h * DV:(h + 1) * DV]
        out = _rms(hh[h], ng_ref[:, h * DV:(h + 1) * DV]) * _sigmoid(og)
        o_ref[:, h * DV:(h + 1) * DV] = out.astype(o_ref.dtype)


def _mlstm(proj3, small3, c0, n0, m0, par, norm_g, *, R, T, N, col0, off_i, off_f):
    B, H, DK, DV = c0.shape
    QK, VW = H * DK, H * DV
    assert col0 % QK == 0 and (col0 + 2 * QK) % VW == 0 and QK % LANES == 0
    rows = proj3.shape[0] * T
    kern = functools.partial(_mlstm_kernel, R=R, T=T, H=H, DK=DK, DV=DV, off_i=off_i, off_f=off_f)
    qb = col0 // QK
    vb = (col0 + 2 * QK) // VW
    state_specs = [
        pl.BlockSpec((R, H, DK, DV), lambda b, n: (b, 0, 0, 0)),
        pl.BlockSpec((R, H, DK), lambda b, n: (b, 0, 0)),
        pl.BlockSpec((R, 1, H), lambda b, n: (b, 0, 0)),
    ]
    return pl.pallas_call(
        kern,
        grid=(B // R, N),
        in_specs=[
            pl.BlockSpec((R, T, QK), lambda b, n: (b * N + n, 0, qb)),
            pl.BlockSpec((R, T, QK), lambda b, n: (b * N + n, 0, qb + 1)),
            pl.BlockSpec((R, T, VW), lambda b, n: (b * N + n, 0, vb)),
            pl.BlockSpec((R, T, VW), lambda b, n: (b * N + n, 0, vb + 1)),
            pl.BlockSpec((R, T, LANES), lambda b, n: (b * N + n, 0, 0)),
            *state_specs,
            pl.BlockSpec((SUBLANES, LANES), lambda b, n: (0, 0)),
            pl.BlockSpec((1, VW), lambda b, n: (0, 0)),
        ],
        out_specs=[pl.BlockSpec((R * T, VW), lambda b, n: (b * N + n, 0)), *state_specs],
        out_shape=[jax.ShapeDtypeStruct((rows, VW), BF16),
                   jax.ShapeDtypeStruct(c0.shape, F32),
                   jax.ShapeDtypeStruct(n0.shape, F32),
                   jax.ShapeDtypeStruct(m0.shape, F32)],
        compiler_params=_cparams(("parallel", "arbitrary")),
        name="mlstm",
    )(proj3, proj3, proj3, proj3, small3, c0, n0, m0, par, norm_g.reshape(1, VW))


def _outproj_kernel(x_ref, a_ref, b_ref, wa_ref, wb_ref, o_ref):
    o_ref[...] = (x_ref[...] + jnp.dot(a_ref[...], wa_ref[...], preferred_element_type=F32)
                  + jnp.dot(b_ref[...], wb_ref[...], preferred_element_type=F32))


def _outproj(x, a, b, w_out):
    m, d = x.shape
    ka, kb = a.shape[1], b.shape[1]
    assert ka == kb
    tm = _pick(m, (512, 256, 128, 64, 32, 16, 8))
    return pl.pallas_call(
        _outproj_kernel,
        grid=(m // tm,),
        in_specs=[
            pl.BlockSpec((tm, d), lambda i: (i, 0)),
            pl.BlockSpec((tm, ka), lambda i: (i, 0)),
            pl.BlockSpec((tm, kb), lambda i: (i, 0)),
            pl.BlockSpec((ka, d), lambda i: (0, 0)),
            pl.BlockSpec((kb, d), lambda i: (1, 0)),
        ],
        out_specs=pl.BlockSpec((tm, d), lambda i: (i, 0)),
        out_shape=jax.ShapeDtypeStruct((m, d), F32),
        compiler_params=_cparams(("parallel",)),
        name="outproj",
    )(x, a, b, w_out, w_out)


def _ffn_kernel(x_ref, g_ref, wg_ref, wu_ref, cwg_ref, cwu_ref, cbg_ref, cbu_ref, wd_ref,
                stg_ref, stu_ref, gf_ref, y_ref, nsg_ref, nsu_ref, h_ref, act_ref, ext_ref, *carry,
                tm, shift, pad, tps, width, nj, final_norm):
    i = pl.program_id(0)
    j = pl.program_id(1)
    hist = (width - 1) * shift

    def activate_tile(slot):
        conv = []
        for idx, (w_ref, cw_ref, cb_ref, st_ref, ns_ref) in enumerate(
                ((wg_ref, cwg_ref, cbg_ref, stg_ref, nsg_ref), (wu_ref, cwu_ref, cbu_ref, stu_ref, nsu_ref))):
            u = jnp.dot(h_ref[...], w_ref[...], preferred_element_type=F32)
            ext_ref[idx, pad:pad + tm, :] = u
            if tps == 1:
                ext_ref[idx, pad - hist:pad, :] = st_ref[0]
            else:
                ext_ref[idx, pad - hist:pad, :] = jnp.where(i % tps == 0, st_ref[0], carry[0][j, idx])
            cw = cw_ref[...]
            c = u * cw[width - 1:width, :] + cb_ref[...]
            for t in range(width - 1):
                off = pad - hist + t * shift
                c = c + ext_ref[idx, off:off + tm, :] * cw[t:t + 1, :]
            new = u[tm - hist:, :]
            ns_ref[0] = new
            if tps != 1:
                carry[0][j, idx] = new
            conv.append(c)
        act_ref[slot] = (conv[0] * _sigmoid(conv[0]) * conv[1]).astype(BF16)

    def down_tile(slot):
        return jnp.dot(act_ref[slot], wd_ref[...], preferred_element_type=F32)

    @pl.when(j == 0)
    def _():
        h_ref[...] = _rms(x_ref[...], g_ref[...]).astype(BF16)
        y_ref[...] = jnp.zeros_like(y_ref)
        activate_tile(0)

    @pl.when((j > 0) & (j < nj))
    def _():
        y_ref[...] += down_tile(0)
        activate_tile(0)

    @pl.when(j == nj)
    def _():
        xo = x_ref[...] + y_ref[...] + down_tile(0)
        y_ref[...] = _rms(xo, gf_ref[...]) if final_norm else xo


def _ffn(x, norm_g, w_up, conv_w, conv_b, w_down, state, final_g, *, tm, shift, tps, final_norm):
    m, d = x.shape
    ff = w_down.shape[0]
    width = conv_w.shape[0]
    hist = (width - 1) * shift
    pad = -(-hist // SUBLANES) * SUBLANES
    tf = _pick(ff, (512, 256, 128))
    nj = ff // tf
    conv_b = conv_b.reshape(1, 2 * ff)
    kern = functools.partial(_ffn_kernel, tm=tm, shift=shift, pad=pad, tps=tps, width=width, nj=nj,
                             final_norm=final_norm)
    scratch = [pltpu.VMEM((tm, d), BF16), pltpu.VMEM((1, tm, tf), BF16), pltpu.VMEM((2, pad + tm, tf), F32)]
    cur = lambda j: jnp.minimum(j, nj - 1)
    prv = lambda j: jnp.maximum(j - 1, 0)
    if tps != 1:
        scratch.append(pltpu.VMEM((nj, 2, hist, tf), F32))
    st_shape = jax.ShapeDtypeStruct((m // tm, hist, ff), F32)
    y, nsg, nsu = pl.pallas_call(
        kern,
        grid=(m // tm, nj + 1),
        in_specs=[
            pl.BlockSpec((tm, d), lambda i, j: (i, 0)),
            pl.BlockSpec((1, d), lambda i, j: (0, 0)),
            pl.BlockSpec((d, tf), lambda i, j: (0, cur(j))),
            pl.BlockSpec((d, tf), lambda i, j: (0, nj + cur(j))),
            pl.BlockSpec((width, tf), lambda i, j: (0, cur(j))),
            pl.BlockSpec((width, tf), lambda i, j: (0, nj + cur(j))),
            pl.BlockSpec((1, tf), lambda i, j: (0, cur(j))),
            pl.BlockSpec((1, tf), lambda i, j: (0, nj + cur(j))),
            pl.BlockSpec((tf, d), lambda i, j: (prv(j), 0)),
            pl.BlockSpec((1, hist, tf), lambda i, j: (i // tps, 0, cur(j))),
            pl.BlockSpec((1, hist, tf), lambda i, j: (i // tps, 0, nj + cur(j))),
            pl.BlockSpec((1, d), lambda i, j: (0, 0)),
        ],
        out_specs=[
            pl.BlockSpec((tm, d), lambda i, j: (i, 0)),
            pl.BlockSpec((1, hist, tf), lambda i, j: (i, 0, cur(j))),
            pl.BlockSpec((1, hist, tf), lambda i, j: (i, 0, cur(j))),
        ],
        out_shape=[jax.ShapeDtypeStruct((m, d), F32), st_shape, st_shape],
        scratch_shapes=scratch,
        compiler_params=_cparams(("arbitrary", "arbitrary")),
        name="convffn",
    )(x, norm_g.reshape(1, d), w_up, w_up, conv_w, conv_w, conv_b, conv_b, w_down, state, state,
      final_g.reshape(1, d))
    return y, nsg[tps - 1::tps], nsu[tps - 1::tps]


def _layer(x, st, p, final_g, final_norm, decode):
    conv_buf, s0, c0, n0, m0, ffn_buf = st
    B, L, D = x.shape
    Hd, Hm = s0.shape[1], c0.shape[1]
    dn_qkv = conv_buf.shape[-1]
    dn_vw = Hd * s0.shape[3]
    ff = p["w_down"].shape[0]

    T = next((c for c in (MIX_CHUNK, CHUNK) if L % c == 0), L)
    N = L // T
    R = _pick(B, (8, 4, 2, 1)) if decode else 1
    if R * T > 128:
        R = 1

    x2 = x.reshape(B * L, D)
    big, small = _inproj(x2, p["norm_mix_g"], p["w_big"], p["w_small"])
    big3 = big.reshape(B * N, T, big.shape[1])
    small3 = small.reshape(B * N, T, LANES)

    o_dn, s_new = _gdn(big3, small3, conv_buf, s0, p["dn_conv_w"], p["par"], p["dn_norm_g"],
                       R=R, T=T, N=N)
    o_ml, c_new, n_new, m_new = _mlstm(big3, small3, c0, n0, m0.reshape(B, 1, Hm), p["par"],
                                       p["ml_norm_g"], R=R, T=T, N=N, col0=dn_qkv + dn_vw,
                                       off_i=2 * Hd, off_f=2 * Hd + Hm)
    width = p["dn_conv_w"].shape[0]
    big4 = big.reshape(B, L, big.shape[1])
    if L >= width - 1:
        conv_new = big4[:, L - (width - 1):, :dn_qkv]
    else:
        conv_new = jnp.concatenate([conv_buf, big4[:, :, :dn_qkv]], axis=1)[:, L:]

    x1 = _outproj(x2, o_dn, o_ml, p["w_out"])

    fw = p["ffn_conv_w"].shape[0]
    if decode:
        x1t = x1.reshape(B, L, D).transpose(1, 0, 2).reshape(L * B, D)
        state = ffn_buf.transpose(1, 0, 2).reshape(1, (fw - 1) * B, 2 * ff)
        y, nsg, nsu = _ffn(x1t, p["norm_ffn_g"], p["w_up"], p["ffn_conv_w"], p["ffn_conv_b"],
                           p["w_down"], state, final_g, tm=L * B, shift=B, tps=1,
                           final_norm=final_norm)
        y = y.reshape(L, B, D).transpose(1, 0, 2)
        ffn_new = jnp.concatenate([nsg, nsu], axis=-1).reshape(fw - 1, B, 2 * ff).transpose(1, 0, 2)
    else:
        tm = _pick(L, (512, 256, 128, 64, 32, 16, 8))
        y, nsg, nsu = _ffn(x1, p["norm_ffn_g"], p["w_up"], p["ffn_conv_w"], p["ffn_conv_b"],
                           p["w_down"], ffn_buf, final_g, tm=tm, shift=1, tps=L // tm,
                           final_norm=final_norm)
        y = y.reshape(B, L, D)
        ffn_new = jnp.concatenate([nsg, nsu], axis=-1)
    return y, (conv_new, s_new, c_new, n_new, m_new.reshape(B, Hm), ffn_new)


def _regroup_kernel(w_ref, big_ref, small_ref, *, c0, c1, c2):
    n = w_ref.shape[0]
    big_ref[:c0, :] = w_ref[:c0, :].astype(BF16)
    big_ref[c0:, :] = w_ref[c1:c2, :].astype(BF16)
    narrow = jnp.concatenate(
        [w_ref[c0:c1, :], w_ref[c2:, :], jnp.zeros((LANES - (c1 - c0) - (n - c2), w_ref.shape[1]), F32)],
        axis=0)
    small_ref[...] = narrow.astype(BF16)


def _regroup(w_t, c0, c1, c2):
    n, k = w_t.shape
    tk = _pick(k, (256, 128))
    nb = c0 + c2 - c1
    assert c0 % 16 == 0 and c1 % SUBLANES == 0 and c2 % SUBLANES == 0
    return pl.pallas_call(
        functools.partial(_regroup_kernel, c0=c0, c1=c1, c2=c2),
        grid=(k // tk,),
        in_specs=[pl.BlockSpec((n, tk), lambda i: (0, i))],
        out_specs=[pl.BlockSpec((nb, tk), lambda i: (0, i)), pl.BlockSpec((LANES, tk), lambda i: (0, i))],
        out_shape=[jax.ShapeDtypeStruct((nb, k), BF16), jax.ShapeDtypeStruct((LANES, k), BF16)],
        compiler_params=_cparams(("parallel",)),
        name="regroup",
    )(w_t)


def _prep_params(l, Hd, Hm, dn_qkv, dn_vw, ml_qk, ml_vw, norm_mix_g, w_in, dn_conv_w, dn_A_log,
                 dn_dt_bias, dn_norm_g, ml_i_bias, ml_f_bias, ml_norm_g, w_out, norm_ffn_g, w_up,
                 ffn_conv_w, ffn_conv_b, w_down):
    w = w_in[l]
    c0 = dn_qkv + dn_vw
    c1 = c0 + 2 * Hd
    c2 = c1 + 2 * ml_qk + 2 * ml_vw
    nsmall = 2 * Hd + 2 * Hm
    assert nsmall <= LANES and w.shape[1] == c2 + 2 * Hm
    w_big, w_small = _regroup(w.T, c0, c1, c2)
    z = lambda n: jnp.zeros((n,), F32)
    bias = jnp.concatenate([z(Hd), dn_dt_bias[l], ml_i_bias[l], ml_f_bias[l], z(LANES - nsmall)])
    alog = jnp.concatenate([z(Hd), dn_A_log[l], z(LANES - 2 * Hd)])
    par = jnp.zeros((SUBLANES, LANES), F32).at[0].set(bias).at[1].set(alog)
    return dict(norm_mix_g=norm_mix_g[l], w_big=w_big, w_small=w_small, dn_conv_w=dn_conv_w[l],
                par=par, dn_norm_g=dn_norm_g[l], ml_norm_g=ml_norm_g[l], w_out=w_out[l].astype(BF16),
                norm_ffn_g=norm_ffn_g[l], w_up=w_up[l].astype(BF16), ffn_conv_w=ffn_conv_w[l],
                ffn_conv_b=ffn_conv_b[l], w_down=w_down[l].astype(BF16))


def kernel(x_prompt, x_sample, state_dn_conv, state_dn_S, state_ml_C, state_ml_n, state_ml_m,
           state_ffn_conv, norm_mix_g, w_in, dn_conv_w, dn_A_log, dn_dt_bias, dn_norm_g,
           ml_i_bias, ml_f_bias, ml_norm_g, w_out, norm_ffn_g, w_up, ffn_conv_w, ffn_conv_b,
           w_down, norm_final_g):
    states = (state_dn_conv, state_dn_S, state_ml_C, state_ml_n, state_ml_m, state_ffn_conv)
    depth = w_in.shape[0]
    batch = x_prompt.shape[0]
    Hd, dk, dv = state_dn_S.shape[2:]
    Hm, mdk, mdv = state_ml_C.shape[2:]
    xp, xs = x_prompt, x_sample
    p_new = [[] for _ in states]
    s_new = [[] for _ in states]
    for l in range(depth):
        p = _prep_params(l, Hd, Hm, state_dn_conv.shape[-1], Hd * dv, Hm * mdk, Hm * mdv,
                         norm_mix_g, w_in, dn_conv_w, dn_A_log, dn_dt_bias, dn_norm_g, ml_i_bias,
                         ml_f_bias, ml_norm_g, w_out, norm_ffn_g, w_up, ffn_conv_w, ffn_conv_b,
                         w_down)
        last = l == depth - 1
        st_p = tuple(jnp.zeros((batch,) + s.shape[2:], s.dtype) for s in states)
        st_s = tuple(s[l] for s in states)
        xp, np_st = _layer(xp, st_p, p, norm_final_g, last, decode=False)
        xs, ns_st = _layer(xs, st_s, p, norm_final_g, last, decode=True)
        for i in range(len(states)):
            p_new[i].append(np_st[i])
            s_new[i].append(ns_st[i])
    outs_p = tuple(jnp.stack(a, axis=0) for a in p_new)
    outs_s = tuple(jnp.stack(a, axis=0) for a in s_new)
    return (xp, xs) + outs_p + outs_s
```

```python
import functools

import jax
import jax.numpy as jnp
from jax import lax
from jax.experimental import pallas as pl
from jax.experimental.pallas import tpu as pltpu

EPS = 1e-6
CHUNK = 64
MIX_CHUNK = 128
PROJ_KINDS = ("q", "k", "v", "silu", "plain", "plain", "sigmoid")
F32 = jnp.float32
BF16 = jnp.bfloat16
LANES = 128
SUBLANES = 8
VMEM_LIMIT = 56 * 1024 * 1024


def _cparams(sem, vmem=VMEM_LIMIT, flags=None):
    return pltpu.CompilerParams(dimension_semantics=sem, vmem_limit_bytes=vmem, flags=flags)


def _dot(a, b):
    return jnp.dot(a.astype(BF16), b.astype(BF16), preferred_element_type=F32)


def _dot_nt(a, b):
    return lax.dot_general(a.astype(BF16), b.astype(BF16), (((1,), (1,)), ((), ())),
                           preferred_element_type=F32)


def _dot_tn(a, b):
    return lax.dot_general(a.astype(BF16), b.astype(BF16), (((0,), (0,)), ((), ())),
                           preferred_element_type=F32)


def _mask_dot_exact(mask_bf16, x):
    hi = x.astype(BF16)
    r1 = x - hi.astype(F32)
    mid = r1.astype(BF16)
    lo = (r1 - mid.astype(F32)).astype(BF16)
    d = functools.partial(jnp.dot, preferred_element_type=F32)
    return (d(mask_bf16, hi) + d(mask_bf16, mid)) + d(mask_bf16, lo)


def _softplus(x):
    return jnp.maximum(x, 0.0) + jnp.log1p(jnp.exp(-jnp.abs(x)))


def _sigmoid(x):
    return 1.0 / (1.0 + jnp.exp(-x))


def _rms(x, g):
    return x * lax.rsqrt(jnp.mean(x * x, axis=-1, keepdims=True) + EPS) * g


def _log2(n):
    assert n & (n - 1) == 0 and n > 0, n
    return n.bit_length() - 1


def _pick(n, cands):
    for c in cands:
        if n % c == 0:
            return c
    return n


def _inproj_kernel(x_ref, g_ref, wb_ref, ws_ref, cst_ref, cw_ref, ob_ref, os_ref, cs_ref,
                   h_ref, prev_ref, *carry, kinds, R, T, tps, width, DK):
    i = pl.program_id(0)
    j = pl.program_id(1)
    tm, tn = ob_ref.shape
    nt = functools.partial(lax.dot_general, dimension_numbers=(((1,), (1,)), ((), ())),
                           preferred_element_type=F32)
    lo = SUBLANES - (width - 1)

    @pl.when(j == 0)
    def _():
        h = _rms(x_ref[...], g_ref[...]).astype(BF16)
        h_ref[...] = h
        os_ref[...] = nt(h, ws_ref[...])

    def conv_silu(x3, prev):
        r, t, _ = x3.shape
        cw = cw_ref[...]
        tok = lax.broadcasted_iota(jnp.int32, (r, SUBLANES, tn), 1)
        y = x3 * cw[width - 1:width, :]
        for k in range(1, width):
            rolled = pltpu.roll(x3, k, axis=1)
            head = jnp.where(tok >= k, rolled[:, :SUBLANES, :], pltpu.roll(prev, k, axis=1))
            shifted = head if t == SUBLANES else jnp.concatenate([head, rolled[:, SUBLANES:, :]], axis=1)
            y = y + shifted * cw[width - 1 - k:width - k, :]
        return y * _sigmoid(y)

    def l2norm_heads(y, scale):
        parts = []
        for h in range(tn // DK):
            a = y[:, h * DK:(h + 1) * DK]
            a = a * lax.rsqrt(jnp.sum(a * a, axis=-1, keepdims=True) + EPS)
            parts.append(a if scale == 1.0 else a * scale)
        return jnp.concatenate(parts, axis=1)

    nsub = next(n for n in (4, 2, 1) if tm % (n * 2 * LANES) == 0 or n == 1)
    tms = tm // nsub

    for idx, kind in enumerate(kinds):
        @pl.when(j == idx)
        def _():
            is_conv = kind in ("q", "k", "v")
            if is_conv:
                if tps == 1:
                    hist = cst_ref[...]
                else:
                    hist = jnp.where(i % tps == 0, cst_ref[...], carry[0][idx])
                prev_ref[...] = jnp.zeros_like(prev_ref)
                prev_ref[:, lo:SUBLANES, :] = hist
            last = None
            matmul = lambda s: nt(h_ref[s * tms:(s + 1) * tms, :], wb_ref[...])
            ahead = matmul(0)
            for s in range(nsub):
                acc = ahead
                if s + 1 < nsub:
                    ahead = matmul(s + 1)
                if is_conv:
                    if R == 1:
                        x3 = acc.reshape(1, tms, tn)
                        y = conv_silu(x3, prev_ref[...] if s == 0 else last)
                        last = x3[:, tms - SUBLANES:, :]
                    else:
                        rs = R // nsub
                        x3 = acc.reshape(rs, T, tn)
                        y = conv_silu(x3, prev_ref[s * rs:(s + 1) * rs])
                        cs_ref[s * rs:(s + 1) * rs] = x3[:, T - SUBLANES:, :][:, lo:, :]
                    y = y.reshape(tms, tn)
                    if kind == "q":
                        y = l2norm_heads(y, DK ** -0.5)
                    elif kind == "k":
                        y = l2norm_heads(y, 1.0)
                elif kind == "silu":
                    y = acc * _sigmoid(acc)
                elif kind == "sigmoid":
                    y = _sigmoid(acc)
                else:
                    y = acc
                ob_ref[s * tms:(s + 1) * tms, :] = y
            if is_conv and R == 1:
                new = last[:, lo:, :]
                cs_ref[...] = new
                if tps != 1:
                    carry[0][idx] = new


def _inproj(x, g, w_big_t, w_small_t, conv_state, conv_w, *, kinds, R, T, DK):
    m, d = x.shape
    nb = w_big_t.shape[0]
    tm = R * T
    tn = nb // len(kinds)
    width = conv_w.shape[0]
    n_conv = sum(kd in ("q", "k", "v") for kd in kinds)
    assert m % tm == 0 and tn % LANES == 0 and conv_state.shape[-1] == n_conv * tn
    assert kinds[:n_conv] == ("q", "k", "v")[:n_conv]
    n_tiles = m // tm
    tps = n_tiles * R // conv_state.shape[0]
    kern = functools.partial(_inproj_kernel, kinds=kinds, R=R, T=T, tps=tps, width=width, DK=DK)
    jc = lambda j: jnp.minimum(j, n_conv - 1)
    assert T % SUBLANES == 0 and width - 1 <= SUBLANES and (R == 1 or tps == 1)
    scratch = [pltpu.VMEM((tm, d), BF16), pltpu.VMEM((R, SUBLANES, tn), F32)]
    if tps != 1:
        scratch.append(pltpu.VMEM((n_conv, R, width - 1, tn), F32))
    big, small, cs = pl.pallas_call(
        kern,
        grid=(n_tiles, len(kinds)),
        in_specs=[
            pl.BlockSpec((tm, d), lambda i, j: (i, 0)),
            pl.BlockSpec((1, d), lambda i, j: (0, 0)),
            pl.BlockSpec((tn, d), lambda i, j: (j, 0)),
            pl.BlockSpec((LANES, d), lambda i, j: (0, 0)),
            pl.BlockSpec((R, width - 1, tn), lambda i, j: (i // tps, 0, jc(j))),
            pl.BlockSpec((width, tn), lambda i, j: (0, jc(j))),
        ],
        out_specs=[
            pl.BlockSpec((tm, tn), lambda i, j: (i, j)),
            pl.BlockSpec((tm, LANES), lambda i, j: (i, 0)),
            pl.BlockSpec((R, width - 1, tn), lambda i, j: (i, 0, jc(j))),
        ],
        out_shape=[jax.ShapeDtypeStruct((m, nb), F32), jax.ShapeDtypeStruct((m, LANES), F32),
                   jax.ShapeDtypeStruct((n_tiles * R, width - 1, n_conv * tn), F32)],
        scratch_shapes=scratch,
        compiler_params=_cparams(("arbitrary", "arbitrary")),
        name="inproj",
    )(x, g.reshape(1, d), w_big_t, w_small_t, conv_state, conv_w)
    return big, small, cs.reshape(n_tiles, R, width - 1, n_conv * tn)[tps - 1::tps].reshape(
        -1, width - 1, n_conv * tn)


def _seq_masks(R, T):
    RT = R * T
    sh = _log2(T)
    ri = lax.broadcasted_iota(jnp.int32, (RT, RT), 0)
    ci = lax.broadcasted_iota(jnp.int32, (RT, RT), 1)
    same = (ri >> sh) == (ci >> sh)
    tril = same & (ci <= ri)
    strict = same & (ci < ri)
    eye = ri == ci
    lastsel = same & ((ci & (T - 1)) == T - 1)
    rowseq = lax.broadcasted_iota(jnp.int32, (RT, 1), 0) >> sh
    return ri, ci, tril, strict, eye, lastsel, rowseq


def _row_form(col, eye):
    return jnp.sum(jnp.where(eye, col, 0.0), axis=0, keepdims=True)


def _at_last(row, lastsel):
    return jnp.sum(jnp.where(lastsel, row, 0.0), axis=1, keepdims=True)


def _unit_lower_inverses(lms, ri, ci, T):
    ds = None
    s = 1
    while s < T:
        sh = _log2(2 * s)
        blk = ((ri >> sh) == (ci >> sh)) & ((ri & (2 * s - 1)) >= s) & ((ci & (2 * s - 1)) < s)
        ms = [jnp.where(blk, lm, 0.0) for lm in lms]
        if ds is None:
            eye = jnp.where(ri == ci, 1.0, 0.0)
            ds = [eye - m for m in ms]
        else:
            dms = [_dot(d, m) for d, m in zip(ds, ms)]
            ds = [d - _dot(dm, d) for d, dm in zip(ds, dms)]
        s *= 2
    return ds


def _gdn_kernel(q_ref, k_ref, v_ref, z_ref, sm_ref, s0_ref, par_ref, ng_ref,
                o_ref, s_ref, *, R, T, H, DK, DV):
    RT = R * T

    @pl.when(pl.program_id(1) == 0)
    def _():
        s_ref[...] = s0_ref[...]

    qa = q_ref[...].reshape(RT, H * DK)
    ka = k_ref[...].reshape(RT, H * DK)
    va = v_ref[...].reshape(RT, H * DV)
    z = z_ref[...].reshape(RT, H * DV)

    sm = sm_ref[...].reshape(RT, LANES)
    beta_all = _sigmoid(sm)
    g_all = -jnp.exp(par_ref[1:2, :]) * _softplus(sm + par_ref[0:1, :])

    ri, ci, tril, strict, eye, lastsel, rowseq = _seq_masks(R, T)
    g_cum = _mask_dot_exact(jnp.where(tril, 1.0, 0.0).astype(BF16), g_all)

    hs = range(H)
    q = [qa[:, h * DK:(h + 1) * DK] for h in hs]
    k = [ka[:, h * DK:(h + 1) * DK] for h in hs]
    v = [va[:, h * DV:(h + 1) * DV] for h in hs]
    beta = [beta_all[:, h:h + 1] for h in hs]
    gc = [g_cum[:, H + h:H + h + 1] for h in hs]
    gr = [_row_form(a, eye) for a in gc]
    glast = [_at_last(a, lastsel) for a in gr]
    decay = [jnp.exp(jnp.where(tril, c - r, -jnp.inf)) for c, r in zip(gc, gr)]
    kb = [a * b for a, b in zip(k, beta)]
    lm = [jnp.where(strict, _dot_nt(a, b) * d, 0.0) for a, b, d in zip(kb, k, decay)]
    tinv = _unit_lower_inverses(lm, ri, ci, T)
    eg = [jnp.exp(a) for a in gc]
    uw = [_dot(t, jnp.concatenate([a * b, c * e], axis=1))
          for t, a, b, c, e in zip(tinv, v, beta, kb, eg)]
    u = [a[:, :DV] for a in uw]
    w = [a[:, DV:] for a in uw]
    qk = [_dot_nt(a, b) * d for a, b, d in zip(q, k, decay)]
    qe = [a * e for a, e in zip(q, eg)]
    kd = [a * jnp.exp(l - c) for a, l, c in zip(k, glast, gc)]
    gl = [jnp.exp(a) for a in glast]

    if R == 1:
        s_old = [s_ref[0, h] for h in hs]
        ws = [_dot(jnp.concatenate([a, b], axis=0), c) for a, b, c in zip(w, qe, s_old)]
        v_new = [a - b[:RT] for a, b in zip(u, ws)]
        o = [a[RT:] + _dot(b, c) for a, b, c in zip(ws, qk, v_new)]
        for h in hs:
            s_ref[0, h] = s_old[h] * gl[h][0:1, :] + _dot_tn(kd[h], v_new[h])
    else:
        ws = [[_dot(jnp.concatenate([w[h][s * T:(s + 1) * T], qe[h][s * T:(s + 1) * T]], axis=0),
                    s_ref[s, h]) for s in range(R)] for h in hs]
        v_new = [u[h] - jnp.concatenate([a[:T] for a in ws[h]], axis=0) for h in hs]
        o = [jnp.concatenate([a[T:] for a in ws[h]], axis=0) + _dot(qk[h], v_new[h]) for h in hs]
        for h in hs:
            for s in range(R):
                kd_s = jnp.where(rowseq == s, kd[h], 0.0)
                s_ref[s, h] = s_ref[s, h] * gl[h][s * T:s * T + 1, :] + _dot_tn(kd_s, v_new[h])

    for h in hs:
        out = _rms(o[h], ng_ref[...]) * z[:, h * DV:(h + 1) * DV]
        o_ref[:, h * DV:(h + 1) * DV] = out.astype(o_ref.dtype)


def _gdn(proj3, small3, s0, par, norm_g, *, R, T, N):
    B, H, DK, DV = s0.shape
    assert DK == DV and (H * DK) % LANES == 0
    W = H * DK
    rows = proj3.shape[0] * T
    kern = functools.partial(_gdn_kernel, R=R, T=T, H=H, DK=DK, DV=DV)
    col = lambda c: pl.BlockSpec((R, T, W), lambda b, n: (b * N + n, 0, c))
    return pl.pallas_call(
        kern,
        grid=(B // R, N),
        in_specs=[
            col(0), col(1), col(2), col(3),
            pl.BlockSpec((R, T, LANES), lambda b, n: (b * N + n, 0, 0)),
            pl.BlockSpec((R, H, DK, DV), lambda b, n: (b, 0, 0, 0)),
            pl.BlockSpec((SUBLANES, LANES), lambda b, n: (0, 0)),
            pl.BlockSpec((1, DV), lambda b, n: (0, 0)),
        ],
        out_specs=[
            pl.BlockSpec((R * T, H * DV), lambda b, n: (b * N + n, 0)),
            pl.BlockSpec((R, H, DK, DV), lambda b, n: (b, 0, 0, 0)),
        ],
        out_shape=[jax.ShapeDtypeStruct((rows, H * DV), BF16),
                   jax.ShapeDtypeStruct((B, H, DK, DV), F32)],
        compiler_params=_cparams(("parallel", "arbitrary")),
        name="gdn",
    )(proj3, proj3, proj3, proj3, small3, s0, par, norm_g.reshape(1, DV))


def _mlstm_kernel(q_ref, k_ref, v_ref, og_ref, sm_ref, c0_ref, n0_ref, m0_ref, par_ref, ng_ref,
                  o_ref, c_ref, n_ref, m_ref, *, R, T, H, DK, DV, off_i, off_f):
    RT = R * T

    @pl.when(pl.program_id(1) == 0)
    def _():
        c_ref[...] = c0_ref[...]
        n_ref[...] = n0_ref[...]
        m_ref[...] = m0_ref[...]

    pre = sm_ref[...].reshape(RT, LANES) + par_ref[0:1, :]
    logf_all = -_softplus(-pre)
    ri, ci, tril, strict, eye, lastsel, rowseq = _seq_masks(R, T)
    b_all = _mask_dot_exact(jnp.where(tril, 1.0, 0.0).astype(BF16), logf_all)

    qa = q_ref[...].reshape(RT, H * DK)
    ka = k_ref[...].reshape(RT, H * DK)
    va = v_ref[...].reshape(RT, H * DV)
    oa = og_ref[...].reshape(RT, H * DV)

    hs = range(H)
    q = [qa[:, h * DK:(h + 1) * DK] * (DK ** -0.5) for h in hs]
    k = [ka[:, h * DK:(h + 1) * DK] for h in hs]
    v = [va[:, h * DV:(h + 1) * DV] for h in hs]
    bc = [b_all[:, off_f + h:off_f + h + 1] for h in hs]
    ic = [pre[:, off_i + h:off_i + h + 1] for h in hs]
    dm = [jnp.where(tril, b + _row_form(i - b, eye), -jnp.inf) for b, i in zip(bc, ic)]

    if R == 1:
        m_rows = [m_ref[0, :, h:h + 1] for h in hs]
    else:
        m_rows = []
        for h in hs:
            mr = jnp.zeros((RT, 1), F32)
            for s in range(R):
                mr = jnp.where(rowseq == s, m_ref[s, :, h:h + 1], mr)
            m_rows.append(mr)
    m_new = [jnp.maximum(b + m, jnp.max(d, axis=1, keepdims=True)) for b, m, d in zip(bc, m_rows, dm)]
    inter = [jnp.exp(b + m - mn) for b, m, mn in zip(bc, m_rows, m_new)]
    smat = [_dot_nt(a, b) * jnp.exp(d - mn) for a, b, d, mn in zip(q, k, dm, m_new)]

    if R == 1:
        qc = [_dot(q[h], c_ref[0, h]) for h in hs]
        qn = [jnp.sum(q[h] * n_ref[0, h:h + 1, :], axis=1, keepdims=True) for h in hs]
    else:
        qc, qn = [], []
        for h in hs:
            rows = [slice(s * T, (s + 1) * T) for s in range(R)]
            qc.append(jnp.concatenate([_dot(q[h][r], c_ref[s, h]) for s, r in enumerate(rows)], axis=0))
            qn.append(jnp.concatenate(
                [jnp.sum(q[h][r] * n_ref[s, h:h + 1, :], axis=1, keepdims=True)
                 for s, r in enumerate(rows)], axis=0))
    num = [i * c + _dot(s, a) for i, c, s, a in zip(inter, qc, smat, v)]
    den = [i * n + jnp.sum(s, axis=1, keepdims=True) for i, n, s in zip(inter, qn, smat)]
    hh = [a / jnp.maximum(jnp.abs(b), jnp.exp(-mn)) for a, b, mn in zip(num, den, m_new)]

    b_last = [_at_last(_row_form(b, eye), lastsel) for b in bc]
    m_end = [_at_last(_row_form(mn, eye), lastsel) for mn in m_new]
    kw = [a * jnp.exp(bl - b + i - me) for a, bl, b, i, me in zip(k, b_last, bc, ic, m_end)]
    cs = [jnp.exp(bl + m - me) for bl, m, me in zip(b_last, m_rows, m_end)]
    for h in hs:
        for s in range(R):
            kw_s = kw[h] if R == 1 else jnp.where(rowseq == s, kw[h], 0.0)
            cs_s = cs[h][s * T:s * T + 1, :]
            c_ref[s, h] = cs_s * c_ref[s, h] + _dot_tn(kw_s, v[h])
            n_ref[s, h:h + 1, :] = cs_s * n_ref[s, h:h + 1, :] + jnp.sum(kw_s, axis=0, keepdims=True)
            m_ref[s, :, h:h + 1] = m_end[h][s * T:s * T + 1, :]

    for h in hs:
        out = _rms(hh[h], ng_ref[:, h * DV:(h + 1) * DV]) * oa[:, h * DV:(h + 1) * DV]
        o_ref[:, h * DV:(h + 1) * DV] = out.astype(o_ref.dtype)


def _mlstm(proj3, small3, c0, n0, m0, par, norm_g, *, R, T, N, col0, off_i, off_f):
    B, H, DK, DV = c0.shape
    QK, VW = H * DK, H * DV
    assert col0 % QK == 0 and (col0 + 2 * QK) % VW == 0 and QK % LANES == 0
    rows = proj3.shape[0] * T
    kern = functools.partial(_mlstm_kernel, R=R, T=T, H=H, DK=DK, DV=DV, off_i=off_i, off_f=off_f)
    qb = col0 // QK
    vb = (col0 + 2 * QK) // VW
    state_specs = [
        pl.BlockSpec((R, H, DK, DV), lambda b, n: (b, 0, 0, 0)),
        pl.BlockSpec((R, H, DK), lambda b, n: (b, 0, 0)),
        pl.BlockSpec((R, 1, H), lambda b, n: (b, 0, 0)),
    ]
    return pl.pallas_call(
        kern,
        grid=(B // R, N),
        in_specs=[
            pl.BlockSpec((R, T, QK), lambda b, n: (b * N + n, 0, qb)),
            pl.BlockSpec((R, T, QK), lambda b, n: (b * N + n, 0, qb + 1)),
            pl.BlockSpec((R, T, VW), lambda b, n: (b * N + n, 0, vb)),
            pl.BlockSpec((R, T, VW), lambda b, n: (b * N + n, 0, vb + 1)),
            pl.BlockSpec((R, T, LANES), lambda b, n: (b * N + n, 0, 0)),
            *state_specs,
            pl.BlockSpec((SUBLANES, LANES), lambda b, n: (0, 0)),
            pl.BlockSpec((1, VW), lambda b, n: (0, 0)),
        ],
        out_specs=[pl.BlockSpec((R * T, VW), lambda b, n: (b * N + n, 0)), *state_specs],
        out_shape=[jax.ShapeDtypeStruct((rows, VW), BF16),
                   jax.ShapeDtypeStruct(c0.shape, F32),
                   jax.ShapeDtypeStruct(n0.shape, F32),
                   jax.ShapeDtypeStruct(m0.shape, F32)],
        compiler_params=_cparams(("parallel", "arbitrary")),
        name="mlstm",
    )(proj3, proj3, proj3, proj3, small3, c0, n0, m0, par, norm_g.reshape(1, VW))


def _outproj_kernel(x_ref, a_ref, b_ref, wa_ref, wb_ref, o_ref):
    o_ref[...] = (x_ref[...] + jnp.dot(a_ref[...], wa_ref[...], preferred_element_type=F32)
                  + jnp.dot(b_ref[...], wb_ref[...], preferred_element_type=F32))


def _outproj(x, a, b, w_out):
    m, d = x.shape
    ka, kb = a.shape[1], b.shape[1]
    assert ka == kb
    tm = _pick(m, (512, 256, 128, 64, 32, 16, 8))
    return pl.pallas_call(
        _outproj_kernel,
        grid=(m // tm,),
        in_specs=[
            pl.BlockSpec((tm, d), lambda i: (i, 0)),
            pl.BlockSpec((tm, ka), lambda i: (i, 0)),
            pl.BlockSpec((tm, kb), lambda i: (i, 0)),
            pl.BlockSpec((ka, d), lambda i: (0, 0)),
            pl.BlockSpec((kb, d), lambda i: (1, 0)),
        ],
        out_specs=pl.BlockSpec((tm, d), lambda i: (i, 0)),
        out_shape=jax.ShapeDtypeStruct((m, d), F32),
        compiler_params=_cparams(("parallel",)),
        name="outproj",
    )(x, a, b, w_out, w_out)


def _ffn_kernel(x_ref, g_ref, wg_ref, wu_ref, cwg_ref, cwu_ref, cbg_ref, cbu_ref, wd_ref,
                stg_ref, stu_ref, gf_ref, y_ref, nsg_ref, nsu_ref, h_ref, act_ref, ext_ref, *carry,
                tm, shift, pad, tps, width, nj, final_norm):
    i = pl.program_id(0)
    j = pl.program_id(1)
    hist = (width - 1) * shift

    def activate_tile(slot):
        conv = []
        for idx, (w_ref, cw_ref, cb_ref, st_ref, ns_ref) in enumerate(
                ((wg_ref, cwg_ref, cbg_ref, stg_ref, nsg_ref), (wu_ref, cwu_ref, cbu_ref, stu_ref, nsu_ref))):
            u = jnp.dot(h_ref[...], w_ref[...], preferred_element_type=F32)
            ext_ref[idx, pad:pad + tm, :] = u
            if tps == 1:
                ext_ref[idx, pad - hist:pad, :] = st_ref[0]
            else:
                ext_ref[idx, pad - hist:pad, :] = jnp.where(i % tps == 0, st_ref[0], carry[0][j, idx])
            cw = cw_ref[...]
            c = u * cw[width - 1:width, :] + cb_ref[...]
            for t in range(width - 1):
                off = pad - hist + t * shift
                c = c + ext_ref[idx, off:off + tm, :] * cw[t:t + 1, :]
            new = u[tm - hist:, :]
            ns_ref[0] = new
            if tps != 1:
                carry[0][j, idx] = new
            conv.append(c)
        act_ref[slot] = (conv[0] * _sigmoid(conv[0]) * conv[1]).astype(BF16)

    def down_tile(slot):
        return jnp.dot(act_ref[slot], wd_ref[...], preferred_element_type=F32)

    @pl.when(j == 0)
    def _():
        h_ref[...] = _rms(x_ref[...], g_ref[...]).astype(BF16)
        y_ref[...] = jnp.zeros_like(y_ref)
        activate_tile(0)

    @pl.when((j > 0) & (j < nj))
    def _():
        y_ref[...] += down_tile(0)
        activate_tile(0)

    @pl.when(j == nj)
    def _():
        xo = x_ref[...] + y_ref[...] + down_tile(0)
        y_ref[...] = _rms(xo, gf_ref[...]) if final_norm else xo


def _ffn(x, norm_g, w_up, conv_w, conv_b, w_down, state, final_g, *, tm, shift, tps, final_norm):
    m, d = x.shape
    ff = w_down.shape[0]
    width = conv_w.shape[0]
    hist = (width - 1) * shift
    pad = -(-hist // SUBLANES) * SUBLANES
    tf = _pick(ff, (512, 256, 128))
    nj = ff // tf
    conv_b = conv_b.reshape(1, 2 * ff)
    kern = functools.partial(_ffn_kernel, tm=tm, shift=shift, pad=pad, tps=tps, width=width, nj=nj,
                             final_norm=final_norm)
    scratch = [pltpu.VMEM((tm, d), BF16), pltpu.VMEM((1, tm, tf), BF16), pltpu.VMEM((2, pad + tm, tf), F32)]
    cur = lambda j: jnp.minimum(j, nj - 1)
    prv = lambda j: jnp.maximum(j - 1, 0)
    if tps != 1:
        scratch.append(pltpu.VMEM((nj, 2, hist, tf), F32))
    st_shape = jax.ShapeDtypeStruct((m // tm, hist, ff), F32)
    y, nsg, nsu = pl.pallas_call(
        kern,
        grid=(m // tm, nj + 1),
        in_specs=[
            pl.BlockSpec((tm, d), lambda i, j: (i, 0)),
            pl.BlockSpec((1, d), lambda i, j: (0, 0)),
            pl.BlockSpec((d, tf), lambda i, j: (0, cur(j))),
            pl.BlockSpec((d, tf), lambda i, j: (0, nj + cur(j))),
            pl.BlockSpec((width, tf), lambda i, j: (0, cur(j))),
            pl.BlockSpec((width, tf), lambda i, j: (0, nj + cur(j))),
            pl.BlockSpec((1, tf), lambda i, j: (0, cur(j))),
            pl.BlockSpec((1, tf), lambda i, j: (0, nj + cur(j))),
            pl.BlockSpec((tf, d), lambda i, j: (prv(j), 0)),
            pl.BlockSpec((1, hist, tf), lambda i, j: (i // tps, 0, cur(j))),
            pl.BlockSpec((1, hist, tf), lambda i, j: (i // tps, 0, nj + cur(j))),
            pl.BlockSpec((1, d), lambda i, j: (0, 0)),
        ],
        out_specs=[
            pl.BlockSpec((tm, d), lambda i, j: (i, 0)),
            pl.BlockSpec((1, hist, tf), lambda i, j: (i, 0, cur(j))),
            pl.BlockSpec((1, hist, tf), lambda i, j: (i, 0, cur(j))),
        ],
        out_shape=[jax.ShapeDtypeStruct((m, d), F32), st_shape, st_shape],
        scratch_shapes=scratch,
        compiler_params=_cparams(("arbitrary", "arbitrary")),
        name="convffn",
    )(x, norm_g.reshape(1, d), w_up, w_up, conv_w, conv_w, conv_b, conv_b, w_down, state, state,
      final_g.reshape(1, d))
    return y, nsg[tps - 1::tps], nsu[tps - 1::tps]


def _layer(x, st, p, final_g, final_norm, decode):
    conv_buf, s0, c0, n0, m0, ffn_buf = st
    B, L, D = x.shape
    Hd, Hm = s0.shape[1], c0.shape[1]
    dn_qkv = conv_buf.shape[-1]
    dn_vw = Hd * s0.shape[3]
    ff = p["w_down"].shape[0]

    T = next((c for c in (MIX_CHUNK, CHUNK) if L % c == 0), L)
    N = L // T
    R = _pick(B, (8, 4, 2, 1)) if decode else 1
    if R * T > 128:
        R = 1

    x2 = x.reshape(B * L, D)
    width = p["dn_conv_w"].shape[0]
    assert L >= width - 1
    rp, tp = (B, L) if decode else (1, _pick(L, (1024, 512, 256, 128, 64, 32, 16, 8)))
    big, small, conv_new = _inproj(x2, p["norm_mix_g"], p["w_big"], p["w_small"], conv_buf,
                                   p["dn_conv_w"], kinds=PROJ_KINDS, R=rp, T=tp, DK=s0.shape[2])
    assert big.shape[1] == len(PROJ_KINDS) * dn_vw == dn_qkv + dn_vw + 3 * dn_vw
    big3 = big.reshape(B * N, T, big.shape[1])
    small3 = small.reshape(B * N, T, LANES)

    o_dn, s_new = _gdn(big3, small3, s0, p["par"], p["dn_norm_g"], R=R, T=T, N=N)
    o_ml, c_new, n_new, m_new = _mlstm(big3, small3, c0, n0, m0.reshape(B, 1, Hm), p["par"],
                                       p["ml_norm_g"], R=R, T=T, N=N, col0=dn_qkv + dn_vw,
                                       off_i=2 * Hd, off_f=2 * Hd + Hm)

    x1 = _outproj(x2, o_dn, o_ml, p["w_out"])

    fw = p["ffn_conv_w"].shape[0]
    if decode:
        x1t = x1.reshape(B, L, D).transpose(1, 0, 2).reshape(L * B, D)
        state = ffn_buf.transpose(1, 0, 2).reshape(1, (fw - 1) * B, 2 * ff)
        y, nsg, nsu = _ffn(x1t, p["norm_ffn_g"], p["w_up"], p["ffn_conv_w"], p["ffn_conv_b"],
                           p["w_down"], state, final_g, tm=L * B, shift=B, tps=1,
                           final_norm=final_norm)
        y = y.reshape(L, B, D).transpose(1, 0, 2)
        ffn_new = jnp.concatenate([nsg, nsu], axis=-1).reshape(fw - 1, B, 2 * ff).transpose(1, 0, 2)
    else:
        tm = _pick(L, (512, 256, 128, 64, 32, 16, 8))
        y, nsg, nsu = _ffn(x1, p["norm_ffn_g"], p["w_up"], p["ffn_conv_w"], p["ffn_conv_b"],
                           p["w_down"], ffn_buf, final_g, tm=tm, shift=1, tps=L // tm,
                           final_norm=final_norm)
        y = y.reshape(B, L, D)
        ffn_new = jnp.concatenate([nsg, nsu], axis=-1)
    return y, (conv_new, s_new, c_new, n_new, m_new.reshape(B, Hm), ffn_new)


def _regroup_kernel(w_ref, big_ref, small_ref, *, c0, c1, c2):
    n = w_ref.shape[0]
    big_ref[:c0, :] = w_ref[:c0, :].astype(BF16)
    big_ref[c0:, :] = w_ref[c1:c2, :].astype(BF16)
    narrow = jnp.concatenate(
        [w_ref[c0:c1, :], w_ref[c2:, :], jnp.zeros((LANES - (c1 - c0) - (n - c2), w_ref.shape[1]), F32)],
        axis=0)
    small_ref[...] = narrow.astype(BF16)


def _regroup(w_t, c0, c1, c2):
    n, k = w_t.shape
    tk = _pick(k, (256, 128))
    nb = c0 + c2 - c1
    assert c0 % 16 == 0 and c1 % SUBLANES == 0 and c2 % SUBLANES == 0
    return pl.pallas_call(
        functools.partial(_regroup_kernel, c0=c0, c1=c1, c2=c2),
        grid=(k // tk,),
        in_specs=[pl.BlockSpec((n, tk), lambda i: (0, i))],
        out_specs=[pl.BlockSpec((nb, tk), lambda i: (0, i)), pl.BlockSpec((LANES, tk), lambda i: (0, i))],
        out_shape=[jax.ShapeDtypeStruct((nb, k), BF16), jax.ShapeDtypeStruct((LANES, k), BF16)],
        compiler_params=_cparams(("parallel",)),
        name="regroup",
    )(w_t)


def _prep_params(l, Hd, Hm, dn_qkv, dn_vw, ml_qk, ml_vw, norm_mix_g, w_in, dn_conv_w, dn_A_log,
                 dn_dt_bias, dn_norm_g, ml_i_bias, ml_f_bias, ml_norm_g, w_out, norm_ffn_g, w_up,
                 ffn_conv_w, ffn_conv_b, w_down):
    w = w_in[l]
    c0 = dn_qkv + dn_vw
    c1 = c0 + 2 * Hd
    c2 = c1 + 2 * ml_qk + 2 * ml_vw
    nsmall = 2 * Hd + 2 * Hm
    assert nsmall <= LANES and w.shape[1] == c2 + 2 * Hm
    w_big, w_small = _regroup(w.T, c0, c1, c2)
    z = lambda n: jnp.zeros((n,), F32)
    bias = jnp.concatenate([z(Hd), dn_dt_bias[l], ml_i_bias[l], ml_f_bias[l], z(LANES - nsmall)])
    alog = jnp.concatenate([z(Hd), dn_A_log[l], z(LANES - 2 * Hd)])
    par = jnp.zeros((SUBLANES, LANES), F32).at[0].set(bias).at[1].set(alog)
    return dict(norm_mix_g=norm_mix_g[l], w_big=w_big, w_small=w_small, dn_conv_w=dn_conv_w[l],
                par=par, dn_norm_g=dn_norm_g[l], ml_norm_g=ml_norm_g[l], w_out=w_out[l].astype(BF16),
                norm_ffn_g=norm_ffn_g[l], w_up=w_up[l].astype(BF16), ffn_conv_w=ffn_conv_w[l],
                ffn_conv_b=ffn_conv_b[l], w_down=w_down[l].astype(BF16))


def kernel(x_prompt, x_sample, state_dn_conv, state_dn_S, state_ml_C, state_ml_n, state_ml_m,
           state_ffn_conv, norm_mix_g, w_in, dn_conv_w, dn_A_log, dn_dt_bias, dn_norm_g,
           ml_i_bias, ml_f_bias, ml_norm_g, w_out, norm_ffn_g, w_up, ffn_conv_w, ffn_conv_b,
           w_down, norm_final_g):
    states = (state_dn_conv, state_dn_S, state_ml_C, state_ml_n, state_ml_m, state_ffn_conv)
    depth = w_in.shape[0]
    batch = x_prompt.shape[0]
    Hd, dk, dv = state_dn_S.shape[2:]
    Hm, mdk, mdv = state_ml_C.shape[2:]
    xp, xs = x_prompt, x_sample
    p_new = [[] for _ in states]
    s_new = [[] for _ in states]
    for l in range(depth):
        p = _prep_params(l, Hd, Hm, state_dn_conv.shape[-1], Hd * dv, Hm * mdk, Hm * mdv,
                         norm_mix_g, w_in, dn_conv_w, dn_A_log, dn_dt_bias, dn_norm_g, ml_i_bias,
                         ml_f_bias, ml_norm_g, w_out, norm_ffn_g, w_up, ffn_conv_w, ffn_conv_b,
                         w_down)
        last = l == depth - 1
        st_p = tuple(jnp.zeros((batch,) + s.shape[2:], s.dtype) for s in states)
        st_s = tuple(s[l] for s in states)
        xp, np_st = _layer(xp, st_p, p, norm_final_g, last, decode=False)
        xs, ns_st = _layer(xs, st_s, p, norm_final_g, last, decode=True)
        for i in range(len(states)):
            p_new[i].append(np_st[i])
            s_new[i].append(ns_st[i])
    outs_p = tuple(jnp.stack(a, axis=0) for a in p_new)
    outs_s = tuple(jnp.stack(a, axis=0) for a in s_new)
    return (xp, xs) + outs_p + outs_s
```

```python
import functools

import jax
import jax.numpy as jnp
from jax import lax
from jax.experimental import pallas as pl
from jax.experimental.pallas import tpu as pltpu

EPS = 1e-6
CHUNK = 64
MIX_CHUNK = 128
F32 = jnp.float32
BF16 = jnp.bfloat16
LANES = 128
SUBLANES = 8
VMEM_LIMIT = 56 * 1024 * 1024
WIDE_STATE_BYTES = 2 * 1024 * 1024


def _cparams(sem, vmem=VMEM_LIMIT, flags=None):
    return pltpu.CompilerParams(dimension_semantics=sem, vmem_limit_bytes=vmem, flags=flags)


def _dot(a, b):
    return jnp.dot(a.astype(BF16), b.astype(BF16), preferred_element_type=F32)


def _dot_nt(a, b):
    return lax.dot_general(a.astype(BF16), b.astype(BF16), (((1,), (1,)), ((), ())),
                           preferred_element_type=F32)


def _dot_tn(a, b):
    return lax.dot_general(a.astype(BF16), b.astype(BF16), (((0,), (0,)), ((), ())),
                           preferred_element_type=F32)


def _mask_dot_exact(mask_bf16, x):
    hi = x.astype(BF16)
    r1 = x - hi.astype(F32)
    mid = r1.astype(BF16)
    lo = (r1 - mid.astype(F32)).astype(BF16)
    d = functools.partial(jnp.dot, preferred_element_type=F32)
    return (d(mask_bf16, hi) + d(mask_bf16, mid)) + d(mask_bf16, lo)


def _softplus(x):
    return jnp.maximum(x, 0.0) + jnp.log1p(jnp.exp(-jnp.abs(x)))


def _sigmoid(x):
    return 1.0 / (1.0 + jnp.exp(-x))


def _rms(x, g):
    return x * lax.rsqrt(jnp.mean(x * x, axis=-1, keepdims=True) + EPS) * g


def _log2(n):
    assert n & (n - 1) == 0 and n > 0, n
    return n.bit_length() - 1


def _pick(n, cands):
    for c in cands:
        if n % c == 0:
            return c
    return n


def _inproj_kernel(x_ref, g_ref, wb_ref, ws_ref, ob_ref, os_ref, h_ref):
    nt = functools.partial(lax.dot_general, dimension_numbers=(((1,), (1,)), ((), ())),
                           preferred_element_type=F32)

    @pl.when(pl.program_id(1) == 0)
    def _():
        h = _rms(x_ref[...], g_ref[...]).astype(BF16)
        h_ref[...] = h
        os_ref[...] = nt(h, ws_ref[...])

    ob_ref[...] = nt(h_ref[...], wb_ref[...])


def _inproj(x, g, w_big_t, w_small_t):
    m, d = x.shape
    nb = w_big_t.shape[0]
    tm = _pick(m, (1024, 512, 256, 128, 64, 32, 16, 8))
    tn = _pick(nb, (1024, 512, 256, 128))
    return pl.pallas_call(
        _inproj_kernel,
        grid=(m // tm, nb // tn),
        in_specs=[
            pl.BlockSpec((tm, d), lambda i, j: (i, 0)),
            pl.BlockSpec((1, d), lambda i, j: (0, 0)),
            pl.BlockSpec((tn, d), lambda i, j: (j, 0)),
            pl.BlockSpec((LANES, d), lambda i, j: (0, 0)),
        ],
        out_specs=[
            pl.BlockSpec((tm, tn), lambda i, j: (i, j)),
            pl.BlockSpec((tm, LANES), lambda i, j: (i, 0)),
        ],
        out_shape=[jax.ShapeDtypeStruct((m, nb), F32), jax.ShapeDtypeStruct((m, LANES), F32)],
        scratch_shapes=[pltpu.VMEM((tm, d), BF16)],
        compiler_params=_cparams(("parallel", "arbitrary")),
        name="inproj",
    )(x, g.reshape(1, d), w_big_t, w_small_t)


def _seq_masks(R, T):
    RT = R * T
    sh = _log2(T)
    ri = lax.broadcasted_iota(jnp.int32, (RT, RT), 0)
    ci = lax.broadcasted_iota(jnp.int32, (RT, RT), 1)
    same = (ri >> sh) == (ci >> sh)
    tril = same & (ci <= ri)
    strict = same & (ci < ri)
    eye = ri == ci
    lastsel = same & ((ci & (T - 1)) == T - 1)
    rowseq = lax.broadcasted_iota(jnp.int32, (RT, 1), 0) >> sh
    return ri, ci, tril, strict, eye, lastsel, rowseq


def _row_form(col, eye):
    return jnp.sum(jnp.where(eye, col, 0.0), axis=0, keepdims=True)


def _at_last(row, lastsel):
    return jnp.sum(jnp.where(lastsel, row, 0.0), axis=1, keepdims=True)


def _unit_lower_inverses(lms, ri, ci, T):
    ds = None
    s = 1
    while s < T:
        sh = _log2(2 * s)
        blk = ((ri >> sh) == (ci >> sh)) & ((ri & (2 * s - 1)) >= s) & ((ci & (2 * s - 1)) < s)
        ms = [jnp.where(blk, lm, 0.0) for lm in lms]
        if ds is None:
            eye = jnp.where(ri == ci, 1.0, 0.0)
            ds = [eye - m for m in ms]
        else:
            dms = [_dot(d, m) for d, m in zip(ds, ms)]
            ds = [d - _dot(dm, d) for d, dm in zip(ds, dms)]
        s *= 2
    return ds


def _causal_conv(ext_ref, x_ref, st_ref, cw_ref, first, T, width):
    lo = SUBLANES - (width - 1)

    @pl.when(first)
    def _():
        ext_ref[:, lo:SUBLANES, :] = st_ref[...]

    ext_ref[:, SUBLANES:SUBLANES + T, :] = x_ref[...]
    cw = cw_ref[...]
    acc = None
    for j in range(width):
        term = ext_ref[:, lo + j:lo + j + T, :] * cw[j:j + 1, :]
        acc = term if acc is None else acc + term
    ext_ref[:, lo:SUBLANES, :] = ext_ref[:, lo + T:SUBLANES + T, :]
    return acc


def _gdn_kernel(qkv_ref, z_ref, sm_ref, cst_ref, s0_ref, cw_ref, par_ref, ng_ref,
                o_ref, s_ref, ext_ref, *, R, T, H, DK, DV, width):
    RT = R * T
    QK = H * DK
    first = pl.program_id(1) == 0

    @pl.when(first)
    def _():
        s_ref[...] = s0_ref[...]

    y = _causal_conv(ext_ref, qkv_ref, cst_ref, cw_ref, first, T, width)
    y = y.reshape(RT, y.shape[-1])
    y = y * _sigmoid(y)
    z = z_ref[...].reshape(RT, H * DV)

    sm = sm_ref[...].reshape(RT, LANES)
    beta_all = _sigmoid(sm)
    g_all = -jnp.exp(par_ref[1:2, :]) * _softplus(sm + par_ref[0:1, :])

    ri, ci, tril, strict, eye, lastsel, rowseq = _seq_masks(R, T)
    g_cum = _mask_dot_exact(jnp.where(tril, 1.0, 0.0).astype(BF16), g_all)

    hs = range(H)
    q = [y[:, h * DK:(h + 1) * DK] for h in hs]
    k = [y[:, QK + h * DK:QK + (h + 1) * DK] for h in hs]
    v = [y[:, 2 * QK + h * DV:2 * QK + (h + 1) * DV] for h in hs]
    q = [a * lax.rsqrt(jnp.sum(a * a, axis=-1, keepdims=True) + EPS) * (DK ** -0.5) for a in q]
    k = [a * lax.rsqrt(jnp.sum(a * a, axis=-1, keepdims=True) + EPS) for a in k]
    beta = [beta_all[:, h:h + 1] for h in hs]
    gc = [g_cum[:, H + h:H + h + 1] for h in hs]
    gr = [_row_form(a, eye) for a in gc]
    glast = [_at_last(a, lastsel) for a in gr]
    decay = [jnp.exp(jnp.where(tril, c - r, -jnp.inf)) for c, r in zip(gc, gr)]
    kb = [a * b for a, b in zip(k, beta)]
    lm = [jnp.where(strict, _dot_nt(a, b) * d, 0.0) for a, b, d in zip(kb, k, decay)]
    tinv = _unit_lower_inverses(lm, ri, ci, T)
    eg = [jnp.exp(a) for a in gc]
    uw = [_dot(t, jnp.concatenate([a * b, c * e], axis=1))
          for t, a, b, c, e in zip(tinv, v, beta, kb, eg)]
    u = [a[:, :DV] for a in uw]
    w = [a[:, DV:] for a in uw]
    qk = [_dot_nt(a, b) * d for a, b, d in zip(q, k, decay)]
    qe = [a * e for a, e in zip(q, eg)]
    kd = [a * jnp.exp(l - c) for a, l, c in zip(k, glast, gc)]
    gl = [jnp.exp(a) for a in glast]

    if R == 1:
        s_old = [s_ref[0, h] for h in hs]
        ws = [_dot(jnp.concatenate([a, b], axis=0), c) for a, b, c in zip(w, qe, s_old)]
        v_new = [a - b[:RT] for a, b in zip(u, ws)]
        o = [a[RT:] + _dot(b, c) for a, b, c in zip(ws, qk, v_new)]
        for h in hs:
            s_ref[0, h] = s_old[h] * gl[h][0:1, :] + _dot_tn(kd[h], v_new[h])
    else:
        ws = [[_dot(jnp.concatenate([w[h][s * T:(s + 1) * T], qe[h][s * T:(s + 1) * T]], axis=0),
                    s_ref[s, h]) for s in range(R)] for h in hs]
        v_new = [u[h] - jnp.concatenate([a[:T] for a in ws[h]], axis=0) for h in hs]
        o = [jnp.concatenate([a[T:] for a in ws[h]], axis=0) + _dot(qk[h], v_new[h]) for h in hs]
        for h in hs:
            for s in range(R):
                kd_s = jnp.where(rowseq == s, kd[h], 0.0)
                s_ref[s, h] = s_ref[s, h] * gl[h][s * T:s * T + 1, :] + _dot_tn(kd_s, v_new[h])

    for h in hs:
        zh = z[:, h * DV:(h + 1) * DV]
        out = _rms(o[h], ng_ref[...]) * (zh * _sigmoid(zh))
        o_ref[:, h * DV:(h + 1) * DV] = out.astype(o_ref.dtype)


def _gdn(proj3, small3, conv_state, s0, conv_w, par, norm_g, *, R, T, N):
    B, H, DK, DV = s0.shape
    QKV = conv_state.shape[-1]
    width = conv_w.shape[0]
    assert QKV == 2 * H * DK + H * DV and QKV % LANES == 0 and (H * DV) % LANES == 0
    assert QKV % (H * DV) == 0
    zblk = QKV // (H * DV)
    rows = proj3.shape[0] * T
    kern = functools.partial(_gdn_kernel, R=R, T=T, H=H, DK=DK, DV=DV, width=width)
    return pl.pallas_call(
        kern,
        grid=(B // R, N),
        in_specs=[
            pl.BlockSpec((R, T, QKV), lambda b, n: (b * N + n, 0, 0)),
            pl.BlockSpec((R, T, H * DV), lambda b, n: (b * N + n, 0, zblk)),
            pl.BlockSpec((R, T, LANES), lambda b, n: (b * N + n, 0, 0)),
            pl.BlockSpec((R, width - 1, QKV), lambda b, n: (b, 0, 0)),
            pl.BlockSpec((R, H, DK, DV), lambda b, n: (b, 0, 0, 0)),
            pl.BlockSpec((width, QKV), lambda b, n: (0, 0)),
            pl.BlockSpec((SUBLANES, LANES), lambda b, n: (0, 0)),
            pl.BlockSpec((1, DV), lambda b, n: (0, 0)),
        ],
        out_specs=[
            pl.BlockSpec((R * T, H * DV), lambda b, n: (b * N + n, 0)),
            pl.BlockSpec((R, H, DK, DV), lambda b, n: (b, 0, 0, 0)),
        ],
        out_shape=[jax.ShapeDtypeStruct((rows, H * DV), BF16),
                   jax.ShapeDtypeStruct((B, H, DK, DV), F32)],
        scratch_shapes=[pltpu.VMEM((R, SUBLANES + T, QKV), F32)],
        compiler_params=_cparams(("parallel", "arbitrary")),
        name="gdn",
    )(proj3, proj3, small3, conv_state, s0, conv_w, par, norm_g.reshape(1, DV))


def _mlstm_kernel(q_ref, k_ref, v_ref, og_ref, sm_ref, c0_ref, n0_ref, m0_ref, par_ref, ng_ref,
                  o_ref, c_ref, n_ref, m_ref, *, R, T, H, DK, DV, off_i, off_f):
    RT = R * T

    @pl.when(pl.program_id(1) == 0)
    def _():
        c_ref[...] = c0_ref[...]
        n_ref[...] = n0_ref[...]
        m_ref[...] = m0_ref[...]

    pre = sm_ref[...].reshape(RT, LANES) + par_ref[0:1, :]
    logf_all = -_softplus(-pre)
    ri, ci, tril, strict, eye, lastsel, rowseq = _seq_masks(R, T)
    b_all = _mask_dot_exact(jnp.where(tril, 1.0, 0.0).astype(BF16), logf_all)

    qa = q_ref[...].reshape(RT, H * DK)
    ka = k_ref[...].reshape(RT, H * DK)
    va = v_ref[...].reshape(RT, H * DV)
    oa = og_ref[...].reshape(RT, H * DV)

    hs = range(H)
    q = [qa[:, h * DK:(h + 1) * DK] * (DK ** -0.5) for h in hs]
    k = [ka[:, h * DK:(h + 1) * DK] for h in hs]
    v = [va[:, h * DV:(h + 1) * DV] for h in hs]
    bc = [b_all[:, off_f + h:off_f + h + 1] for h in hs]
    ic = [pre[:, off_i + h:off_i + h + 1] for h in hs]
    dm = [jnp.where(tril, b + _row_form(i - b, eye), -jnp.inf) for b, i in zip(bc, ic)]

    if R == 1:
        m_rows = [m_ref[0, :, h:h + 1] for h in hs]
    else:
        m_rows = []
        for h in hs:
            mr = jnp.zeros((RT, 1), F32)
            for s in range(R):
                mr = jnp.where(rowseq == s, m_ref[s, :, h:h + 1], mr)
            m_rows.append(mr)
    m_new = [jnp.maximum(b + m, jnp.max(d, axis=1, keepdims=True)) for b, m, d in zip(bc, m_rows, dm)]
    inter = [jnp.exp(b + m - mn) for b, m, mn in zip(bc, m_rows, m_new)]
    smat = [_dot_nt(a, b) * jnp.exp(d - mn) for a, b, d, mn in zip(q, k, dm, m_new)]

    if R == 1:
        qc = [_dot(q[h], c_ref[0, h]) for h in hs]
        qn = [jnp.sum(q[h] * n_ref[0, h:h + 1, :], axis=1, keepdims=True) for h in hs]
    else:
        qc, qn = [], []
        for h in hs:
            rows = [slice(s * T, (s + 1) * T) for s in range(R)]
            qc.append(jnp.concatenate([_dot(q[h][r], c_ref[s, h]) for s, r in enumerate(rows)], axis=0))
            qn.append(jnp.concatenate(
                [jnp.sum(q[h][r] * n_ref[s, h:h + 1, :], axis=1, keepdims=True)
                 for s, r in enumerate(rows)], axis=0))
    num = [i * c + _dot(s, a) for i, c, s, a in zip(inter, qc, smat, v)]
    den = [i * n + jnp.sum(s, axis=1, keepdims=True) for i, n, s in zip(inter, qn, smat)]
    hh = [a / jnp.maximum(jnp.abs(b), jnp.exp(-mn)) for a, b, mn in zip(num, den, m_new)]

    b_last = [_at_last(_row_form(b, eye), lastsel) for b in bc]
    m_end = [_at_last(_row_form(mn, eye), lastsel) for mn in m_new]
    kw = [a * jnp.exp(bl - b + i - me) for a, bl, b, i, me in zip(k, b_last, bc, ic, m_end)]
    cs = [jnp.exp(bl + m - me) for bl, m, me in zip(b_last, m_rows, m_end)]
    for h in hs:
        for s in range(R):
            kw_s = kw[h] if R == 1 else jnp.where(rowseq == s, kw[h], 0.0)
            cs_s = cs[h][s * T:s * T + 1, :]
            c_ref[s, h] = cs_s * c_ref[s, h] + _dot_tn(kw_s, v[h])
            n_ref[s, h:h + 1, :] = cs_s * n_ref[s, h:h + 1, :] + jnp.sum(kw_s, axis=0, keepdims=True)
            m_ref[s, :, h:h + 1] = m_end[h][s * T:s * T + 1, :]

    for h in hs:
        og = oa[:, h * DV:(h + 1) * DV]
        out = _rms(hh[h], ng_ref[:, h * DV:(h + 1) * DV]) * _sigmoid(og)
        o_ref[:, h * DV:(h + 1) * DV] = out.astype(o_ref.dtype)


def _mlstm(proj3, small3, c0, n0, m0, par, norm_g, *, R, T, N, col0, off_i, off_f):
    B, H, DK, DV = c0.shape
    QK, VW = H * DK, H * DV
    assert col0 % QK == 0 and (col0 + 2 * QK) % VW == 0 and QK % LANES == 0
    rows = proj3.shape[0] * T
    kern = functools.partial(_mlstm_kernel, R=R, T=T, H=H, DK=DK, DV=DV, off_i=off_i, off_f=off_f)
    qb = col0 // QK
    vb = (col0 + 2 * QK) // VW
    state_specs = [
        pl.BlockSpec((R, H, DK, DV), lambda b, n: (b, 0, 0, 0)),
        pl.BlockSpec((R, H, DK), lambda b, n: (b, 0, 0)),
        pl.BlockSpec((R, 1, H), lambda b, n: (b, 0, 0)),
    ]
    return pl.pallas_call(
        kern,
        grid=(B // R, N),
        in_specs=[
            pl.BlockSpec((R, T, QK), lambda b, n: (b * N + n, 0, qb)),
            pl.BlockSpec((R, T, QK), lambda b, n: (b * N + n, 0, qb + 1)),
            pl.BlockSpec((R, T, VW), lambda b, n: (b * N + n, 0, vb)),
            pl.BlockSpec((R, T, VW), lambda b, n: (b * N + n, 0, vb + 1)),
            pl.BlockSpec((R, T, LANES), lambda b, n: (b * N + n, 0, 0)),
            *state_specs,
            pl.BlockSpec((SUBLANES, LANES), lambda b, n: (0, 0)),
            pl.BlockSpec((1, VW), lambda b, n: (0, 0)),
        ],
        out_specs=[pl.BlockSpec((R * T, VW), lambda b, n: (b * N + n, 0)), *state_specs],
        out_shape=[jax.ShapeDtypeStruct((rows, VW), BF16),
                   jax.ShapeDtypeStruct(c0.shape, F32),
                   jax.ShapeDtypeStruct(n0.shape, F32),
                   jax.ShapeDtypeStruct(m0.shape, F32)],
        compiler_params=_cparams(("parallel", "arbitrary")),
        name="mlstm",
    )(proj3, proj3, proj3, proj3, small3, c0, n0, m0, par, norm_g.reshape(1, VW))


def _outproj_kernel(x_ref, a_ref, b_ref, wa_ref, wb_ref, o_ref):
    o_ref[...] = (x_ref[...] + jnp.dot(a_ref[...], wa_ref[...], preferred_element_type=F32)
                  + jnp.dot(b_ref[...], wb_ref[...], preferred_element_type=F32))


def _outproj(x, a, b, w_out):
    m, d = x.shape
    ka, kb = a.shape[1], b.shape[1]
    assert ka == kb
    tm = _pick(m, (512, 256, 128, 64, 32, 16, 8))
    return pl.pallas_call(
        _outproj_kernel,
        grid=(m // tm,),
        in_specs=[
            pl.BlockSpec((tm, d), lambda i: (i, 0)),
            pl.BlockSpec((tm, ka), lambda i: (i, 0)),
            pl.BlockSpec((tm, kb), lambda i: (i, 0)),
            pl.BlockSpec((ka, d), lambda i: (0, 0)),
            pl.BlockSpec((kb, d), lambda i: (1, 0)),
        ],
        out_specs=pl.BlockSpec((tm, d), lambda i: (i, 0)),
        out_shape=jax.ShapeDtypeStruct((m, d), F32),
        compiler_params=_cparams(("parallel",)),
        name="outproj",
    )(x, a, b, w_out, w_out)


def _ffn_kernel(x_ref, g_ref, wg_ref, wu_ref, cw_ref, cb_ref, wd_ref, stg_ref, stu_ref, gf_ref,
                y_ref, nsg_ref, nsu_ref, h_ref, act_ref, ext_ref, *carry,
                tm, tf, ff, shift, pad, tps, width, nj, wide_state, final_norm):
    i = pl.program_id(0)
    j = pl.program_id(1)
    hist = (width - 1) * shift

    def activate_tile():
        conv = []
        for idx, (w_ref, st_ref, ns_ref) in enumerate(
                ((wg_ref, stg_ref, nsg_ref), (wu_ref, stu_ref, nsu_ref))):
            col = pl.ds(pl.multiple_of(idx * ff + j * tf, LANES), tf)
            st_col = col if wide_state else slice(None)
            u = jnp.dot(h_ref[...], w_ref[...], preferred_element_type=F32)
            ext_ref[idx, pad:pad + tm, :] = u
            if tps == 1:
                ext_ref[idx, pad - hist:pad, :] = st_ref[0, :, st_col]
            else:
                ext_ref[idx, pad - hist:pad, :] = jnp.where(i % tps == 0, st_ref[0, :, st_col],
                                                            carry[0][j, idx])
            cw = cw_ref[:, col]
            c = u * cw[width - 1:width, :] + cb_ref[:, col]
            for t in range(width - 1):
                off = pad - hist + t * shift
                c = c + ext_ref[idx, off:off + tm, :] * cw[t:t + 1, :]
            new = u[tm - hist:, :]
            ns_ref[0, :, st_col] = new
            if tps != 1:
                carry[0][j, idx] = new
            conv.append(c)
        act_ref[...] = (conv[0] * _sigmoid(conv[0]) * conv[1]).astype(BF16)

    def down_tile():
        return jnp.dot(act_ref[...], wd_ref[...], preferred_element_type=F32)

    @pl.when(j == 0)
    def _():
        h_ref[...] = _rms(x_ref[...], g_ref[...]).astype(BF16)
        y_ref[...] = jnp.zeros_like(y_ref)
        activate_tile()

    @pl.when((j > 0) & (j < nj))
    def _():
        y_ref[...] += down_tile()
        activate_tile()

    @pl.when(j == nj)
    def _():
        xo = x_ref[...] + y_ref[...] + down_tile()
        y_ref[...] = _rms(xo, gf_ref[...]) if final_norm else xo


def _ffn(x, norm_g, w_up, conv_w, conv_b, w_down, state, final_g, *, tm, shift, tps, final_norm):
    m, d = x.shape
    ff = w_down.shape[0]
    width = conv_w.shape[0]
    hist = (width - 1) * shift
    pad = -(-hist // SUBLANES) * SUBLANES
    tf = _pick(ff, (512, 256, 128))
    nj = ff // tf
    n_tiles = m // tm
    wide_state = hist * 2 * ff * 4 <= WIDE_STATE_BYTES
    kern = functools.partial(_ffn_kernel, tm=tm, tf=tf, ff=ff, shift=shift, pad=pad, tps=tps,
                             width=width, nj=nj, wide_state=wide_state, final_norm=final_norm)
    scratch = [pltpu.VMEM((tm, d), BF16), pltpu.VMEM((tm, tf), BF16), pltpu.VMEM((2, pad + tm, tf), F32)]
    cur = lambda j: jnp.minimum(j, nj - 1)
    prv = lambda j: jnp.maximum(j - 1, 0)
    if tps != 1:
        scratch.append(pltpu.VMEM((nj, 2, hist, tf), F32))
    if wide_state:
        st_in = [pl.BlockSpec((1, hist, 2 * ff), lambda i, j: (i // tps, 0, 0))] * 2
        st_out = [pl.BlockSpec((1, hist, 2 * ff), lambda i, j: (i, 0, 0))]
        st_shape = [jax.ShapeDtypeStruct((n_tiles, hist, 2 * ff), F32)]
    else:
        st_in = [pl.BlockSpec((1, hist, tf), lambda i, j: (i // tps, 0, cur(j))),
                 pl.BlockSpec((1, hist, tf), lambda i, j: (i // tps, 0, nj + cur(j)))]
        st_out = [pl.BlockSpec((1, hist, tf), lambda i, j: (i, 0, cur(j)))] * 2
        st_shape = [jax.ShapeDtypeStruct((n_tiles, hist, ff), F32)] * 2
    call = pl.pallas_call(
        kern if not wide_state else _single_history(kern),
        grid=(n_tiles, nj + 1),
        in_specs=[
            pl.BlockSpec((tm, d), lambda i, j: (i, 0)),
            pl.BlockSpec((1, d), lambda i, j: (0, 0)),
            pl.BlockSpec((d, tf), lambda i, j: (0, cur(j))),
            pl.BlockSpec((d, tf), lambda i, j: (0, nj + cur(j))),
            pl.BlockSpec((width, 2 * ff), lambda i, j: (0, 0)),
            pl.BlockSpec((1, 2 * ff), lambda i, j: (0, 0)),
            pl.BlockSpec((tf, d), lambda i, j: (prv(j), 0)),
            *st_in,
            pl.BlockSpec((1, d), lambda i, j: (0, 0)),
        ],
        out_specs=[pl.BlockSpec((tm, d), lambda i, j: (i, 0)), *st_out],
        out_shape=[jax.ShapeDtypeStruct((m, d), F32), *st_shape],
        scratch_shapes=scratch,
        compiler_params=_cparams(("arbitrary", "arbitrary")),
        name="convffn",
    )
    y, *ns = call(x, norm_g.reshape(1, d), w_up, w_up, conv_w, conv_b.reshape(1, 2 * ff), w_down,
                  state, state, final_g.reshape(1, d))
    new = ns[0] if wide_state else jnp.concatenate(ns, axis=-1)
    return y, new[tps - 1::tps]


def _single_history(kern):
    def wrapped(*refs):
        ins, (y_ref, ns_ref), scratch = refs[:10], refs[10:12], refs[12:]
        return kern(*ins, y_ref, ns_ref, ns_ref, *scratch)
    return wrapped


def _layer(x, st, p, final_g, final_norm, decode):
    conv_buf, s0, c0, n0, m0, ffn_buf = st
    B, L, D = x.shape
    Hd, Hm = s0.shape[1], c0.shape[1]
    dn_qkv = conv_buf.shape[-1]
    dn_vw = Hd * s0.shape[3]
    ff = p["w_down"].shape[0]

    T = next((c for c in (MIX_CHUNK, CHUNK) if L % c == 0), L)
    N = L // T
    R = _pick(B, (8, 4, 2, 1)) if decode else 1
    if R * T > 128:
        R = 1

    x2 = x.reshape(B * L, D)
    big, small = _inproj(x2, p["norm_mix_g"], p["w_big"], p["w_small"])
    big3 = big.reshape(B * N, T, big.shape[1])
    small3 = small.reshape(B * N, T, LANES)

    o_dn, s_new = _gdn(big3, small3, conv_buf, s0, p["dn_conv_w"], p["par"], p["dn_norm_g"],
                       R=R, T=T, N=N)
    o_ml, c_new, n_new, m_new = _mlstm(big3, small3, c0, n0, m0.reshape(B, 1, Hm), p["par"],
                                       p["ml_norm_g"], R=R, T=T, N=N, col0=dn_qkv + dn_vw,
                                       off_i=2 * Hd, off_f=2 * Hd + Hm)
    width = p["dn_conv_w"].shape[0]
    big4 = big.reshape(B, L, big.shape[1])
    if L >= width - 1:
        conv_new = big4[:, L - (width - 1):, :dn_qkv]
    else:
        conv_new = jnp.concatenate([conv_buf, big4[:, :, :dn_qkv]], axis=1)[:, L:]

    x1 = _outproj(x2, o_dn, o_ml, p["w_out"])

    fw = p["ffn_conv_w"].shape[0]
    if decode:
        x1t = x1.reshape(B, L, D).transpose(1, 0, 2).reshape(L * B, D)
        state = ffn_buf.transpose(1, 0, 2).reshape(1, (fw - 1) * B, 2 * ff)
        y, new = _ffn(x1t, p["norm_ffn_g"], p["w_up"], p["ffn_conv_w"], p["ffn_conv_b"],
                      p["w_down"], state, final_g, tm=L * B, shift=B, tps=1, final_norm=final_norm)
        y = y.reshape(L, B, D).transpose(1, 0, 2)
        ffn_new = new.reshape(fw - 1, B, 2 * ff).transpose(1, 0, 2)
    else:
        tm = _pick(L, (512, 256, 128, 64, 32, 16, 8))
        y, ffn_new = _ffn(x1, p["norm_ffn_g"], p["w_up"], p["ffn_conv_w"], p["ffn_conv_b"],
                          p["w_down"], ffn_buf, final_g, tm=tm, shift=1, tps=L // tm,
                          final_norm=final_norm)
        y = y.reshape(B, L, D)
    return y, (conv_new, s_new, c_new, n_new, m_new.reshape(B, Hm), ffn_new)


def _regroup_kernel(w_ref, big_ref, small_ref, *, c0, c1, c2):
    n = w_ref.shape[0]
    big_ref[:c0, :] = w_ref[:c0, :].astype(BF16)
    big_ref[c0:, :] = w_ref[c1:c2, :].astype(BF16)
    narrow = jnp.concatenate(
        [w_ref[c0:c1, :], w_ref[c2:, :], jnp.zeros((LANES - (c1 - c0) - (n - c2), w_ref.shape[1]), F32)],
        axis=0)
    small_ref[...] = narrow.astype(BF16)


def _regroup(w_t, c0, c1, c2):
    n, k = w_t.shape
    tk = _pick(k, (256, 128))
    nb = c0 + c2 - c1
    assert c0 % 16 == 0 and c1 % SUBLANES == 0 and c2 % SUBLANES == 0
    return pl.pallas_call(
        functools.partial(_regroup_kernel, c0=c0, c1=c1, c2=c2),
        grid=(k // tk,),
        in_specs=[pl.BlockSpec((n, tk), lambda i: (0, i))],
        out_specs=[pl.BlockSpec((nb, tk), lambda i: (0, i)), pl.BlockSpec((LANES, tk), lambda i: (0, i))],
        out_shape=[jax.ShapeDtypeStruct((nb, k), BF16), jax.ShapeDtypeStruct((LANES, k), BF16)],
        compiler_params=_cparams(("parallel",)),
        name="regroup",
    )(w_t)


def _prep_params(l, Hd, Hm, dn_qkv, dn_vw, ml_qk, ml_vw, norm_mix_g, w_in, dn_conv_w, dn_A_log,
                 dn_dt_bias, dn_norm_g, ml_i_bias, ml_f_bias, ml_norm_g, w_out, norm_ffn_g, w_up,
                 ffn_conv_w, ffn_conv_b, w_down):
    w = w_in[l]
    c0 = dn_qkv + dn_vw
    c1 = c0 + 2 * Hd
    c2 = c1 + 2 * ml_qk + 2 * ml_vw
    nsmall = 2 * Hd + 2 * Hm
    assert nsmall <= LANES and w.shape[1] == c2 + 2 * Hm
    w_big, w_small = _regroup(w.T, c0, c1, c2)
    z = lambda n: jnp.zeros((n,), F32)
    bias = jnp.concatenate([z(Hd), dn_dt_bias[l], ml_i_bias[l], ml_f_bias[l], z(LANES - nsmall)])
    alog = jnp.concatenate([z(Hd), dn_A_log[l], z(LANES - 2 * Hd)])
    par = jnp.zeros((SUBLANES, LANES), F32).at[0].set(bias).at[1].set(alog)
    return dict(norm_mix_g=norm_mix_g[l], w_big=w_big, w_small=w_small, dn_conv_w=dn_conv_w[l],
                par=par, dn_norm_g=dn_norm_g[l], ml_norm_g=ml_norm_g[l], w_out=w_out[l].astype(BF16),
                norm_ffn_g=norm_ffn_g[l], w_up=w_up[l].astype(BF16), ffn_conv_w=ffn_conv_w[l],
                ffn_conv_b=ffn_conv_b[l], w_down=w_down[l].astype(BF16))


def kernel(x_prompt, x_sample, state_dn_conv, state_dn_S, state_ml_C, state_ml_n, state_ml_m,
           state_ffn_conv, norm_mix_g, w_in, dn_conv_w, dn_A_log, dn_dt_bias, dn_norm_g,
           ml_i_bias, ml_f_bias, ml_norm_g, w_out, norm_ffn_g, w_up, ffn_conv_w, ffn_conv_b,
           w_down, norm_final_g):
    states = (state_dn_conv, state_dn_S, state_ml_C, state_ml_n, state_ml_m, state_ffn_conv)
    depth = w_in.shape[0]
    batch = x_prompt.shape[0]
    Hd, dk, dv = state_dn_S.shape[2:]
    Hm, mdk, mdv = state_ml_C.shape[2:]
    xp, xs = x_prompt, x_sample
    p_new = [[] for _ in states]
    s_new = [[] for _ in states]
    for l in range(depth):
        p = _prep_params(l, Hd, Hm, state_dn_conv.shape[-1], Hd * dv, Hm * mdk, Hm * mdv,
                         norm_mix_g, w_in, dn_conv_w, dn_A_log, dn_dt_bias, dn_norm_g, ml_i_bias,
                         ml_f_bias, ml_norm_g, w_out, norm_ffn_g, w_up, ffn_conv_w, ffn_conv_b,
                         w_down)
        last = l == depth - 1
        st_p = tuple(jnp.zeros((batch,) + s.shape[2:], s.dtype) for s in states)
        st_s = tuple(s[l] for s in states)
        xp, np_st = _layer(xp, st_p, p, norm_final_g, last, decode=False)
        xs, ns_st = _layer(xs, st_s, p, norm_final_g, last, decode=True)
        for i in range(len(states)):
            p_new[i].append(np_st[i])
            s_new[i].append(ns_st[i])
    outs_p = tuple(jnp.stack(a, axis=0) for a in p_new)
    outs_s = tuple(jnp.stack(a, axis=0) for a in s_new)
    return (xp, xs) + outs_p + outs_s
```

```python
import functools

import jax
import jax.numpy as jnp
from jax import lax
from jax.experimental import pallas as pl
from jax.experimental.pallas import tpu as pltpu

EPS = 1e-6
CHUNK = 64
MIX_CHUNK = 128
F32 = jnp.float32
BF16 = jnp.bfloat16
LANES = 128
SUBLANES = 8
VMEM_LIMIT = 56 * 1024 * 1024
WIDE_STATE_BYTES = 2 * 1024 * 1024


def _cparams(sem, vmem=VMEM_LIMIT, flags=None):
    return pltpu.CompilerParams(dimension_semantics=sem, vmem_limit_bytes=vmem, flags=flags)


def _dot(a, b):
    return jnp.dot(a.astype(BF16), b.astype(BF16), preferred_element_type=F32)


def _dot_nt(a, b):
    return lax.dot_general(a.astype(BF16), b.astype(BF16), (((1,), (1,)), ((), ())),
                           preferred_element_type=F32)


def _dot_tn(a, b):
    return lax.dot_general(a.astype(BF16), b.astype(BF16), (((0,), (0,)), ((), ())),
                           preferred_element_type=F32)


def _mask_dot_exact(mask_bf16, x):
    hi = x.astype(BF16)
    r1 = x - hi.astype(F32)
    mid = r1.astype(BF16)
    lo = (r1 - mid.astype(F32)).astype(BF16)
    d = functools.partial(jnp.dot, preferred_element_type=F32)
    return (d(mask_bf16, hi) + d(mask_bf16, mid)) + d(mask_bf16, lo)


def _softplus(x):
    return jnp.maximum(x, 0.0) + jnp.log1p(jnp.exp(-jnp.abs(x)))


def _sigmoid(x):
    return 1.0 / (1.0 + jnp.exp(-x))


def _rms(x, g):
    return x * lax.rsqrt(jnp.mean(x * x, axis=-1, keepdims=True) + EPS) * g


def _log2(n):
    assert n & (n - 1) == 0 and n > 0, n
    return n.bit_length() - 1


def _pick(n, cands):
    for c in cands:
        if n % c == 0:
            return c
    return n


def _inproj_kernel(x_ref, g_ref, wb_ref, ws_ref, ob_ref, os_ref, h_ref):
    nt = functools.partial(lax.dot_general, dimension_numbers=(((1,), (1,)), ((), ())),
                           preferred_element_type=F32)

    @pl.when(pl.program_id(1) == 0)
    def _():
        h = _rms(x_ref[...], g_ref[...]).astype(BF16)
        h_ref[...] = h
        os_ref[...] = nt(h, ws_ref[...])

    ob_ref[...] = nt(h_ref[...], wb_ref[...])


def _inproj(x, g, w_big_t, w_small_t):
    m, d = x.shape
    nb = w_big_t.shape[0]
    tm = _pick(m, (1024, 512, 256, 128, 64, 32, 16, 8))
    tn = _pick(nb, (1024, 512, 256, 128))
    return pl.pallas_call(
        _inproj_kernel,
        grid=(m // tm, nb // tn),
        in_specs=[
            pl.BlockSpec((tm, d), lambda i, j: (i, 0)),
            pl.BlockSpec((1, d), lambda i, j: (0, 0)),
            pl.BlockSpec((tn, d), lambda i, j: (j, 0)),
            pl.BlockSpec((LANES, d), lambda i, j: (0, 0)),
        ],
        out_specs=[
            pl.BlockSpec((tm, tn), lambda i, j: (i, j)),
            pl.BlockSpec((tm, LANES), lambda i, j: (i, 0)),
        ],
        out_shape=[jax.ShapeDtypeStruct((m, nb), F32), jax.ShapeDtypeStruct((m, LANES), F32)],
        scratch_shapes=[pltpu.VMEM((tm, d), BF16)],
        compiler_params=_cparams(("parallel", "arbitrary")),
        name="inproj",
    )(x, g.reshape(1, d), w_big_t, w_small_t)


def _seq_masks(R, T):
    RT = R * T
    sh = _log2(T)
    ri = lax.broadcasted_iota(jnp.int32, (RT, RT), 0)
    ci = lax.broadcasted_iota(jnp.int32, (RT, RT), 1)
    same = (ri >> sh) == (ci >> sh)
    tril = same & (ci <= ri)
    strict = same & (ci < ri)
    eye = ri == ci
    lastsel = same & ((ci & (T - 1)) == T - 1)
    rowseq = lax.broadcasted_iota(jnp.int32, (RT, 1), 0) >> sh
    return ri, ci, tril, strict, eye, lastsel, rowseq


def _row_form(col, eye):
    return jnp.sum(jnp.where(eye, col, 0.0), axis=0, keepdims=True)


def _at_last(row, lastsel):
    return jnp.sum(jnp.where(lastsel, row, 0.0), axis=1, keepdims=True)


def _unit_lower_inverses(lms, ri, ci, T):
    ds = None
    s = 1
    while s < T:
        sh = _log2(2 * s)
        blk = ((ri >> sh) == (ci >> sh)) & ((ri & (2 * s - 1)) >= s) & ((ci & (2 * s - 1)) < s)
        ms = [jnp.where(blk, lm, 0.0) for lm in lms]
        if ds is None:
            eye = jnp.where(ri == ci, 1.0, 0.0)
            ds = [eye - m for m in ms]
        else:
            dms = [_dot(d, m) for d, m in zip(ds, ms)]
            yield
            ds = [d - _dot(dm, d) for d, dm in zip(ds, dms)]
        yield
        s *= 2
    return ds


def _causal_conv(ext_ref, x_ref, cw_ref, T, width):
    lo = SUBLANES - (width - 1)
    ext_ref[:, SUBLANES:SUBLANES + T, :] = x_ref[...]
    cw = cw_ref[...]
    acc = None
    for j in range(width):
        term = ext_ref[:, lo + j:lo + j + T, :] * cw[j:j + 1, :]
        acc = term if acc is None else acc + term
    ext_ref[:, lo:SUBLANES, :] = ext_ref[:, lo + T:SUBLANES + T, :]
    return acc


def _gdn_steps(qkv_ref, z_ref, sm_ref, cw_ref, par_ref, ng_ref, o_ref, s_ref, ext_ref, masks,
               *, R, T, H, DK, DV, width):
    RT = R * T
    QK = H * DK

    y = _causal_conv(ext_ref, qkv_ref, cw_ref, T, width)
    y = y.reshape(RT, y.shape[-1])
    y = y * _sigmoid(y)
    z = z_ref[...].reshape(RT, H * DV)
    yield

    sm = sm_ref[...].reshape(RT, LANES)
    beta_all = _sigmoid(sm)
    g_all = -jnp.exp(par_ref[1:2, :]) * _softplus(sm + par_ref[0:1, :])

    ri, ci, tril, strict, eye, lastsel, rowseq = masks
    g_cum = _mask_dot_exact(jnp.where(tril, 1.0, 0.0).astype(BF16), g_all)
    yield

    hs = range(H)
    q = [y[:, h * DK:(h + 1) * DK] for h in hs]
    k = [y[:, QK + h * DK:QK + (h + 1) * DK] for h in hs]
    v = [y[:, 2 * QK + h * DV:2 * QK + (h + 1) * DV] for h in hs]
    q = [a * lax.rsqrt(jnp.sum(a * a, axis=-1, keepdims=True) + EPS) * (DK ** -0.5) for a in q]
    yield
    k = [a * lax.rsqrt(jnp.sum(a * a, axis=-1, keepdims=True) + EPS) for a in k]
    yield
    beta = [beta_all[:, h:h + 1] for h in hs]
    gc = [g_cum[:, H + h:H + h + 1] for h in hs]
    gr = [_row_form(a, eye) for a in gc]
    glast = [_at_last(a, lastsel) for a in gr]
    yield
    decay = [jnp.exp(jnp.where(tril, c - r, -jnp.inf)) for c, r in zip(gc, gr)]
    kb = [a * b for a, b in zip(k, beta)]
    yield
    lm = [jnp.where(strict, _dot_nt(a, b) * d, 0.0) for a, b, d in zip(kb, k, decay)]
    yield
    tinv = yield from _unit_lower_inverses(lm, ri, ci, T)
    eg = [jnp.exp(a) for a in gc]
    uw = [_dot(t, jnp.concatenate([a * b, c * e], axis=1))
          for t, a, b, c, e in zip(tinv, v, beta, kb, eg)]
    yield
    u = [a[:, :DV] for a in uw]
    w = [a[:, DV:] for a in uw]
    qk = [_dot_nt(a, b) * d for a, b, d in zip(q, k, decay)]
    yield
    qe = [a * e for a, e in zip(q, eg)]
    kd = [a * jnp.exp(l - c) for a, l, c in zip(k, glast, gc)]
    gl = [jnp.exp(a) for a in glast]
    yield

    if R == 1:
        s_old = [s_ref[0, h] for h in hs]
        ws = [_dot(jnp.concatenate([a, b], axis=0), c) for a, b, c in zip(w, qe, s_old)]
        yield
        v_new = [a - b[:RT] for a, b in zip(u, ws)]
        o = [a[RT:] + _dot(b, c) for a, b, c in zip(ws, qk, v_new)]
        yield
        for h in hs:
            s_ref[0, h] = s_old[h] * gl[h][0:1, :] + _dot_tn(kd[h], v_new[h])
        yield
    else:
        ws = [[_dot(jnp.concatenate([w[h][s * T:(s + 1) * T], qe[h][s * T:(s + 1) * T]], axis=0),
                    s_ref[s, h]) for s in range(R)] for h in hs]
        yield
        v_new = [u[h] - jnp.concatenate([a[:T] for a in ws[h]], axis=0) for h in hs]
        o = [jnp.concatenate([a[T:] for a in ws[h]], axis=0) + _dot(qk[h], v_new[h]) for h in hs]
        yield
        for h in hs:
            for s in range(R):
                kd_s = jnp.where(rowseq == s, kd[h], 0.0)
                s_ref[s, h] = s_ref[s, h] * gl[h][s * T:s * T + 1, :] + _dot_tn(kd_s, v_new[h])
            yield

    for h in hs:
        zh = z[:, h * DV:(h + 1) * DV]
        out = _rms(o[h], ng_ref[...]) * (zh * _sigmoid(zh))
        o_ref[:, h * DV:(h + 1) * DV] = out.astype(o_ref.dtype)
    yield


def _mlstm_steps(q_ref, k_ref, v_ref, og_ref, sm_ref, par_ref, ng_ref, o_ref, c_ref, n_ref, m_ref,
                 masks, *, R, T, H, DK, DV, off_i, off_f):
    RT = R * T
    ri, ci, tril, strict, eye, lastsel, rowseq = masks

    pre = sm_ref[...].reshape(RT, LANES) + par_ref[0:1, :]
    logf_all = -_softplus(-pre)
    b_all = _mask_dot_exact(jnp.where(tril, 1.0, 0.0).astype(BF16), logf_all)
    yield

    qa = q_ref[...].reshape(RT, H * DK)
    ka = k_ref[...].reshape(RT, H * DK)
    va = v_ref[...].reshape(RT, H * DV)
    oa = og_ref[...].reshape(RT, H * DV)

    hs = range(H)
    q = [qa[:, h * DK:(h + 1) * DK] * (DK ** -0.5) for h in hs]
    k = [ka[:, h * DK:(h + 1) * DK] for h in hs]
    v = [va[:, h * DV:(h + 1) * DV] for h in hs]
    bc = [b_all[:, off_f + h:off_f + h + 1] for h in hs]
    ic = [pre[:, off_i + h:off_i + h + 1] for h in hs]
    dm = [jnp.where(tril, b + _row_form(i - b, eye), -jnp.inf) for b, i in zip(bc, ic)]
    yield

    if R == 1:
        m_rows = [m_ref[0, :, h:h + 1] for h in hs]
    else:
        m_rows = []
        for h in hs:
            mr = jnp.zeros((RT, 1), F32)
            for s in range(R):
                mr = jnp.where(rowseq == s, m_ref[s, :, h:h + 1], mr)
            m_rows.append(mr)
    m_new = [jnp.maximum(b + m, jnp.max(d, axis=1, keepdims=True)) for b, m, d in zip(bc, m_rows, dm)]
    inter = [jnp.exp(b + m - mn) for b, m, mn in zip(bc, m_rows, m_new)]
    yield
    smat = [_dot_nt(a, b) * jnp.exp(d - mn) for a, b, d, mn in zip(q, k, dm, m_new)]
    yield

    if R == 1:
        qc = [_dot(q[h], c_ref[0, h]) for h in hs]
        qn = [jnp.sum(q[h] * n_ref[0, h:h + 1, :], axis=1, keepdims=True) for h in hs]
    else:
        qc, qn = [], []
        for h in hs:
            rows = [slice(s * T, (s + 1) * T) for s in range(R)]
            qc.append(jnp.concatenate([_dot(q[h][r], c_ref[s, h]) for s, r in enumerate(rows)], axis=0))
            qn.append(jnp.concatenate(
                [jnp.sum(q[h][r] * n_ref[s, h:h + 1, :], axis=1, keepdims=True)
                 for s, r in enumerate(rows)], axis=0))
            yield
    yield
    num = [i * c + _dot(s, a) for i, c, s, a in zip(inter, qc, smat, v)]
    den = [i * n + jnp.sum(s, axis=1, keepdims=True) for i, n, s in zip(inter, qn, smat)]
    yield
    hh = [a / jnp.maximum(jnp.abs(b), jnp.exp(-mn)) for a, b, mn in zip(num, den, m_new)]

    b_last = [_at_last(_row_form(b, eye), lastsel) for b in bc]
    m_end = [_at_last(_row_form(mn, eye), lastsel) for mn in m_new]
    yield
    kw = [a * jnp.exp(bl - b + i - me) for a, bl, b, i, me in zip(k, b_last, bc, ic, m_end)]
    cs = [jnp.exp(bl + m - me) for bl, m, me in zip(b_last, m_rows, m_end)]
    yield
    for h in hs:
        for s in range(R):
            kw_s = kw[h] if R == 1 else jnp.where(rowseq == s, kw[h], 0.0)
            cs_s = cs[h][s * T:s * T + 1, :]
            c_ref[s, h] = cs_s * c_ref[s, h] + _dot_tn(kw_s, v[h])
            n_ref[s, h:h + 1, :] = cs_s * n_ref[s, h:h + 1, :] + jnp.sum(kw_s, axis=0, keepdims=True)
            m_ref[s, :, h:h + 1] = m_end[h][s * T:s * T + 1, :]
        yield

    for h in hs:
        og = oa[:, h * DV:(h + 1) * DV]
        out = _rms(hh[h], ng_ref[:, h * DV:(h + 1) * DV]) * _sigmoid(og)
        o_ref[:, h * DV:(h + 1) * DV] = out.astype(o_ref.dtype)
    yield


def _mixers_kernel(qkv_ref, z_ref, mq_ref, mk_ref, mv_ref, mo_ref, sm_ref, cst_ref, s0_ref, c0_ref,
                   n0_ref, m0_ref, cw_ref, par_ref, dng_ref, mng_ref,
                   odn_ref, s_ref, oml_ref, c_ref, n_ref, m_ref, ext_ref,
                   *, R, T, Hd, Hm, DK, DV, MDK, MDV, width, off_i, off_f):
    @pl.when(pl.program_id(1) == 0)
    def _():
        ext_ref[:, SUBLANES - (width - 1):SUBLANES, :] = cst_ref[...]
        s_ref[...] = s0_ref[...]
        c_ref[...] = c0_ref[...]
        n_ref[...] = n0_ref[...]
        m_ref[...] = m0_ref[...]

    masks = _seq_masks(R, T)
    streams = [
        _gdn_steps(qkv_ref, z_ref, sm_ref, cw_ref, par_ref, dng_ref, odn_ref, s_ref, ext_ref, masks,
                   R=R, T=T, H=Hd, DK=DK, DV=DV, width=width),
        _mlstm_steps(mq_ref, mk_ref, mv_ref, mo_ref, sm_ref, par_ref, mng_ref, oml_ref, c_ref, n_ref,
                     m_ref, masks, R=R, T=T, H=Hm, DK=MDK, DV=MDV, off_i=off_i, off_f=off_f),
    ]
    turns = [2, 1]
    while streams:
        for gen, n in list(zip(streams, turns)):
            for _ in range(n):
                if gen in streams and next(gen, "done") == "done":
                    turns.pop(streams.index(gen))
                    streams.remove(gen)


def _mixers(proj3, small3, conv_state, s0, c0, n0, m0, conv_w, par, dn_norm_g, ml_norm_g,
            *, R, T, N, off_i, off_f):
    B, Hd, DK, DV = s0.shape
    _, Hm, MDK, MDV = c0.shape
    QKV = conv_state.shape[-1]
    VW, MQK, MVW = Hd * DV, Hm * MDK, Hm * MDV
    width = conv_w.shape[0]
    col0 = QKV + VW
    assert QKV == 2 * Hd * DK + VW and QKV % VW == 0 and VW % LANES == 0
    assert col0 % MQK == 0 and (col0 + 2 * MQK) % MVW == 0 and MQK % LANES == 0
    rows = proj3.shape[0] * T
    qb, vb = col0 // MQK, (col0 + 2 * MQK) // MVW
    kern = functools.partial(_mixers_kernel, R=R, T=T, Hd=Hd, Hm=Hm, DK=DK, DV=DV, MDK=MDK, MDV=MDV,
                             width=width, off_i=off_i, off_f=off_f)
    tok = lambda w, c: pl.BlockSpec((R, T, w), lambda b, n: (b * N + n, 0, c))
    const = lambda shape: pl.BlockSpec(shape, lambda b, n: (0,) * len(shape))
    per_seq = lambda *dims: pl.BlockSpec((R,) + dims, lambda b, n: (b,) + (0,) * len(dims))
    state_specs = [per_seq(Hd, DK, DV), per_seq(Hm, MDK, MDV), per_seq(Hm, MDK), per_seq(1, Hm)]
    odn, s_new, oml, c_new, n_new, m_new = pl.pallas_call(
        kern,
        grid=(B // R, N),
        in_specs=[
            tok(QKV, 0), tok(VW, QKV // VW), tok(MQK, qb), tok(MQK, qb + 1), tok(MVW, vb),
            tok(MVW, vb + 1), tok(LANES, 0),
            per_seq(width - 1, QKV), *state_specs,
            const((width, QKV)), const((SUBLANES, LANES)), const((1, DV)), const((1, MVW)),
        ],
        out_specs=[pl.BlockSpec((R * T, VW), lambda b, n: (b * N + n, 0)), state_specs[0],
                   pl.BlockSpec((R * T, MVW), lambda b, n: (b * N + n, 0)), *state_specs[1:]],
        out_shape=[jax.ShapeDtypeStruct((rows, VW), BF16), jax.ShapeDtypeStruct(s0.shape, F32),
                   jax.ShapeDtypeStruct((rows, MVW), BF16), jax.ShapeDtypeStruct(c0.shape, F32),
                   jax.ShapeDtypeStruct(n0.shape, F32), jax.ShapeDtypeStruct(m0.shape, F32)],
        scratch_shapes=[pltpu.VMEM((R, SUBLANES + T, QKV), F32)],
        compiler_params=_cparams(("parallel", "arbitrary")),
        name="mixers",
    )(proj3, proj3, proj3, proj3, proj3, proj3, small3, conv_state, s0, c0, n0, m0, conv_w, par,
      dn_norm_g.reshape(1, DV), ml_norm_g.reshape(1, MVW))
    return odn, s_new, oml, c_new, n_new, m_new


def _outproj_kernel(x_ref, a_ref, b_ref, wa_ref, wb_ref, o_ref):
    o_ref[...] = (x_ref[...] + jnp.dot(a_ref[...], wa_ref[...], preferred_element_type=F32)
                  + jnp.dot(b_ref[...], wb_ref[...], preferred_element_type=F32))


def _outproj(x, a, b, w_out):
    m, d = x.shape
    ka, kb = a.shape[1], b.shape[1]
    assert ka == kb
    tm = _pick(m, (512, 256, 128, 64, 32, 16, 8))
    return pl.pallas_call(
        _outproj_kernel,
        grid=(m // tm,),
        in_specs=[
            pl.BlockSpec((tm, d), lambda i: (i, 0)),
            pl.BlockSpec((tm, ka), lambda i: (i, 0)),
            pl.BlockSpec((tm, kb), lambda i: (i, 0)),
            pl.BlockSpec((ka, d), lambda i: (0, 0)),
            pl.BlockSpec((kb, d), lambda i: (1, 0)),
        ],
        out_specs=pl.BlockSpec((tm, d), lambda i: (i, 0)),
        out_shape=jax.ShapeDtypeStruct((m, d), F32),
        compiler_params=_cparams(("parallel",)),
        name="outproj",
    )(x, a, b, w_out, w_out)


def _ffn_kernel(x_ref, g_ref, wg_ref, wu_ref, cw_ref, cb_ref, wd_ref, stg_ref, stu_ref, gf_ref,
                y_ref, nsg_ref, nsu_ref, h_ref, act_ref, ext_ref, *carry,
                tm, tf, ff, shift, pad, tps, width, nj, wide_state, final_norm):
    i = pl.program_id(0)
    j = pl.program_id(1)
    hist = (width - 1) * shift

    def activate_tile():
        conv = []
        for idx, (w_ref, st_ref, ns_ref) in enumerate(
                ((wg_ref, stg_ref, nsg_ref), (wu_ref, stu_ref, nsu_ref))):
            col = pl.ds(pl.multiple_of(idx * ff + j * tf, LANES), tf)
            st_col = col if wide_state else slice(None)
            u = jnp.dot(h_ref[...], w_ref[...], preferred_element_type=F32)
            ext_ref[idx, pad:pad + tm, :] = u
            if tps == 1:
                ext_ref[idx, pad - hist:pad, :] = st_ref[0, :, st_col]
            else:
                ext_ref[idx, pad - hist:pad, :] = jnp.where(i % tps == 0, st_ref[0, :, st_col],
                                                            carry[0][j, idx])
            cw = cw_ref[:, col]
            c = u * cw[width - 1:width, :] + cb_ref[:, col]
            for t in range(width - 1):
                off = pad - hist + t * shift
                c = c + ext_ref[idx, off:off + tm, :] * cw[t:t + 1, :]
            new = u[tm - hist:, :]
            ns_ref[0, :, st_col] = new
            if tps != 1:
                carry[0][j, idx] = new
            conv.append(c)
        act_ref[...] = (conv[0] * _sigmoid(conv[0]) * conv[1]).astype(BF16)

    def down_tile():
        return jnp.dot(act_ref[...], wd_ref[...], preferred_element_type=F32)

    @pl.when(j == 0)
    def _():
        h_ref[...] = _rms(x_ref[...], g_ref[...]).astype(BF16)
        y_ref[...] = jnp.zeros_like(y_ref)
        activate_tile()

    @pl.when((j > 0) & (j < nj))
    def _():
        y_ref[...] += down_tile()
        activate_tile()

    @pl.when(j == nj)
    def _():
        xo = x_ref[...] + y_ref[...] + down_tile()
        y_ref[...] = _rms(xo, gf_ref[...]) if final_norm else xo


def _ffn(x, norm_g, w_up, conv_w, conv_b, w_down, state, final_g, *, tm, shift, tps, final_norm):
    m, d = x.shape
    ff = w_down.shape[0]
    width = conv_w.shape[0]
    hist = (width - 1) * shift
    pad = -(-hist // SUBLANES) * SUBLANES
    tf = _pick(ff, (512, 256, 128))
    nj = ff // tf
    n_tiles = m // tm
    wide_state = hist * 2 * ff * 4 <= WIDE_STATE_BYTES
    kern = functools.partial(_ffn_kernel, tm=tm, tf=tf, ff=ff, shift=shift, pad=pad, tps=tps,
                             width=width, nj=nj, wide_state=wide_state, final_norm=final_norm)
    scratch = [pltpu.VMEM((tm, d), BF16), pltpu.VMEM((tm, tf), BF16), pltpu.VMEM((2, pad + tm, tf), F32)]
    cur = lambda j: jnp.minimum(j, nj - 1)
    prv = lambda j: jnp.maximum(j - 1, 0)
    if tps != 1:
        scratch.append(pltpu.VMEM((nj, 2, hist, tf), F32))
    if wide_state:
        st_in = [pl.BlockSpec((1, hist, 2 * ff), lambda i, j: (i // tps, 0, 0))] * 2
        st_out = [pl.BlockSpec((1, hist, 2 * ff), lambda i, j: (i, 0, 0))]
        st_shape = [jax.ShapeDtypeStruct((n_tiles, hist, 2 * ff), F32)]
    else:
        st_in = [pl.BlockSpec((1, hist, tf), lambda i, j: (i // tps, 0, cur(j))),
                 pl.BlockSpec((1, hist, tf), lambda i, j: (i // tps, 0, nj + cur(j)))]
        st_out = [pl.BlockSpec((1, hist, tf), lambda i, j: (i, 0, cur(j)))] * 2
        st_shape = [jax.ShapeDtypeStruct((n_tiles, hist, ff), F32)] * 2
    call = pl.pallas_call(
        kern if not wide_state else _single_history(kern),
        grid=(n_tiles, nj + 1),
        in_specs=[
            pl.BlockSpec((tm, d), lambda i, j: (i, 0)),
            pl.BlockSpec((1, d), lambda i, j: (0, 0)),
            pl.BlockSpec((d, tf), lambda i, j: (0, cur(j))),
            pl.BlockSpec((d, tf), lambda i, j: (0, nj + cur(j))),
            pl.BlockSpec((width, 2 * ff), lambda i, j: (0, 0)),
            pl.BlockSpec((1, 2 * ff), lambda i, j: (0, 0)),
            pl.BlockSpec((tf, d), lambda i, j: (prv(j), 0)),
            *st_in,
            pl.BlockSpec((1, d), lambda i, j: (0, 0)),
        ],
        out_specs=[pl.BlockSpec((tm, d), lambda i, j: (i, 0)), *st_out],
        out_shape=[jax.ShapeDtypeStruct((m, d), F32), *st_shape],
        scratch_shapes=scratch,
        compiler_params=_cparams(("arbitrary", "arbitrary")),
        name="convffn",
    )
    y, *ns = call(x, norm_g.reshape(1, d), w_up, w_up, conv_w, conv_b.reshape(1, 2 * ff), w_down,
                  state, state, final_g.reshape(1, d))
    new = ns[0] if wide_state else jnp.concatenate(ns, axis=-1)
    return y, new[tps - 1::tps]


def _single_history(kern):
    def wrapped(*refs):
        ins, (y_ref, ns_ref), scratch = refs[:10], refs[10:12], refs[12:]
        return kern(*ins, y_ref, ns_ref, ns_ref, *scratch)
    return wrapped


def _layer(x, st, p, final_g, final_norm, decode):
    conv_buf, s0, c0, n0, m0, ffn_buf = st
    B, L, D = x.shape
    Hd, Hm = s0.shape[1], c0.shape[1]
    dn_qkv = conv_buf.shape[-1]
    dn_vw = Hd * s0.shape[3]
    ff = p["w_down"].shape[0]

    T = next((c for c in (MIX_CHUNK, CHUNK) if L % c == 0), L)
    N = L // T
    R = _pick(B, (8, 4, 2, 1)) if decode else 1
    if R * T > 128:
        R = 1

    x2 = x.reshape(B * L, D)
    big, small = _inproj(x2, p["norm_mix_g"], p["w_big"], p["w_small"])
    big3 = big.reshape(B * N, T, big.shape[1])
    small3 = small.reshape(B * N, T, LANES)

    o_dn, s_new, o_ml, c_new, n_new, m_new = _mixers(
        big3, small3, conv_buf, s0, c0, n0, m0.reshape(B, 1, Hm), p["dn_conv_w"], p["par"],
        p["dn_norm_g"], p["ml_norm_g"], R=R, T=T, N=N, off_i=2 * Hd, off_f=2 * Hd + Hm)
    width = p["dn_conv_w"].shape[0]
    big4 = big.reshape(B, L, big.shape[1])
    if L >= width - 1:
        conv_new = big4[:, L - (width - 1):, :dn_qkv]
    else:
        conv_new = jnp.concatenate([conv_buf, big4[:, :, :dn_qkv]], axis=1)[:, L:]

    x1 = _outproj(x2, o_dn, o_ml, p["w_out"])

    fw = p["ffn_conv_w"].shape[0]
    if decode:
        x1t = x1.reshape(B, L, D).transpose(1, 0, 2).reshape(L * B, D)
        state = ffn_buf.transpose(1, 0, 2).reshape(1, (fw - 1) * B, 2 * ff)
        y, new = _ffn(x1t, p["norm_ffn_g"], p["w_up"], p["ffn_conv_w"], p["ffn_conv_b"],
                      p["w_down"], state, final_g, tm=L * B, shift=B, tps=1, final_norm=final_norm)
        y = y.reshape(L, B, D).transpose(1, 0, 2)
        ffn_new = new.reshape(fw - 1, B, 2 * ff).transpose(1, 0, 2)
    else:
        tm = _pick(L, (512, 256, 128, 64, 32, 16, 8))
        y, ffn_new = _ffn(x1, p["norm_ffn_g"], p["w_up"], p["ffn_conv_w"], p["ffn_conv_b"],
                          p["w_down"], ffn_buf, final_g, tm=tm, shift=1, tps=L // tm,
                          final_norm=final_norm)
        y = y.reshape(B, L, D)
    return y, (conv_new, s_new, c_new, n_new, m_new.reshape(B, Hm), ffn_new)


def _regroup_kernel(w_ref, big_ref, small_ref, *, c0, c1, c2):
    n = w_ref.shape[0]
    big_ref[:c0, :] = w_ref[:c0, :].astype(BF16)
    big_ref[c0:, :] = w_ref[c1:c2, :].astype(BF16)
    narrow = jnp.concatenate(
        [w_ref[c0:c1, :], w_ref[c2:, :], jnp.zeros((LANES - (c1 - c0) - (n - c2), w_ref.shape[1]), F32)],
        axis=0)
    small_ref[...] = narrow.astype(BF16)


def _regroup(w_t, c0, c1, c2):
    n, k = w_t.shape
    tk = _pick(k, (256, 128))
    nb = c0 + c2 - c1
    assert c0 % 16 == 0 and c1 % SUBLANES == 0 and c2 % SUBLANES == 0
    return pl.pallas_call(
        functools.partial(_regroup_kernel, c0=c0, c1=c1, c2=c2),
        grid=(k // tk,),
        in_specs=[pl.BlockSpec((n, tk), lambda i: (0, i))],
        out_specs=[pl.BlockSpec((nb, tk), lambda i: (0, i)), pl.BlockSpec((LANES, tk), lambda i: (0, i))],
        out_shape=[jax.ShapeDtypeStruct((nb, k), BF16), jax.ShapeDtypeStruct((LANES, k), BF16)],
        compiler_params=_cparams(("parallel",)),
        name="regroup",
    )(w_t)


def _prep_params(l, Hd, Hm, dn_qkv, dn_vw, ml_qk, ml_vw, norm_mix_g, w_in, dn_conv_w, dn_A_log,
                 dn_dt_bias, dn_norm_g, ml_i_bias, ml_f_bias, ml_norm_g, w_out, norm_ffn_g, w_up,
                 ffn_conv_w, ffn_conv_b, w_down):
    w = w_in[l]
    c0 = dn_qkv + dn_vw
    c1 = c0 + 2 * Hd
    c2 = c1 + 2 * ml_qk + 2 * ml_vw
    nsmall = 2 * Hd + 2 * Hm
    assert nsmall <= LANES and w.shape[1] == c2 + 2 * Hm
    w_big, w_small = _regroup(w.T, c0, c1, c2)
    z = lambda n: jnp.zeros((n,), F32)
    bias = jnp.concatenate([z(Hd), dn_dt_bias[l], ml_i_bias[l], ml_f_bias[l], z(LANES - nsmall)])
    alog = jnp.concatenate([z(Hd), dn_A_log[l], z(LANES - 2 * Hd)])
    par = jnp.zeros((SUBLANES, LANES), F32).at[0].set(bias).at[1].set(alog)
    return dict(norm_mix_g=norm_mix_g[l], w_big=w_big, w_small=w_small, dn_conv_w=dn_conv_w[l],
                par=par, dn_norm_g=dn_norm_g[l], ml_norm_g=ml_norm_g[l], w_out=w_out[l].astype(BF16),
                norm_ffn_g=norm_ffn_g[l], w_up=w_up[l].astype(BF16), ffn_conv_w=ffn_conv_w[l],
                ffn_conv_b=ffn_conv_b[l], w_down=w_down[l].astype(BF16))


def kernel(x_prompt, x_sample, state_dn_conv, state_dn_S, state_ml_C, state_ml_n, state_ml_m,
           state_ffn_conv, norm_mix_g, w_in, dn_conv_w, dn_A_log, dn_dt_bias, dn_norm_g,
           ml_i_bias, ml_f_bias, ml_norm_g, w_out, norm_ffn_g, w_up, ffn_conv_w, ffn_conv_b,
           w_down, norm_final_g):
    states = (state_dn_conv, state_dn_S, state_ml_C, state_ml_n, state_ml_m, state_ffn_conv)
    depth = w_in.shape[0]
    batch = x_prompt.shape[0]
    Hd, dk, dv = state_dn_S.shape[2:]
    Hm, mdk, mdv = state_ml_C.shape[2:]
    xp, xs = x_prompt, x_sample
    p_new = [[] for _ in states]
    s_new = [[] for _ in states]
    for l in range(depth):
        p = _prep_params(l, Hd, Hm, state_dn_conv.shape[-1], Hd * dv, Hm * mdk, Hm * mdv,
                         norm_mix_g, w_in, dn_conv_w, dn_A_log, dn_dt_bias, dn_norm_g, ml_i_bias,
                         ml_f_bias, ml_norm_g, w_out, norm_ffn_g, w_up, ffn_conv_w, ffn_conv_b,
                         w_down)
        last = l == depth - 1
        st_p = tuple(jnp.zeros((batch,) + s.shape[2:], s.dtype) for s in states)
        st_s = tuple(s[l] for s in states)
        xp, np_st = _layer(xp, st_p, p, norm_final_g, last, decode=False)
        xs, ns_st = _layer(xs, st_s, p, norm_final_g, last, decode=True)
        for i in range(len(states)):
            p_new[i].append(np_st[i])
            s_new[i].append(ns_st[i])
    outs_p = tuple(jnp.stack(a, axis=0) for a in p_new)
    outs_s = tuple(jnp.stack(a, axis=0) for a in s_new)
    return (xp, xs) + outs_p + outs_s
```

```python
import functools

import jax
import jax.numpy as jnp
from jax import lax
from jax.experimental import pallas as pl
from jax.experimental.pallas import tpu as pltpu

EPS = 1e-6
CHUNK = 64
MIX_CHUNK = 128
F32 = jnp.float32
BF16 = jnp.bfloat16
LANES = 128
SUBLANES = 8
VMEM_LIMIT = 56 * 1024 * 1024
WIDE_STATE_BYTES = 2 * 1024 * 1024


def _cparams(sem, vmem=VMEM_LIMIT, flags=None):
    return pltpu.CompilerParams(dimension_semantics=sem, vmem_limit_bytes=vmem, flags=flags)


def _dot(a, b):
    return jnp.dot(a.astype(BF16), b.astype(BF16), preferred_element_type=F32)


def _dot_nt(a, b):
    return lax.dot_general(a.astype(BF16), b.astype(BF16), (((1,), (1,)), ((), ())),
                           preferred_element_type=F32)


def _dot_tn(a, b):
    return lax.dot_general(a.astype(BF16), b.astype(BF16), (((0,), (0,)), ((), ())),
                           preferred_element_type=F32)


def _mask_dot_exact(mask_bf16, x):
    hi = x.astype(BF16)
    r1 = x - hi.astype(F32)
    mid = r1.astype(BF16)
    lo = (r1 - mid.astype(F32)).astype(BF16)
    d = functools.partial(jnp.dot, preferred_element_type=F32)
    return (d(mask_bf16, hi) + d(mask_bf16, mid)) + d(mask_bf16, lo)


def _softplus(x):
    return jnp.maximum(x, 0.0) + jnp.log1p(jnp.exp(-jnp.abs(x)))


def _sigmoid(x):
    return 1.0 / (1.0 + jnp.exp(-x))


def _rms(x, g):
    return x * lax.rsqrt(jnp.mean(x * x, axis=-1, keepdims=True) + EPS) * g


def _log2(n):
    assert n & (n - 1) == 0 and n > 0, n
    return n.bit_length() - 1


def _pick(n, cands):
    for c in cands:
        if n % c == 0:
            return c
    return n


def _inproj_kernel(x_ref, g_ref, wb_ref, ws_ref, ob_ref, os_ref, h_ref):
    nt = functools.partial(lax.dot_general, dimension_numbers=(((1,), (1,)), ((), ())),
                           preferred_element_type=F32)

    @pl.when(pl.program_id(1) == 0)
    def _():
        h = _rms(x_ref[...], g_ref[...]).astype(BF16)
        h_ref[...] = h
        os_ref[...] = nt(h, ws_ref[...])

    ob_ref[...] = nt(h_ref[...], wb_ref[...])


def _inproj(x, g, w_big_t, w_small_t):
    m, d = x.shape
    nb = w_big_t.shape[0]
    tm = _pick(m, (1024, 512, 256, 128, 64, 32, 16, 8))
    tn = _pick(nb, (1024, 512, 256, 128))
    return pl.pallas_call(
        _inproj_kernel,
        grid=(m // tm, nb // tn),
        in_specs=[
            pl.BlockSpec((tm, d), lambda i, j: (i, 0)),
            pl.BlockSpec((1, d), lambda i, j: (0, 0)),
            pl.BlockSpec((tn, d), lambda i, j: (j, 0)),
            pl.BlockSpec((LANES, d), lambda i, j: (0, 0)),
        ],
        out_specs=[
            pl.BlockSpec((tm, tn), lambda i, j: (i, j)),
            pl.BlockSpec((tm, LANES), lambda i, j: (i, 0)),
        ],
        out_shape=[jax.ShapeDtypeStruct((m, nb), F32), jax.ShapeDtypeStruct((m, LANES), F32)],
        scratch_shapes=[pltpu.VMEM((tm, d), BF16)],
        compiler_params=_cparams(("parallel", "arbitrary")),
        name="inproj",
    )(x, g.reshape(1, d), w_big_t, w_small_t)


def _seq_masks(R, T):
    RT = R * T
    sh = _log2(T)
    ri = lax.broadcasted_iota(jnp.int32, (RT, RT), 0)
    ci = lax.broadcasted_iota(jnp.int32, (RT, RT), 1)
    same = (ri >> sh) == (ci >> sh)
    tril = same & (ci <= ri)
    strict = same & (ci < ri)
    eye = ri == ci
    lastsel = same & ((ci & (T - 1)) == T - 1)
    rowseq = lax.broadcasted_iota(jnp.int32, (RT, 1), 0) >> sh
    return ri, ci, tril, strict, eye, lastsel, rowseq


def _row_form(col, eye):
    return jnp.sum(jnp.where(eye, col, 0.0), axis=0, keepdims=True)


def _at_last(row, lastsel):
    return jnp.sum(jnp.where(lastsel, row, 0.0), axis=1, keepdims=True)


def _unit_lower_inverses(lms, ri, ci, T):
    ds = None
    s = 1
    while s < T:
        sh = _log2(2 * s)
        blk = ((ri >> sh) == (ci >> sh)) & ((ri & (2 * s - 1)) >= s) & ((ci & (2 * s - 1)) < s)
        ms = [jnp.where(blk, lm, 0.0) for lm in lms]
        if ds is None:
            eye = jnp.where(ri == ci, 1.0, 0.0)
            ds = [eye - m for m in ms]
        else:
            dms = [_dot(d, m) for d, m in zip(ds, ms)]
            yield
            ds = [d - _dot(dm, d) for d, dm in zip(ds, dms)]
        yield
        s *= 2
    return ds


def _causal_conv(ext_ref, x_ref, cw_ref, T, width):
    lo = SUBLANES - (width - 1)
    ext_ref[:, SUBLANES:SUBLANES + T, :] = x_ref[...]
    cw = cw_ref[...]
    acc = None
    for j in range(width):
        term = ext_ref[:, lo + j:lo + j + T, :] * cw[j:j + 1, :]
        acc = term if acc is None else acc + term
    ext_ref[:, lo:SUBLANES, :] = ext_ref[:, lo + T:SUBLANES + T, :]
    return acc


def _gdn_steps(qkv_ref, z_ref, sm_ref, cw_ref, par_ref, ng_ref, o_ref, s_ref, ext_ref, masks,
               *, R, T, H, DK, DV, width):
    RT = R * T
    QK = H * DK

    y = _causal_conv(ext_ref, qkv_ref, cw_ref, T, width)
    y = y.reshape(RT, y.shape[-1])
    y = y * _sigmoid(y)
    z = z_ref[...].reshape(RT, H * DV)
    yield

    sm = sm_ref[...].reshape(RT, LANES)
    beta_all = _sigmoid(sm)
    g_all = -jnp.exp(par_ref[1:2, :]) * _softplus(sm + par_ref[0:1, :])

    ri, ci, tril, strict, eye, lastsel, rowseq = masks
    g_cum = _mask_dot_exact(jnp.where(tril, 1.0, 0.0).astype(BF16), g_all)
    yield

    hs = range(H)
    q = [y[:, h * DK:(h + 1) * DK] for h in hs]
    k = [y[:, QK + h * DK:QK + (h + 1) * DK] for h in hs]
    v = [y[:, 2 * QK + h * DV:2 * QK + (h + 1) * DV] for h in hs]
    q = [a * lax.rsqrt(jnp.sum(a * a, axis=-1, keepdims=True) + EPS) * (DK ** -0.5) for a in q]
    yield
    k = [a * lax.rsqrt(jnp.sum(a * a, axis=-1, keepdims=True) + EPS) for a in k]
    yield
    beta = [beta_all[:, h:h + 1] for h in hs]
    gc = [g_cum[:, H + h:H + h + 1] for h in hs]
    gr = [_row_form(a, eye) for a in gc]
    glast = [_at_last(a, lastsel) for a in gr]
    yield
    decay = [jnp.exp(jnp.where(tril, c - r, -jnp.inf)) for c, r in zip(gc, gr)]
    kb = [a * b for a, b in zip(k, beta)]
    yield
    lm = [jnp.where(strict, _dot_nt(a, b) * d, 0.0) for a, b, d in zip(kb, k, decay)]
    yield
    tinv = yield from _unit_lower_inverses(lm, ri, ci, T)
    eg = [jnp.exp(a) for a in gc]
    uw = [_dot(t, jnp.concatenate([a * b, c * e], axis=1))
          for t, a, b, c, e in zip(tinv, v, beta, kb, eg)]
    yield
    u = [a[:, :DV] for a in uw]
    w = [a[:, DV:] for a in uw]
    qk = [_dot_nt(a, b) * d for a, b, d in zip(q, k, decay)]
    yield
    qe = [a * e for a, e in zip(q, eg)]
    kd = [a * jnp.exp(l - c) for a, l, c in zip(k, glast, gc)]
    gl = [jnp.exp(a) for a in glast]
    yield

    if R == 1:
        s_old = [s_ref[0, h] for h in hs]
        ws = [_dot(jnp.concatenate([a, b], axis=0), c) for a, b, c in zip(w, qe, s_old)]
        yield
        v_new = [a - b[:RT] for a, b in zip(u, ws)]
        o = [a[RT:] + _dot(b, c) for a, b, c in zip(ws, qk, v_new)]
        yield
        for h in hs:
            s_ref[0, h] = s_old[h] * gl[h][0:1, :] + _dot_tn(kd[h], v_new[h])
        yield
    else:
        ws = [[_dot(jnp.concatenate([w[h][s * T:(s + 1) * T], qe[h][s * T:(s + 1) * T]], axis=0),
                    s_ref[s, h]) for s in range(R)] for h in hs]
        yield
        v_new = [u[h] - jnp.concatenate([a[:T] for a in ws[h]], axis=0) for h in hs]
        o = [jnp.concatenate([a[T:] for a in ws[h]], axis=0) + _dot(qk[h], v_new[h]) for h in hs]
        yield
        for h in hs:
            for s in range(R):
                kd_s = jnp.where(rowseq == s, kd[h], 0.0)
                s_ref[s, h] = s_ref[s, h] * gl[h][s * T:s * T + 1, :] + _dot_tn(kd_s, v_new[h])
            yield

    for h in hs:
        zh = z[:, h * DV:(h + 1) * DV]
        out = _rms(o[h], ng_ref[...]) * (zh * _sigmoid(zh))
        o_ref[:, h * DV:(h + 1) * DV] = out.astype(o_ref.dtype)
    yield


def _mlstm_steps(q_ref, k_ref, v_ref, og_ref, sm_ref, par_ref, ng_ref, o_ref, c_ref, n_ref, m_ref,
                 masks, *, R, T, H, DK, DV, off_i, off_f):
    RT = R * T
    ri, ci, tril, strict, eye, lastsel, rowseq = masks

    pre = sm_ref[...].reshape(RT, LANES) + par_ref[0:1, :]
    logf_all = -_softplus(-pre)
    b_all = _mask_dot_exact(jnp.where(tril, 1.0, 0.0).astype(BF16), logf_all)
    yield

    qa = q_ref[...].reshape(RT, H * DK)
    ka = k_ref[...].reshape(RT, H * DK)
    va = v_ref[...].reshape(RT, H * DV)
    oa = og_ref[...].reshape(RT, H * DV)

    hs = range(H)
    q = [qa[:, h * DK:(h + 1) * DK] * (DK ** -0.5) for h in hs]
    k = [ka[:, h * DK:(h + 1) * DK] for h in hs]
    v = [va[:, h * DV:(h + 1) * DV] for h in hs]
    bc = [b_all[:, off_f + h:off_f + h + 1] for h in hs]
    ic = [pre[:, off_i + h:off_i + h + 1] for h in hs]
    dm = [jnp.where(tril, b + _row_form(i - b, eye), -jnp.inf) for b, i in zip(bc, ic)]
    yield

    if R == 1:
        m_rows = [m_ref[0, :, h:h + 1] for h in hs]
    else:
        m_rows = []
        for h in hs:
            mr = jnp.zeros((RT, 1), F32)
            for s in range(R):
                mr = jnp.where(rowseq == s, m_ref[s, :, h:h + 1], mr)
            m_rows.append(mr)
    m_new = [jnp.maximum(b + m, jnp.max(d, axis=1, keepdims=True)) for b, m, d in zip(bc, m_rows, dm)]
    inter = [jnp.exp(b + m - mn) for b, m, mn in zip(bc, m_rows, m_new)]
    yield
    smat = [_dot_nt(a, b) * jnp.exp(d - mn) for a, b, d, mn in zip(q, k, dm, m_new)]
    yield

    if R == 1:
        qc = [_dot(q[h], c_ref[0, h]) for h in hs]
        qn = [jnp.sum(q[h] * n_ref[0, h:h + 1, :], axis=1, keepdims=True) for h in hs]
    else:
        qc, qn = [], []
        for h in hs:
            rows = [slice(s * T, (s + 1) * T) for s in range(R)]
            qc.append(jnp.concatenate([_dot(q[h][r], c_ref[s, h]) for s, r in enumerate(rows)], axis=0))
            qn.append(jnp.concatenate(
                [jnp.sum(q[h][r] * n_ref[s, h:h + 1, :], axis=1, keepdims=True)
                 for s, r in enumerate(rows)], axis=0))
            yield
    yield
    num = [i * c + _dot(s, a) for i, c, s, a in zip(inter, qc, smat, v)]
    den = [i * n + jnp.sum(s, axis=1, keepdims=True) for i, n, s in zip(inter, qn, smat)]
    yield
    hh = [a / jnp.maximum(jnp.abs(b), jnp.exp(-mn)) for a, b, mn in zip(num, den, m_new)]

    b_last = [_at_last(_row_form(b, eye), lastsel) for b in bc]
    m_end = [_at_last(_row_form(mn, eye), lastsel) for mn in m_new]
    yield
    kw = [a * jnp.exp(bl - b + i - me) for a, bl, b, i, me in zip(k, b_last, bc, ic, m_end)]
    cs = [jnp.exp(bl + m - me) for bl, m, me in zip(b_last, m_rows, m_end)]
    yield
    for h in hs:
        for s in range(R):
            kw_s = kw[h] if R == 1 else jnp.where(rowseq == s, kw[h], 0.0)
            cs_s = cs[h][s * T:s * T + 1, :]
            c_ref[s, h] = cs_s * c_ref[s, h] + _dot_tn(kw_s, v[h])
            n_ref[s, h:h + 1, :] = cs_s * n_ref[s, h:h + 1, :] + jnp.sum(kw_s, axis=0, keepdims=True)
            m_ref[s, :, h:h + 1] = m_end[h][s * T:s * T + 1, :]
        yield

    for h in hs:
        og = oa[:, h * DV:(h + 1) * DV]
        out = _rms(hh[h], ng_ref[:, h * DV:(h + 1) * DV]) * _sigmoid(og)
        o_ref[:, h * DV:(h + 1) * DV] = out.astype(o_ref.dtype)
    yield


def _mixers_kernel(qkv_ref, z_ref, mq_ref, mk_ref, mv_ref, mo_ref, sm_ref, cst_ref, s0_ref, c0_ref,
                   n0_ref, m0_ref, cw_ref, par_ref, dng_ref, mng_ref,
                   odn_ref, s_ref, oml_ref, c_ref, n_ref, m_ref, ext_ref,
                   *, R, T, Hd, Hm, DK, DV, MDK, MDV, width, off_i, off_f):
    @pl.when(pl.program_id(1) == 0)
    def _():
        ext_ref[:, SUBLANES - (width - 1):SUBLANES, :] = cst_ref[...]
        s_ref[...] = s0_ref[...]
        c_ref[...] = c0_ref[...]
        n_ref[...] = n0_ref[...]
        m_ref[...] = m0_ref[...]

    masks = _seq_masks(R, T)
    streams = [
        _gdn_steps(qkv_ref, z_ref, sm_ref, cw_ref, par_ref, dng_ref, odn_ref, s_ref, ext_ref, masks,
                   R=R, T=T, H=Hd, DK=DK, DV=DV, width=width),
        _mlstm_steps(mq_ref, mk_ref, mv_ref, mo_ref, sm_ref, par_ref, mng_ref, oml_ref, c_ref, n_ref,
                     m_ref, masks, R=R, T=T, H=Hm, DK=MDK, DV=MDV, off_i=off_i, off_f=off_f),
    ]
    turns = [2, 1]
    while streams:
        for gen, n in list(zip(streams, turns)):
            for _ in range(n):
                if gen in streams and next(gen, "done") == "done":
                    turns.pop(streams.index(gen))
                    streams.remove(gen)


def _mixers(proj3, small3, conv_state, s0, c0, n0, m0, conv_w, par, dn_norm_g, ml_norm_g,
            *, R, T, N, off_i, off_f):
    B, Hd, DK, DV = s0.shape
    _, Hm, MDK, MDV = c0.shape
    QKV = conv_state.shape[-1]
    VW, MQK, MVW = Hd * DV, Hm * MDK, Hm * MDV
    width = conv_w.shape[0]
    col0 = QKV + VW
    assert QKV == 2 * Hd * DK + VW and QKV % VW == 0 and VW % LANES == 0
    assert col0 % MQK == 0 and (col0 + 2 * MQK) % MVW == 0 and MQK % LANES == 0
    rows = proj3.shape[0] * T
    qb, vb = col0 // MQK, (col0 + 2 * MQK) // MVW
    kern = functools.partial(_mixers_kernel, R=R, T=T, Hd=Hd, Hm=Hm, DK=DK, DV=DV, MDK=MDK, MDV=MDV,
                             width=width, off_i=off_i, off_f=off_f)
    tok = lambda w, c: pl.BlockSpec((R, T, w), lambda b, n: (b * N + n, 0, c))
    const = lambda shape: pl.BlockSpec(shape, lambda b, n: (0,) * len(shape))
    per_seq = lambda *dims: pl.BlockSpec((R,) + dims, lambda b, n: (b,) + (0,) * len(dims))
    state_specs = [per_seq(Hd, DK, DV), per_seq(Hm, MDK, MDV), per_seq(Hm, MDK), per_seq(1, Hm)]
    odn, s_new, oml, c_new, n_new, m_new = pl.pallas_call(
        kern,
        grid=(B // R, N),
        in_specs=[
            tok(QKV, 0), tok(VW, QKV // VW), tok(MQK, qb), tok(MQK, qb + 1), tok(MVW, vb),
            tok(MVW, vb + 1), tok(LANES, 0),
            per_seq(width - 1, QKV), *state_specs,
            const((width, QKV)), const((SUBLANES, LANES)), const((1, DV)), const((1, MVW)),
        ],
        out_specs=[pl.BlockSpec((R * T, VW), lambda b, n: (b * N + n, 0)), state_specs[0],
                   pl.BlockSpec((R * T, MVW), lambda b, n: (b * N + n, 0)), *state_specs[1:]],
        out_shape=[jax.ShapeDtypeStruct((rows, VW), BF16), jax.ShapeDtypeStruct(s0.shape, F32),
                   jax.ShapeDtypeStruct((rows, MVW), BF16), jax.ShapeDtypeStruct(c0.shape, F32),
                   jax.ShapeDtypeStruct(n0.shape, F32), jax.ShapeDtypeStruct(m0.shape, F32)],
        scratch_shapes=[pltpu.VMEM((R, SUBLANES + T, QKV), F32)],
        compiler_params=_cparams(("parallel", "arbitrary")),
        name="mixers",
    )(proj3, proj3, proj3, proj3, proj3, proj3, small3, conv_state, s0, c0, n0, m0, conv_w, par,
      dn_norm_g.reshape(1, DV), ml_norm_g.reshape(1, MVW))
    return odn, s_new, oml, c_new, n_new, m_new


def _outproj_kernel(x_ref, a_ref, b_ref, wa_ref, wb_ref, o_ref):
    o_ref[...] = (x_ref[...] + jnp.dot(a_ref[...], wa_ref[...], preferred_element_type=F32)
                  + jnp.dot(b_ref[...], wb_ref[...], preferred_element_type=F32))


def _outproj(x, a, b, w_out):
    m, d = x.shape
    ka, kb = a.shape[1], b.shape[1]
    assert ka == kb
    tm = _pick(m, (512, 256, 128, 64, 32, 16, 8))
    return pl.pallas_call(
        _outproj_kernel,
        grid=(m // tm,),
        in_specs=[
            pl.BlockSpec((tm, d), lambda i: (i, 0)),
            pl.BlockSpec((tm, ka), lambda i: (i, 0)),
            pl.BlockSpec((tm, kb), lambda i: (i, 0)),
            pl.BlockSpec((ka, d), lambda i: (0, 0)),
            pl.BlockSpec((kb, d), lambda i: (1, 0)),
        ],
        out_specs=pl.BlockSpec((tm, d), lambda i: (i, 0)),
        out_shape=jax.ShapeDtypeStruct((m, d), F32),
        compiler_params=_cparams(("parallel",)),
        name="outproj",
    )(x, a, b, w_out, w_out)


def _ffn_kernel(x_ref, g_ref, wg_ref, wu_ref, cw_ref, cb_ref, wd_ref, stg_ref, stu_ref, gf_ref,
                y_ref, nsg_ref, nsu_ref, h_ref, act_ref, ext_ref, *carry,
                tm, tf, ff, shift, pad, tps, width, nj, wide_state, final_norm):
    i = pl.program_id(0)
    j = pl.program_id(1)
    hist = (width - 1) * shift

    def activate_tile():
        conv = []
        for idx, (w_ref, st_ref, ns_ref) in enumerate(
                ((wg_ref, stg_ref, nsg_ref), (wu_ref, stu_ref, nsu_ref))):
            col = pl.ds(pl.multiple_of(idx * ff + j * tf, LANES), tf)
            st_col = col if wide_state else slice(None)
            u = jnp.dot(h_ref[...], w_ref[...], preferred_element_type=F32)
            ext_ref[idx, pad:pad + tm, :] = u
            if tps == 1:
                ext_ref[idx, pad - hist:pad, :] = st_ref[0, :, st_col]
            else:
                ext_ref[idx, pad - hist:pad, :] = jnp.where(i % tps == 0, st_ref[0, :, st_col],
                                                            carry[0][j, idx])
            cw = cw_ref[:, col]
            c = u * cw[width - 1:width, :] + cb_ref[:, col]
            for t in range(width - 1):
                off = pad - hist + t * shift
                c = c + ext_ref[idx, off:off + tm, :] * cw[t:t + 1, :]
            new = u[tm - hist:, :]
            ns_ref[0, :, st_col] = new
            if tps != 1:
                carry[0][j, idx] = new
            conv.append(c)
        act_ref[...] = (conv[0] * _sigmoid(conv[0]) * conv[1]).astype(BF16)

    def down_tile():
        return jnp.dot(act_ref[...], wd_ref[...], preferred_element_type=F32)

    @pl.when(j == 0)
    def _():
        h_ref[...] = _rms(x_ref[...], g_ref[...]).astype(BF16)
        y_ref[...] = jnp.zeros_like(y_ref)
        activate_tile()

    @pl.when((j > 0) & (j < nj))
    def _():
        y_ref[...] += down_tile()
        activate_tile()

    @pl.when(j == nj)
    def _():
        xo = x_ref[...] + y_ref[...] + down_tile()
        y_ref[...] = _rms(xo, gf_ref[...]) if final_norm else xo


def _ffn(x, norm_g, w_up, conv_w, conv_b, w_down, state, final_g, *, tm, shift, tps, final_norm):
    m, d = x.shape
    ff = w_down.shape[0]
    width = conv_w.shape[0]
    hist = (width - 1) * shift
    pad = -(-hist // SUBLANES) * SUBLANES
    tf = _pick(ff, (512, 256, 128))
    nj = ff // tf
    n_tiles = m // tm
    wide_state = hist * 2 * ff * 4 <= WIDE_STATE_BYTES
    kern = functools.partial(_ffn_kernel, tm=tm, tf=tf, ff=ff, shift=shift, pad=pad, tps=tps,
                             width=width, nj=nj, wide_state=wide_state, final_norm=final_norm)
    scratch = [pltpu.VMEM((tm, d), BF16), pltpu.VMEM((tm, tf), BF16), pltpu.VMEM((2, pad + tm, tf), F32)]
    cur = lambda j: jnp.minimum(j, nj - 1)
    prv = lambda j: jnp.maximum(j - 1, 0)
    if tps != 1:
        scratch.append(pltpu.VMEM((nj, 2, hist, tf), F32))
    if wide_state:
        st_in = [pl.BlockSpec((1, hist, 2 * ff), lambda i, j: (i // tps, 0, 0))] * 2
        st_out = [pl.BlockSpec((1, hist, 2 * ff), lambda i, j: (i, 0, 0))]
        st_shape = [jax.ShapeDtypeStruct((n_tiles, hist, 2 * ff), F32)]
    else:
        st_in = [pl.BlockSpec((1, hist, tf), lambda i, j: (i // tps, 0, cur(j))),
                 pl.BlockSpec((1, hist, tf), lambda i, j: (i // tps, 0, nj + cur(j)))]
        st_out = [pl.BlockSpec((1, hist, tf), lambda i, j: (i, 0, cur(j)))] * 2
        st_shape = [jax.ShapeDtypeStruct((n_tiles, hist, ff), F32)] * 2
    call = pl.pallas_call(
        kern if not wide_state else _single_history(kern),
        grid=(n_tiles, nj + 1),
        in_specs=[
            pl.BlockSpec((tm, d), lambda i, j: (i, 0), pipeline_mode=pl.Buffered(1)),
            pl.BlockSpec((1, d), lambda i, j: (0, 0)),
            pl.BlockSpec((d, tf), lambda i, j: (0, cur(j))),
            pl.BlockSpec((d, tf), lambda i, j: (0, nj + cur(j))),
            pl.BlockSpec((width, 2 * ff), lambda i, j: (0, 0)),
            pl.BlockSpec((1, 2 * ff), lambda i, j: (0, 0)),
            pl.BlockSpec((tf, d), lambda i, j: (prv(j), 0)),
            *st_in,
            pl.BlockSpec((1, d), lambda i, j: (0, 0)),
        ],
        out_specs=[pl.BlockSpec((tm, d), lambda i, j: (i, 0)), *st_out],
        out_shape=[jax.ShapeDtypeStruct((m, d), F32), *st_shape],
        scratch_shapes=scratch,
        compiler_params=_cparams(("arbitrary", "arbitrary")),
        name="convffn",
    )
    y, *ns = call(x, norm_g.reshape(1, d), w_up, w_up, conv_w, conv_b.reshape(1, 2 * ff), w_down,
                  state, state, final_g.reshape(1, d))
    new = ns[0] if wide_state else jnp.concatenate(ns, axis=-1)
    return y, new[tps - 1::tps]


def _single_history(kern):
    def wrapped(*refs):
        ins, (y_ref, ns_ref), scratch = refs[:10], refs[10:12], refs[12:]
        return kern(*ins, y_ref, ns_ref, ns_ref, *scratch)
    return wrapped


def _layer(x, st, p, final_g, final_norm, decode):
    conv_buf, s0, c0, n0, m0, ffn_buf = st
    B, L, D = x.shape
    Hd, Hm = s0.shape[1], c0.shape[1]
    dn_qkv = conv_buf.shape[-1]
    dn_vw = Hd * s0.shape[3]
    ff = p["w_down"].shape[0]

    T = next((c for c in (MIX_CHUNK, CHUNK) if L % c == 0), L)
    N = L // T
    R = _pick(B, (8, 4, 2, 1)) if decode else 1
    if R * T > 128:
        R = 1

    x2 = x.reshape(B * L, D)
    big, small = _inproj(x2, p["norm_mix_g"], p["w_big"], p["w_small"])
    big3 = big.reshape(B * N, T, big.shape[1])
    small3 = small.reshape(B * N, T, LANES)

    o_dn, s_new, o_ml, c_new, n_new, m_new = _mixers(
        big3, small3, conv_buf, s0, c0, n0, m0.reshape(B, 1, Hm), p["dn_conv_w"], p["par"],
        p["dn_norm_g"], p["ml_norm_g"], R=R, T=T, N=N, off_i=2 * Hd, off_f=2 * Hd + Hm)
    width = p["dn_conv_w"].shape[0]
    big4 = big.reshape(B, L, big.shape[1])
    if L >= width - 1:
        conv_new = big4[:, L - (width - 1):, :dn_qkv]
    else:
        conv_new = jnp.concatenate([conv_buf, big4[:, :, :dn_qkv]], axis=1)[:, L:]

    x1 = _outproj(x2, o_dn, o_ml, p["w_out"])

    fw = p["ffn_conv_w"].shape[0]
    if decode:
        x1t = x1.reshape(B, L, D).transpose(1, 0, 2).reshape(L * B, D)
        state = ffn_buf.transpose(1, 0, 2).reshape(1, (fw - 1) * B, 2 * ff)
        y, new = _ffn(x1t, p["norm_ffn_g"], p["w_up"], p["ffn_conv_w"], p["ffn_conv_b"],
                      p["w_down"], state, final_g, tm=L * B, shift=B, tps=1, final_norm=final_norm)
        y = y.reshape(L, B, D).transpose(1, 0, 2)
        ffn_new = new.reshape(fw - 1, B, 2 * ff).transpose(1, 0, 2)
    else:
        tm = _pick(L, (1024, 512, 256, 128, 64, 32, 16, 8))
        y, ffn_new = _ffn(x1, p["norm_ffn_g"], p["w_up"], p["ffn_conv_w"], p["ffn_conv_b"],
                          p["w_down"], ffn_buf, final_g, tm=tm, shift=1, tps=L // tm,
                          final_norm=final_norm)
        y = y.reshape(B, L, D)
    return y, (conv_new, s_new, c_new, n_new, m_new.reshape(B, Hm), ffn_new)


def _regroup_kernel(w_ref, big_ref, small_ref, *, c0, c1, c2):
    n = w_ref.shape[0]
    big_ref[:c0, :] = w_ref[:c0, :].astype(BF16)
    big_ref[c0:, :] = w_ref[c1:c2, :].astype(BF16)
    narrow = jnp.concatenate(
        [w_ref[c0:c1, :], w_ref[c2:, :], jnp.zeros((LANES - (c1 - c0) - (n - c2), w_ref.shape[1]), F32)],
        axis=0)
    small_ref[...] = narrow.astype(BF16)


def _regroup(w_t, c0, c1, c2):
    n, k = w_t.shape
    tk = _pick(k, (256, 128))
    nb = c0 + c2 - c1
    assert c0 % 16 == 0 and c1 % SUBLANES == 0 and c2 % SUBLANES == 0
    return pl.pallas_call(
        functools.partial(_regroup_kernel, c0=c0, c1=c1, c2=c2),
        grid=(k // tk,),
        in_specs=[pl.BlockSpec((n, tk), lambda i: (0, i))],
        out_specs=[pl.BlockSpec((nb, tk), lambda i: (0, i)), pl.BlockSpec((LANES, tk), lambda i: (0, i))],
        out_shape=[jax.ShapeDtypeStruct((nb, k), BF16), jax.ShapeDtypeStruct((LANES, k), BF16)],
        compiler_params=_cparams(("parallel",)),
        name="regroup",
    )(w_t)


def _prep_params(l, Hd, Hm, dn_qkv, dn_vw, ml_qk, ml_vw, norm_mix_g, w_in, dn_conv_w, dn_A_log,
                 dn_dt_bias, dn_norm_g, ml_i_bias, ml_f_bias, ml_norm_g, w_out, norm_ffn_g, w_up,
                 ffn_conv_w, ffn_conv_b, w_down):
    w = w_in[l]
    c0 = dn_qkv + dn_vw
    c1 = c0 + 2 * Hd
    c2 = c1 + 2 * ml_qk + 2 * ml_vw
    nsmall = 2 * Hd + 2 * Hm
    assert nsmall <= LANES and w.shape[1] == c2 + 2 * Hm
    w_big, w_small = _regroup(w.T, c0, c1, c2)
    z = lambda n: jnp.zeros((n,), F32)
    bias = jnp.concatenate([z(Hd), dn_dt_bias[l], ml_i_bias[l], ml_f_bias[l], z(LANES - nsmall)])
    alog = jnp.concatenate([z(Hd), dn_A_log[l], z(LANES - 2 * Hd)])
    par = jnp.zeros((SUBLANES, LANES), F32).at[0].set(bias).at[1].set(alog)
    return dict(norm_mix_g=norm_mix_g[l], w_big=w_big, w_small=w_small, dn_conv_w=dn_conv_w[l],
                par=par, dn_norm_g=dn_norm_g[l], ml_norm_g=ml_norm_g[l], w_out=w_out[l].astype(BF16),
                norm_ffn_g=norm_ffn_g[l], w_up=w_up[l].astype(BF16), ffn_conv_w=ffn_conv_w[l],
                ffn_conv_b=ffn_conv_b[l], w_down=w_down[l].astype(BF16))


def kernel(x_prompt, x_sample, state_dn_conv, state_dn_S, state_ml_C, state_ml_n, state_ml_m,
           state_ffn_conv, norm_mix_g, w_in, dn_conv_w, dn_A_log, dn_dt_bias, dn_norm_g,
           ml_i_bias, ml_f_bias, ml_norm_g, w_out, norm_ffn_g, w_up, ffn_conv_w, ffn_conv_b,
           w_down, norm_final_g):
    states = (state_dn_conv, state_dn_S, state_ml_C, state_ml_n, state_ml_m, state_ffn_conv)
    depth = w_in.shape[0]
    batch = x_prompt.shape[0]
    Hd, dk, dv = state_dn_S.shape[2:]
    Hm, mdk, mdv = state_ml_C.shape[2:]
    xp, xs = x_prompt, x_sample
    p_new = [[] for _ in states]
    s_new = [[] for _ in states]
    for l in range(depth):
        p = _prep_params(l, Hd, Hm, state_dn_conv.shape[-1], Hd * dv, Hm * mdk, Hm * mdv,
                         norm_mix_g, w_in, dn_conv_w, dn_A_log, dn_dt_bias, dn_norm_g, ml_i_bias,
                         ml_f_bias, ml_norm_g, w_out, norm_ffn_g, w_up, ffn_conv_w, ffn_conv_b,
                         w_down)
        last = l == depth - 1
        st_p = tuple(jnp.zeros((batch,) + s.shape[2:], s.dtype) for s in states)
        st_s = tuple(s[l] for s in states)
        xp, np_st = _layer(xp, st_p, p, norm_final_g, last, decode=False)
        xs, ns_st = _layer(xs, st_s, p, norm_final_g, last, decode=True)
        for i in range(len(states)):
            p_new[i].append(np_st[i])
            s_new[i].append(ns_st[i])
    outs_p = tuple(jnp.stack(a, axis=0) for a in p_new)
    outs_s = tuple(jnp.stack(a, axis=0) for a in s_new)
    return (xp, xs) + outs_p + outs_s
```

```python
import functools

import jax
import jax.numpy as jnp
from jax import lax
from jax.experimental import pallas as pl
from jax.experimental.pallas import tpu as pltpu

EPS = 1e-6
CHUNK = 64
MIX_CHUNK = 128
F32 = jnp.float32
BF16 = jnp.bfloat16
LANES = 128
SUBLANES = 8
VMEM_LIMIT = 56 * 1024 * 1024
WIDE_STATE_BYTES = 2 * 1024 * 1024


def _cparams(sem, vmem=VMEM_LIMIT, flags=None):
    return pltpu.CompilerParams(dimension_semantics=sem, vmem_limit_bytes=vmem, flags=flags)


def _dot(a, b):
    return jnp.dot(a.astype(BF16), b.astype(BF16), preferred_element_type=F32)


def _dot_nt(a, b):
    return lax.dot_general(a.astype(BF16), b.astype(BF16), (((1,), (1,)), ((), ())),
                           preferred_element_type=F32)


def _dot_tn(a, b):
    return lax.dot_general(a.astype(BF16), b.astype(BF16), (((0,), (0,)), ((), ())),
                           preferred_element_type=F32)


def _mask_dot_exact(mask_bf16, x):
    hi = x.astype(BF16)
    r1 = x - hi.astype(F32)
    mid = r1.astype(BF16)
    lo = (r1 - mid.astype(F32)).astype(BF16)
    d = functools.partial(jnp.dot, preferred_element_type=F32)
    return (d(mask_bf16, hi) + d(mask_bf16, mid)) + d(mask_bf16, lo)


def _softplus(x):
    return jnp.maximum(x, 0.0) + jnp.log1p(jnp.exp(-jnp.abs(x)))


def _sigmoid(x):
    return 1.0 / (1.0 + jnp.exp(-x))


def _rms(x, g):
    return x * lax.rsqrt(jnp.mean(x * x, axis=-1, keepdims=True) + EPS) * g


def _log2(n):
    assert n & (n - 1) == 0 and n > 0, n
    return n.bit_length() - 1


def _pick(n, cands):
    for c in cands:
        if n % c == 0:
            return c
    return n


def _inproj_kernel(x_ref, g_ref, wb_ref, ws_ref, ob_ref, os_ref, h_ref):
    nt = functools.partial(lax.dot_general, dimension_numbers=(((1,), (1,)), ((), ())),
                           preferred_element_type=F32)

    @pl.when(pl.program_id(1) == 0)
    def _():
        h = _rms(x_ref[...], g_ref[...]).astype(BF16)
        h_ref[...] = h
        os_ref[...] = nt(h, ws_ref[...])

    ob_ref[...] = nt(h_ref[...], wb_ref[...])


def _inproj(x, g, w_big_t, w_small_t):
    m, d = x.shape
    nb = w_big_t.shape[0]
    tm = _pick(m, (1024, 512, 256, 128, 64, 32, 16, 8))
    tn = _pick(nb, (1024, 512, 256, 128))
    return pl.pallas_call(
        _inproj_kernel,
        grid=(m // tm, nb // tn),
        in_specs=[
            pl.BlockSpec((tm, d), lambda i, j: (i, 0)),
            pl.BlockSpec((1, d), lambda i, j: (0, 0)),
            pl.BlockSpec((tn, d), lambda i, j: (j, 0)),
            pl.BlockSpec((LANES, d), lambda i, j: (0, 0)),
        ],
        out_specs=[
            pl.BlockSpec((tm, tn), lambda i, j: (i, j)),
            pl.BlockSpec((tm, LANES), lambda i, j: (i, 0)),
        ],
        out_shape=[jax.ShapeDtypeStruct((m, nb), F32), jax.ShapeDtypeStruct((m, LANES), F32)],
        scratch_shapes=[pltpu.VMEM((tm, d), BF16)],
        compiler_params=_cparams(("parallel", "arbitrary")),
        name="inproj",
    )(x, g.reshape(1, d), w_big_t, w_small_t)


def _seq_masks(R, T):
    RT = R * T
    sh = _log2(T)
    ri = lax.broadcasted_iota(jnp.int32, (RT, RT), 0)
    ci = lax.broadcasted_iota(jnp.int32, (RT, RT), 1)
    same = (ri >> sh) == (ci >> sh)
    tril = same & (ci <= ri)
    strict = same & (ci < ri)
    eye = ri == ci
    lastsel = same & ((ci & (T - 1)) == T - 1)
    rowseq = lax.broadcasted_iota(jnp.int32, (RT, 1), 0) >> sh
    return ri, ci, tril, strict, eye, lastsel, rowseq


def _row_form(col, eye):
    return jnp.sum(jnp.where(eye, col, 0.0), axis=0, keepdims=True)


def _at_last(row, lastsel):
    return jnp.sum(jnp.where(lastsel, row, 0.0), axis=1, keepdims=True)


def _unit_lower_inverses(lms, ri, ci, T):
    ds = None
    s = 1
    while s < T:
        sh = _log2(2 * s)
        blk = ((ri >> sh) == (ci >> sh)) & ((ri & (2 * s - 1)) >= s) & ((ci & (2 * s - 1)) < s)
        ms = [jnp.where(blk, lm, 0.0) for lm in lms]
        if ds is None:
            eye = jnp.where(ri == ci, 1.0, 0.0)
            ds = [eye - m for m in ms]
        else:
            dms = [_dot(d, m) for d, m in zip(ds, ms)]
            yield
            ds = [d - _dot(dm, d) for d, dm in zip(ds, dms)]
        yield
        s *= 2
    return ds


def _causal_conv(ext_ref, x_ref, cw_ref, T, width, cols):
    lo = SUBLANES - (width - 1)
    ext_ref[:, SUBLANES:SUBLANES + T, cols] = x_ref[:, :, cols]
    cw = cw_ref[:, cols]
    acc = None
    for j in range(width):
        term = ext_ref[:, lo + j:lo + j + T, cols] * cw[j:j + 1, :]
        acc = term if acc is None else acc + term
    ext_ref[:, lo:SUBLANES, cols] = ext_ref[:, lo + T:SUBLANES + T, cols]
    return acc


def _gdn_steps(qkv_ref, z_ref, sm_ref, cw_ref, par_ref, ng_ref, o_ref, s_ref, ext_ref, masks,
               *, R, T, H, DK, DV, width):
    RT = R * T
    QK = H * DK

    groups = []
    for c0, c1 in ((0, QK), (QK, 2 * QK), (2 * QK, 2 * QK + H * DV)):
        a = _causal_conv(ext_ref, qkv_ref, cw_ref, T, width, slice(c0, c1)).reshape(RT, c1 - c0)
        groups.append(a * _sigmoid(a))
        if R > 1:
            yield
    yq, yk, yv = groups
    if R == 1:
        yield
    z = z_ref[...].reshape(RT, H * DV)

    sm = sm_ref[...].reshape(RT, LANES)
    beta_all = _sigmoid(sm)
    g_all = -jnp.exp(par_ref[1:2, :]) * _softplus(sm + par_ref[0:1, :])

    ri, ci, tril, strict, eye, lastsel, rowseq = masks
    g_cum = _mask_dot_exact(jnp.where(tril, 1.0, 0.0).astype(BF16), g_all)
    yield

    hs = range(H)
    q = [yq[:, h * DK:(h + 1) * DK] for h in hs]
    k = [yk[:, h * DK:(h + 1) * DK] for h in hs]
    v = [yv[:, h * DV:(h + 1) * DV] for h in hs]
    q = [a * lax.rsqrt(jnp.sum(a * a, axis=-1, keepdims=True) + EPS) * (DK ** -0.5) for a in q]
    yield
    k = [a * lax.rsqrt(jnp.sum(a * a, axis=-1, keepdims=True) + EPS) for a in k]
    yield
    beta = [beta_all[:, h:h + 1] for h in hs]
    gc = [g_cum[:, H + h:H + h + 1] for h in hs]
    gr = [_row_form(a, eye) for a in gc]
    glast = [_at_last(a, lastsel) for a in gr]
    yield
    decay = [jnp.exp(jnp.where(tril, c - r, -jnp.inf)) for c, r in zip(gc, gr)]
    kb = [a * b for a, b in zip(k, beta)]
    yield
    lm = [jnp.where(strict, _dot_nt(a, b) * d, 0.0) for a, b, d in zip(kb, k, decay)]
    yield
    tinv = yield from _unit_lower_inverses(lm, ri, ci, T)
    eg = [jnp.exp(a) for a in gc]
    uw = [_dot(t, jnp.concatenate([a * b, c * e], axis=1))
          for t, a, b, c, e in zip(tinv, v, beta, kb, eg)]
    yield
    u = [a[:, :DV] for a in uw]
    w = [a[:, DV:] for a in uw]
    qk = [_dot_nt(a, b) * d for a, b, d in zip(q, k, decay)]
    yield
    qe = [a * e for a, e in zip(q, eg)]
    kd = [a * jnp.exp(l - c) for a, l, c in zip(k, glast, gc)]
    gl = [jnp.exp(a) for a in glast]
    yield

    if R == 1:
        s_old = [s_ref[0, h] for h in hs]
        ws = [_dot(jnp.concatenate([a, b], axis=0), c) for a, b, c in zip(w, qe, s_old)]
        yield
        v_new = [a - b[:RT] for a, b in zip(u, ws)]
        o = [a[RT:] + _dot(b, c) for a, b, c in zip(ws, qk, v_new)]
        yield
        for h in hs:
            s_ref[0, h] = s_old[h] * gl[h][0:1, :] + _dot_tn(kd[h], v_new[h])
        yield
    else:
        ws = [[_dot(jnp.concatenate([w[h][s * T:(s + 1) * T], qe[h][s * T:(s + 1) * T]], axis=0),
                    s_ref[s, h]) for s in range(R)] for h in hs]
        yield
        v_new = [u[h] - jnp.concatenate([a[:T] for a in ws[h]], axis=0) for h in hs]
        o = [jnp.concatenate([a[T:] for a in ws[h]], axis=0) + _dot(qk[h], v_new[h]) for h in hs]
        yield
        for h in hs:
            for s in range(R):
                kd_s = jnp.where(rowseq == s, kd[h], 0.0)
                s_ref[s, h] = s_ref[s, h] * gl[h][s * T:s * T + 1, :] + _dot_tn(kd_s, v_new[h])
            yield

    for h in hs:
        zh = z[:, h * DV:(h + 1) * DV]
        out = _rms(o[h], ng_ref[...]) * (zh * _sigmoid(zh))
        o_ref[:, h * DV:(h + 1) * DV] = out.astype(o_ref.dtype)
    yield


def _mlstm_steps(q_ref, k_ref, v_ref, og_ref, sm_ref, par_ref, ng_ref, o_ref, c_ref, n_ref, m_ref,
                 masks, *, R, T, H, DK, DV, off_i, off_f):
    RT = R * T
    ri, ci, tril, strict, eye, lastsel, rowseq = masks

    pre = sm_ref[...].reshape(RT, LANES) + par_ref[0:1, :]
    logf_all = -_softplus(-pre)
    b_all = _mask_dot_exact(jnp.where(tril, 1.0, 0.0).astype(BF16), logf_all)
    yield

    qa = q_ref[...].reshape(RT, H * DK)
    ka = k_ref[...].reshape(RT, H * DK)
    va = v_ref[...].reshape(RT, H * DV)
    oa = og_ref[...].reshape(RT, H * DV)

    hs = range(H)
    q = [qa[:, h * DK:(h + 1) * DK] * (DK ** -0.5) for h in hs]
    k = [ka[:, h * DK:(h + 1) * DK] for h in hs]
    v = [va[:, h * DV:(h + 1) * DV] for h in hs]
    bc = [b_all[:, off_f + h:off_f + h + 1] for h in hs]
    ic = [pre[:, off_i + h:off_i + h + 1] for h in hs]
    dm = [jnp.where(tril, b + _row_form(i - b, eye), -jnp.inf) for b, i in zip(bc, ic)]
    yield

    if R == 1:
        m_rows = [m_ref[0, :, h:h + 1] for h in hs]
    else:
        m_rows = []
        for h in hs:
            mr = jnp.zeros((RT, 1), F32)
            for s in range(R):
                mr = jnp.where(rowseq == s, m_ref[s, :, h:h + 1], mr)
            m_rows.append(mr)
    m_new = [jnp.maximum(b + m, jnp.max(d, axis=1, keepdims=True)) for b, m, d in zip(bc, m_rows, dm)]
    inter = [jnp.exp(b + m - mn) for b, m, mn in zip(bc, m_rows, m_new)]
    yield
    smat = [_dot_nt(a, b) * jnp.exp(d - mn) for a, b, d, mn in zip(q, k, dm, m_new)]
    yield

    if R == 1:
        qc = [_dot(q[h], c_ref[0, h]) for h in hs]
        qn = [jnp.sum(q[h] * n_ref[0, h:h + 1, :], axis=1, keepdims=True) for h in hs]
    else:
        qc, qn = [], []
        for h in hs:
            rows = [slice(s * T, (s + 1) * T) for s in range(R)]
            qc.append(jnp.concatenate([_dot(q[h][r], c_ref[s, h]) for s, r in enumerate(rows)], axis=0))
            qn.append(jnp.concatenate(
                [jnp.sum(q[h][r] * n_ref[s, h:h + 1, :], axis=1, keepdims=True)
                 for s, r in enumerate(rows)], axis=0))
            yield
    yield
    num = [i * c + _dot(s, a) for i, c, s, a in zip(inter, qc, smat, v)]
    den = [i * n + jnp.sum(s, axis=1, keepdims=True) for i, n, s in zip(inter, qn, smat)]
    yield
    hh = [a / jnp.maximum(jnp.abs(b), jnp.exp(-mn)) for a, b, mn in zip(num, den, m_new)]

    b_last = [_at_last(_row_form(b, eye), lastsel) for b in bc]
    m_end = [_at_last(_row_form(mn, eye), lastsel) for mn in m_new]
    yield
    kw = [a * jnp.exp(bl - b + i - me) for a, bl, b, i, me in zip(k, b_last, bc, ic, m_end)]
    cs = [jnp.exp(bl + m - me) for bl, m, me in zip(b_last, m_rows, m_end)]
    yield
    for h in hs:
        for s in range(R):
            kw_s = kw[h] if R == 1 else jnp.where(rowseq == s, kw[h], 0.0)
            cs_s = cs[h][s * T:s * T + 1, :]
            c_ref[s, h] = cs_s * c_ref[s, h] + _dot_tn(kw_s, v[h])
            n_ref[s, h:h + 1, :] = cs_s * n_ref[s, h:h + 1, :] + jnp.sum(kw_s, axis=0, keepdims=True)
            m_ref[s, :, h:h + 1] = m_end[h][s * T:s * T + 1, :]
        yield

    for h in hs:
        og = oa[:, h * DV:(h + 1) * DV]
        out = _rms(hh[h], ng_ref[:, h * DV:(h + 1) * DV]) * _sigmoid(og)
        o_ref[:, h * DV:(h + 1) * DV] = out.astype(o_ref.dtype)
    yield


def _mixers_kernel(qkv_ref, z_ref, mq_ref, mk_ref, mv_ref, mo_ref, sm_ref, cst_ref, s0_ref, c0_ref,
                   n0_ref, m0_ref, cw_ref, par_ref, dng_ref, mng_ref,
                   odn_ref, s_ref, oml_ref, c_ref, n_ref, m_ref, ext_ref,
                   *, R, T, Hd, Hm, DK, DV, MDK, MDV, width, off_i, off_f):
    @pl.when(pl.program_id(1) == 0)
    def _():
        ext_ref[:, SUBLANES - (width - 1):SUBLANES, :] = cst_ref[...]
        s_ref[...] = s0_ref[...]
        c_ref[...] = c0_ref[...]
        n_ref[...] = n0_ref[...]
        m_ref[...] = m0_ref[...]

    masks = _seq_masks(R, T)
    streams = [
        _gdn_steps(qkv_ref, z_ref, sm_ref, cw_ref, par_ref, dng_ref, odn_ref, s_ref, ext_ref, masks,
                   R=R, T=T, H=Hd, DK=DK, DV=DV, width=width),
        _mlstm_steps(mq_ref, mk_ref, mv_ref, mo_ref, sm_ref, par_ref, mng_ref, oml_ref, c_ref, n_ref,
                     m_ref, masks, R=R, T=T, H=Hm, DK=MDK, DV=MDV, off_i=off_i, off_f=off_f),
    ]
    turns = [2, 1]
    while streams:
        for gen, n in list(zip(streams, turns)):
            for _ in range(n):
                if gen in streams and next(gen, "done") == "done":
                    turns.pop(streams.index(gen))
                    streams.remove(gen)


def _mixers(proj3, small3, conv_state, s0, c0, n0, m0, conv_w, par, dn_norm_g, ml_norm_g,
            *, R, T, N, off_i, off_f):
    B, Hd, DK, DV = s0.shape
    _, Hm, MDK, MDV = c0.shape
    QKV = conv_state.shape[-1]
    VW, MQK, MVW = Hd * DV, Hm * MDK, Hm * MDV
    width = conv_w.shape[0]
    col0 = QKV + VW
    assert QKV == 2 * Hd * DK + VW and QKV % VW == 0 and VW % LANES == 0
    assert col0 % MQK == 0 and (col0 + 2 * MQK) % MVW == 0 and MQK % LANES == 0
    rows = proj3.shape[0] * T
    qb, vb = col0 // MQK, (col0 + 2 * MQK) // MVW
    kern = functools.partial(_mixers_kernel, R=R, T=T, Hd=Hd, Hm=Hm, DK=DK, DV=DV, MDK=MDK, MDV=MDV,
                             width=width, off_i=off_i, off_f=off_f)
    tok = lambda w, c: pl.BlockSpec((R, T, w), lambda b, n: (b * N + n, 0, c))
    const = lambda shape: pl.BlockSpec(shape, lambda b, n: (0,) * len(shape))
    per_seq = lambda *dims: pl.BlockSpec((R,) + dims, lambda b, n: (b,) + (0,) * len(dims))
    state_specs = [per_seq(Hd, DK, DV), per_seq(Hm, MDK, MDV), per_seq(Hm, MDK), per_seq(1, Hm)]
    odn, s_new, oml, c_new, n_new, m_new = pl.pallas_call(
        kern,
        grid=(B // R, N),
        in_specs=[
            tok(QKV, 0), tok(VW, QKV // VW), tok(MQK, qb), tok(MQK, qb + 1), tok(MVW, vb),
            tok(MVW, vb + 1), tok(LANES, 0),
            per_seq(width - 1, QKV), *state_specs,
            const((width, QKV)), const((SUBLANES, LANES)), const((1, DV)), const((1, MVW)),
        ],
        out_specs=[pl.BlockSpec((R * T, VW), lambda b, n: (b * N + n, 0)), state_specs[0],
                   pl.BlockSpec((R * T, MVW), lambda b, n: (b * N + n, 0)), *state_specs[1:]],
        out_shape=[jax.ShapeDtypeStruct((rows, VW), BF16), jax.ShapeDtypeStruct(s0.shape, F32),
                   jax.ShapeDtypeStruct((rows, MVW), BF16), jax.ShapeDtypeStruct(c0.shape, F32),
                   jax.ShapeDtypeStruct(n0.shape, F32), jax.ShapeDtypeStruct(m0.shape, F32)],
        scratch_shapes=[pltpu.VMEM((R, SUBLANES + T, QKV), F32)],
        compiler_params=_cparams(("parallel", "arbitrary")),
        name="mixers",
    )(proj3, proj3, proj3, proj3, proj3, proj3, small3, conv_state, s0, c0, n0, m0, conv_w, par,
      dn_norm_g.reshape(1, DV), ml_norm_g.reshape(1, MVW))
    return odn, s_new, oml, c_new, n_new, m_new


def _outproj_kernel(x_ref, a_ref, b_ref, wa_ref, wb_ref, o_ref):
    o_ref[...] = (x_ref[...] + jnp.dot(a_ref[...], wa_ref[...], preferred_element_type=F32)
                  + jnp.dot(b_ref[...], wb_ref[...], preferred_element_type=F32))


def _outproj(x, a, b, w_out):
    m, d = x.shape
    ka, kb = a.shape[1], b.shape[1]
    assert ka == kb
    tm = _pick(m, (512, 256, 128, 64, 32, 16, 8))
    return pl.pallas_call(
        _outproj_kernel,
        grid=(m // tm,),
        in_specs=[
            pl.BlockSpec((tm, d), lambda i: (i, 0)),
            pl.BlockSpec((tm, ka), lambda i: (i, 0)),
            pl.BlockSpec((tm, kb), lambda i: (i, 0)),
            pl.BlockSpec((ka, d), lambda i: (0, 0), pipeline_mode=pl.Buffered(1)),
            pl.BlockSpec((kb, d), lambda i: (1, 0), pipeline_mode=pl.Buffered(1)),
        ],
        out_specs=pl.BlockSpec((tm, d), lambda i: (i, 0)),
        out_shape=jax.ShapeDtypeStruct((m, d), F32),
        compiler_params=_cparams(("parallel",)),
        name="outproj",
    )(x, a, b, w_out, w_out)


def _ffn_kernel(x_ref, g_ref, wg_ref, wu_ref, cw_ref, cb_ref, wd_ref, stg_ref, stu_ref, gf_ref,
                y_ref, nsg_ref, nsu_ref, h_ref, act_ref, ext_ref, *carry,
                tm, tf, ff, shift, pad, tps, width, nj, wide_state, final_norm):
    i = pl.program_id(0)
    j = pl.program_id(1)
    hist = (width - 1) * shift

    def activate_tile():
        conv = []
        for idx, (w_ref, st_ref, ns_ref) in enumerate(
                ((wg_ref, stg_ref, nsg_ref), (wu_ref, stu_ref, nsu_ref))):
            col = pl.ds(pl.multiple_of(idx * ff + j * tf, LANES), tf)
            st_col = col if wide_state else slice(None)
            u = jnp.dot(h_ref[...], w_ref[...], preferred_element_type=F32)
            ext_ref[idx, pad:pad + tm, :] = u
            if tps == 1:
                ext_ref[idx, pad - hist:pad, :] = st_ref[0, :, st_col]
            else:
                ext_ref[idx, pad - hist:pad, :] = jnp.where(i % tps == 0, st_ref[0, :, st_col],
                                                            carry[0][j, idx])
            cw = cw_ref[:, col]
            c = u * cw[width - 1:width, :] + cb_ref[:, col]
            for t in range(width - 1):
                off = pad - hist + t * shift
                c = c + ext_ref[idx, off:off + tm, :] * cw[t:t + 1, :]
            new = u[tm - hist:, :]
            ns_ref[0, :, st_col] = new
            if tps != 1:
                carry[0][j, idx] = new
            conv.append(c)
        act_ref[...] = (conv[0] * _sigmoid(conv[0]) * conv[1]).astype(BF16)

    def down_tile():
        return jnp.dot(act_ref[...], wd_ref[...], preferred_element_type=F32)

    @pl.when(j == 0)
    def _():
        h_ref[...] = _rms(x_ref[...], g_ref[...]).astype(BF16)
        y_ref[...] = jnp.zeros_like(y_ref)
        activate_tile()

    @pl.when((j > 0) & (j < nj))
    def _():
        y_ref[...] += down_tile()
        activate_tile()

    @pl.when(j == nj)
    def _():
        xo = x_ref[...] + y_ref[...] + down_tile()
        y_ref[...] = _rms(xo, gf_ref[...]) if final_norm else xo


def _ffn(x, norm_g, w_up, conv_w, conv_b, w_down, state, final_g, *, tm, shift, tps, final_norm):
    m, d = x.shape
    ff = w_down.shape[0]
    width = conv_w.shape[0]
    hist = (width - 1) * shift
    pad = -(-hist // SUBLANES) * SUBLANES
    tf = _pick(ff, (512, 256, 128))
    nj = ff // tf
    n_tiles = m // tm
    wide_state = hist * 2 * ff * 4 <= WIDE_STATE_BYTES
    kern = functools.partial(_ffn_kernel, tm=tm, tf=tf, ff=ff, shift=shift, pad=pad, tps=tps,
                             width=width, nj=nj, wide_state=wide_state, final_norm=final_norm)
    scratch = [pltpu.VMEM((tm, d), BF16), pltpu.VMEM((tm, tf), BF16), pltpu.VMEM((2, pad + tm, tf), F32)]
    cur = lambda j: jnp.minimum(j, nj - 1)
    prv = lambda j: jnp.maximum(j - 1, 0)
    if tps != 1:
        scratch.append(pltpu.VMEM((nj, 2, hist, tf), F32))
    if wide_state:
        st_in = [pl.BlockSpec((1, hist, 2 * ff), lambda i, j: (i // tps, 0, 0))] * 2
        st_out = [pl.BlockSpec((1, hist, 2 * ff), lambda i, j: (i, 0, 0))]
        st_shape = [jax.ShapeDtypeStruct((n_tiles, hist, 2 * ff), F32)]
    else:
        st_in = [pl.BlockSpec((1, hist, tf), lambda i, j: (i // tps, 0, cur(j))),
                 pl.BlockSpec((1, hist, tf), lambda i, j: (i // tps, 0, nj + cur(j)))]
        st_out = [pl.BlockSpec((1, hist, tf), lambda i, j: (i, 0, cur(j)))] * 2
        st_shape = [jax.ShapeDtypeStruct((n_tiles, hist, ff), F32)] * 2
    call = pl.pallas_call(
        kern if not wide_state else _single_history(kern),
        grid=(n_tiles, nj + 1),
        in_specs=[
            pl.BlockSpec((tm, d), lambda i, j: (i, 0), pipeline_mode=pl.Buffered(1)),
            pl.BlockSpec((1, d), lambda i, j: (0, 0)),
            pl.BlockSpec((d, tf), lambda i, j: (0, cur(j))),
            pl.BlockSpec((d, tf), lambda i, j: (0, nj + cur(j))),
            pl.BlockSpec((width, 2 * ff), lambda i, j: (0, 0)),
            pl.BlockSpec((1, 2 * ff), lambda i, j: (0, 0)),
            pl.BlockSpec((tf, d), lambda i, j: (prv(j), 0)),
            *st_in,
            pl.BlockSpec((1, d), lambda i, j: (0, 0)),
        ],
        out_specs=[pl.BlockSpec((tm, d), lambda i, j: (i, 0)), *st_out],
        out_shape=[jax.ShapeDtypeStruct((m, d), F32), *st_shape],
        scratch_shapes=scratch,
        compiler_params=_cparams(("arbitrary", "arbitrary")),
        name="convffn",
    )
    y, *ns = call(x, norm_g.reshape(1, d), w_up, w_up, conv_w, conv_b.reshape(1, 2 * ff), w_down,
                  state, state, final_g.reshape(1, d))
    new = ns[0] if wide_state else jnp.concatenate(ns, axis=-1)
    return y, new[tps - 1::tps]


def _single_history(kern):
    def wrapped(*refs):
        ins, (y_ref, ns_ref), scratch = refs[:10], refs[10:12], refs[12:]
        return kern(*ins, y_ref, ns_ref, ns_ref, *scratch)
    return wrapped


def _layer(x, st, p, final_g, final_norm, decode):
    conv_buf, s0, c0, n0, m0, ffn_buf = st
    B, L, D = x.shape
    Hd, Hm = s0.shape[1], c0.shape[1]
    dn_qkv = conv_buf.shape[-1]
    dn_vw = Hd * s0.shape[3]
    ff = p["w_down"].shape[0]

    T = next((c for c in (MIX_CHUNK, CHUNK) if L % c == 0), L)
    N = L // T
    R = _pick(B, (8, 4, 2, 1)) if decode else 1
    if R * T > 128:
        R = 1

    x2 = x.reshape(B * L, D)
    big, small = _inproj(x2, p["norm_mix_g"], p["w_big"], p["w_small"])
    big3 = big.reshape(B * N, T, big.shape[1])
    small3 = small.reshape(B * N, T, LANES)

    o_dn, s_new, o_ml, c_new, n_new, m_new = _mixers(
        big3, small3, conv_buf, s0, c0, n0, m0.reshape(B, 1, Hm), p["dn_conv_w"], p["par"],
        p["dn_norm_g"], p["ml_norm_g"], R=R, T=T, N=N, off_i=2 * Hd, off_f=2 * Hd + Hm)
    width = p["dn_conv_w"].shape[0]
    big4 = big.reshape(B, L, big.shape[1])
    if L >= width - 1:
        conv_new = big4[:, L - (width - 1):, :dn_qkv]
    else:
        conv_new = jnp.concatenate([conv_buf, big4[:, :, :dn_qkv]], axis=1)[:, L:]

    x1 = _outproj(x2, o_dn, o_ml, p["w_out"])

    fw = p["ffn_conv_w"].shape[0]
    if decode:
        x1t = x1.reshape(B, L, D).transpose(1, 0, 2).reshape(L * B, D)
        state = ffn_buf.transpose(1, 0, 2).reshape(1, (fw - 1) * B, 2 * ff)
        y, new = _ffn(x1t, p["norm_ffn_g"], p["w_up"], p["ffn_conv_w"], p["ffn_conv_b"],
                      p["w_down"], state, final_g, tm=L * B, shift=B, tps=1, final_norm=final_norm)
        y = y.reshape(L, B, D).transpose(1, 0, 2)
        ffn_new = new.reshape(fw - 1, B, 2 * ff).transpose(1, 0, 2)
    else:
        tm = _pick(L, (1024, 512, 256, 128, 64, 32, 16, 8))
        y, ffn_new = _ffn(x1, p["norm_ffn_g"], p["w_up"], p["ffn_conv_w"], p["ffn_conv_b"],
                          p["w_down"], ffn_buf, final_g, tm=tm, shift=1, tps=L // tm,
                          final_norm=final_norm)
        y = y.reshape(B, L, D)
    return y, (conv_new, s_new, c_new, n_new, m_new.reshape(B, Hm), ffn_new)


def _regroup_kernel(w_ref, big_ref, small_ref, *, c0, c1, c2):
    n = w_ref.shape[0]
    big_ref[:c0, :] = w_ref[:c0, :].astype(BF16)
    big_ref[c0:, :] = w_ref[c1:c2, :].astype(BF16)
    narrow = jnp.concatenate(
        [w_ref[c0:c1, :], w_ref[c2:, :], jnp.zeros((LANES - (c1 - c0) - (n - c2), w_ref.shape[1]), F32)],
        axis=0)
    small_ref[...] = narrow.astype(BF16)


def _regroup(w_t, c0, c1, c2):
    n, k = w_t.shape
    tk = _pick(k, (256, 128))
    nb = c0 + c2 - c1
    assert c0 % 16 == 0 and c1 % SUBLANES == 0 and c2 % SUBLANES == 0
    return pl.pallas_call(
        functools.partial(_regroup_kernel, c0=c0, c1=c1, c2=c2),
        grid=(k // tk,),
        in_specs=[pl.BlockSpec((n, tk), lambda i: (0, i))],
        out_specs=[pl.BlockSpec((nb, tk), lambda i: (0, i)), pl.BlockSpec((LANES, tk), lambda i: (0, i))],
        out_shape=[jax.ShapeDtypeStruct((nb, k), BF16), jax.ShapeDtypeStruct((LANES, k), BF16)],
        compiler_params=_cparams(("parallel",)),
        name="regroup",
    )(w_t)


def _prep_params(l, Hd, Hm, dn_qkv, dn_vw, ml_qk, ml_vw, norm_mix_g, w_in, dn_conv_w, dn_A_log,
                 dn_dt_bias, dn_norm_g, ml_i_bias, ml_f_bias, ml_norm_g, w_out, norm_ffn_g, w_up,
                 ffn_conv_w, ffn_conv_b, w_down):
    w = w_in[l]
    c0 = dn_qkv + dn_vw
    c1 = c0 + 2 * Hd
    c2 = c1 + 2 * ml_qk + 2 * ml_vw
    nsmall = 2 * Hd + 2 * Hm
    assert nsmall <= LANES and w.shape[1] == c2 + 2 * Hm
    w_big, w_small = _regroup(w.T, c0, c1, c2)
    z = lambda n: jnp.zeros((n,), F32)
    bias = jnp.concatenate([z(Hd), dn_dt_bias[l], ml_i_bias[l], ml_f_bias[l], z(LANES - nsmall)])
    alog = jnp.concatenate([z(Hd), dn_A_log[l], z(LANES - 2 * Hd)])
    par = jnp.zeros((SUBLANES, LANES), F32).at[0].set(bias).at[1].set(alog)
    return dict(norm_mix_g=norm_mix_g[l], w_big=w_big, w_small=w_small, dn_conv_w=dn_conv_w[l],
                par=par, dn_norm_g=dn_norm_g[l], ml_norm_g=ml_norm_g[l], w_out=w_out[l].astype(BF16),
                norm_ffn_g=norm_ffn_g[l], w_up=w_up[l].astype(BF16), ffn_conv_w=ffn_conv_w[l],
                ffn_conv_b=ffn_conv_b[l], w_down=w_down[l].astype(BF16))


def kernel(x_prompt, x_sample, state_dn_conv, state_dn_S, state_ml_C, state_ml_n, state_ml_m,
           state_ffn_conv, norm_mix_g, w_in, dn_conv_w, dn_A_log, dn_dt_bias, dn_norm_g,
           ml_i_bias, ml_f_bias, ml_norm_g, w_out, norm_ffn_g, w_up, ffn_conv_w, ffn_conv_b,
           w_down, norm_final_g):
    states = (state_dn_conv, state_dn_S, state_ml_C, state_ml_n, state_ml_m, state_ffn_conv)
    depth = w_in.shape[0]
    batch = x_prompt.shape[0]
    Hd, dk, dv = state_dn_S.shape[2:]
    Hm, mdk, mdv = state_ml_C.shape[2:]
    xp, xs = x_prompt, x_sample
    p_new = [[] for _ in states]
    s_new = [[] for _ in states]
    for l in range(depth):
        p = _prep_params(l, Hd, Hm, state_dn_conv.shape[-1], Hd * dv, Hm * mdk, Hm * mdv,
                         norm_mix_g, w_in, dn_conv_w, dn_A_log, dn_dt_bias, dn_norm_g, ml_i_bias,
                         ml_f_bias, ml_norm_g, w_out, norm_ffn_g, w_up, ffn_conv_w, ffn_conv_b,
                         w_down)
        last = l == depth - 1
        st_p = tuple(jnp.zeros((batch,) + s.shape[2:], s.dtype) for s in states)
        st_s = tuple(s[l] for s in states)
        xp, np_st = _layer(xp, st_p, p, norm_final_g, last, decode=False)
        xs, ns_st = _layer(xs, st_s, p, norm_final_g, last, decode=True)
        for i in range(len(states)):
            p_new[i].append(np_st[i])
            s_new[i].append(ns_st[i])
    outs_p = tuple(jnp.stack(a, axis=0) for a in p_new)
    outs_s = tuple(jnp.stack(a, axis=0) for a in s_new)
    return (xp, xs) + outs_p + outs_s
```

```python
import functools

import jax
import jax.numpy as jnp
from jax import lax
from jax.experimental import pallas as pl
from jax.experimental.pallas import tpu as pltpu

EPS = 1e-6
CHUNK = 64
MIX_CHUNK = 128
F32 = jnp.float32
BF16 = jnp.bfloat16
LANES = 128
SUBLANES = 8
VMEM_LIMIT = 56 * 1024 * 1024
WIDE_STATE_BYTES = 2 * 1024 * 1024


def _cparams(sem, vmem=VMEM_LIMIT, flags=None):
    return pltpu.CompilerParams(dimension_semantics=sem, vmem_limit_bytes=vmem, flags=flags)


def _dot(a, b):
    return jnp.dot(a.astype(BF16), b.astype(BF16), preferred_element_type=F32)


def _dot_nt(a, b):
    return lax.dot_general(a.astype(BF16), b.astype(BF16), (((1,), (1,)), ((), ())),
                           preferred_element_type=F32)


def _dot_tn(a, b):
    return lax.dot_general(a.astype(BF16), b.astype(BF16), (((0,), (0,)), ((), ())),
                           preferred_element_type=F32)


def _mask_dot_exact(mask_bf16, x):
    hi = x.astype(BF16)
    r1 = x - hi.astype(F32)
    mid = r1.astype(BF16)
    lo = (r1 - mid.astype(F32)).astype(BF16)
    d = functools.partial(jnp.dot, preferred_element_type=F32)
    return (d(mask_bf16, hi) + d(mask_bf16, mid)) + d(mask_bf16, lo)


def _softplus(x):
    return jnp.maximum(x, 0.0) + jnp.log1p(jnp.exp(-jnp.abs(x)))


def _sigmoid(x):
    return 1.0 / (1.0 + jnp.exp(-x))


def _rms(x, g):
    return x * lax.rsqrt(jnp.mean(x * x, axis=-1, keepdims=True) + EPS) * g


def _log2(n):
    assert n & (n - 1) == 0 and n > 0, n
    return n.bit_length() - 1


def _pick(n, cands):
    for c in cands:
        if n % c == 0:
            return c
    return n


def _inproj_kernel(x_ref, g_ref, wb_ref, ws_ref, ob_ref, os_ref, h_ref):
    nt = functools.partial(lax.dot_general, dimension_numbers=(((1,), (1,)), ((), ())),
                           preferred_element_type=F32)

    @pl.when(pl.program_id(1) == 0)
    def _():
        h = _rms(x_ref[...], g_ref[...]).astype(BF16)
        h_ref[...] = h
        os_ref[...] = nt(h, ws_ref[...])

    ob_ref[...] = nt(h_ref[...], wb_ref[...])


def _inproj(x, g, w_big_t, w_small_t):
    m, d = x.shape
    nb = w_big_t.shape[0]
    tm = _pick(m, (1024, 512, 256, 128, 64, 32, 16, 8))
    tn = _pick(nb, (1024, 512, 256, 128))
    return pl.pallas_call(
        _inproj_kernel,
        grid=(m // tm, nb // tn),
        in_specs=[
            pl.BlockSpec((tm, d), lambda i, j: (i, 0)),
            pl.BlockSpec((1, d), lambda i, j: (0, 0)),
            pl.BlockSpec((tn, d), lambda i, j: (j, 0)),
            pl.BlockSpec((LANES, d), lambda i, j: (0, 0)),
        ],
        out_specs=[
            pl.BlockSpec((tm, tn), lambda i, j: (i, j)),
            pl.BlockSpec((tm, LANES), lambda i, j: (i, 0)),
        ],
        out_shape=[jax.ShapeDtypeStruct((m, nb), F32), jax.ShapeDtypeStruct((m, LANES), F32)],
        scratch_shapes=[pltpu.VMEM((tm, d), BF16)],
        compiler_params=_cparams(("parallel", "arbitrary")),
        name="inproj",
    )(x, g.reshape(1, d), w_big_t, w_small_t)


def _seq_masks(R, T):
    RT = R * T
    sh = _log2(T)
    ri = lax.broadcasted_iota(jnp.int32, (RT, RT), 0)
    ci = lax.broadcasted_iota(jnp.int32, (RT, RT), 1)
    same = (ri >> sh) == (ci >> sh)
    tril = same & (ci <= ri)
    strict = same & (ci < ri)
    eye = ri == ci
    lastsel = same & ((ci & (T - 1)) == T - 1)
    rowseq = lax.broadcasted_iota(jnp.int32, (RT, 1), 0) >> sh
    return ri, ci, tril, strict, eye, lastsel, rowseq


def _row_form(col, eye):
    return jnp.sum(jnp.where(eye, col, 0.0), axis=0, keepdims=True)


def _at_last(row, lastsel):
    return jnp.sum(jnp.where(lastsel, row, 0.0), axis=1, keepdims=True)


def _unit_lower_inverses(lms, ri, ci, T):
    ds = None
    s = 1
    while s < T:
        sh = _log2(2 * s)
        blk = ((ri >> sh) == (ci >> sh)) & ((ri & (2 * s - 1)) >= s) & ((ci & (2 * s - 1)) < s)
        ms = [jnp.where(blk, lm, 0.0) for lm in lms]
        if ds is None:
            eye = jnp.where(ri == ci, 1.0, 0.0)
            ds = [eye - m for m in ms]
        else:
            dms = [_dot(d, m) for d, m in zip(ds, ms)]
            yield
            ds = [d - _dot(dm, d) for d, dm in zip(ds, dms)]
        yield
        s *= 2
    return ds


def _causal_conv(ext_ref, x_ref, cw_ref, T, width, cols):
    lo = SUBLANES - (width - 1)
    ext_ref[:, SUBLANES:SUBLANES + T, cols] = x_ref[:, :, cols]
    cw = cw_ref[:, cols]
    acc = None
    for j in range(width):
        term = ext_ref[:, lo + j:lo + j + T, cols] * cw[j:j + 1, :]
        acc = term if acc is None else acc + term
    ext_ref[:, lo:SUBLANES, cols] = ext_ref[:, lo + T:SUBLANES + T, cols]
    return acc


def _gdn_steps(qkv_ref, z_ref, sm_ref, cw_ref, par_ref, ng_ref, o_ref, s_ref, ext_ref, masks,
               *, R, T, H, DK, DV, width):
    RT = R * T
    QK = H * DK

    groups = []
    for c0, c1 in ((0, QK), (QK, 2 * QK), (2 * QK, 2 * QK + H * DV)):
        a = _causal_conv(ext_ref, qkv_ref, cw_ref, T, width, slice(c0, c1)).reshape(RT, c1 - c0)
        groups.append(a * _sigmoid(a))
        if R > 1:
            yield
    yq, yk, yv = groups
    if R == 1:
        yield
    z = z_ref[...].reshape(RT, H * DV)

    sm = sm_ref[...].reshape(RT, LANES)
    beta_all = _sigmoid(sm)
    g_all = -jnp.exp(par_ref[1:2, :]) * _softplus(sm + par_ref[0:1, :])

    ri, ci, tril, strict, eye, lastsel, rowseq = masks
    g_cum = _mask_dot_exact(jnp.where(tril, 1.0, 0.0).astype(BF16), g_all)
    yield

    hs = range(H)
    q = [yq[:, h * DK:(h + 1) * DK] for h in hs]
    k = [yk[:, h * DK:(h + 1) * DK] for h in hs]
    v = [yv[:, h * DV:(h + 1) * DV] for h in hs]
    q = [a * lax.rsqrt(jnp.sum(a * a, axis=-1, keepdims=True) + EPS) * (DK ** -0.5) for a in q]
    yield
    k = [a * lax.rsqrt(jnp.sum(a * a, axis=-1, keepdims=True) + EPS) for a in k]
    yield
    beta = [beta_all[:, h:h + 1] for h in hs]
    gc = [g_cum[:, H + h:H + h + 1] for h in hs]
    gr = [_row_form(a, eye) for a in gc]
    glast = [_at_last(a, lastsel) for a in gr]
    yield
    decay = [jnp.exp(jnp.where(tril, c - r, -jnp.inf)) for c, r in zip(gc, gr)]
    kb = [a * b for a, b in zip(k, beta)]
    yield
    lm = [jnp.where(strict, _dot_nt(a, b) * d, 0.0) for a, b, d in zip(kb, k, decay)]
    yield
    tinv = yield from _unit_lower_inverses(lm, ri, ci, T)
    eg = [jnp.exp(a) for a in gc]
    uw = [_dot(t, jnp.concatenate([a * b, c * e], axis=1))
          for t, a, b, c, e in zip(tinv, v, beta, kb, eg)]
    yield
    u = [a[:, :DV] for a in uw]
    w = [a[:, DV:] for a in uw]
    qk = [_dot_nt(a, b) * d for a, b, d in zip(q, k, decay)]
    yield
    qe = [a * e for a, e in zip(q, eg)]
    kd = [a * jnp.exp(l - c) for a, l, c in zip(k, glast, gc)]
    gl = [jnp.exp(a) for a in glast]
    yield

    if R == 1:
        s_old = [s_ref[0, h] for h in hs]
        ws = [_dot(jnp.concatenate([a, b], axis=0), c) for a, b, c in zip(w, qe, s_old)]
        yield
        v_new = [a - b[:RT] for a, b in zip(u, ws)]
        o = [a[RT:] + _dot(b, c) for a, b, c in zip(ws, qk, v_new)]
        yield
        for h in hs:
            s_ref[0, h] = s_old[h] * gl[h][0:1, :] + _dot_tn(kd[h], v_new[h])
        yield
    else:
        ws = [[_dot(jnp.concatenate([w[h][s * T:(s + 1) * T], qe[h][s * T:(s + 1) * T]], axis=0),
                    s_ref[s, h]) for s in range(R)] for h in hs]
        yield
        v_new = [u[h] - jnp.concatenate([a[:T] for a in ws[h]], axis=0) for h in hs]
        o = [jnp.concatenate([a[T:] for a in ws[h]], axis=0) + _dot(qk[h], v_new[h]) for h in hs]
        yield
        for h in hs:
            for s in range(R):
                kd_s = jnp.where(rowseq == s, kd[h], 0.0)
                s_ref[s, h] = s_ref[s, h] * gl[h][s * T:s * T + 1, :] + _dot_tn(kd_s, v_new[h])
            yield

    for h in hs:
        zh = z[:, h * DV:(h + 1) * DV]
        out = _rms(o[h], ng_ref[...]) * (zh * _sigmoid(zh))
        o_ref[:, h * DV:(h + 1) * DV] = out.astype(o_ref.dtype)
    yield


def _mlstm_steps(q_ref, k_ref, v_ref, og_ref, sm_ref, par_ref, ng_ref, o_ref, c_ref, n_ref, m_ref,
                 masks, *, R, T, H, DK, DV, off_i, off_f):
    RT = R * T
    ri, ci, tril, strict, eye, lastsel, rowseq = masks

    pre = sm_ref[...].reshape(RT, LANES) + par_ref[0:1, :]
    logf_all = -_softplus(-pre)
    b_all = _mask_dot_exact(jnp.where(tril, 1.0, 0.0).astype(BF16), logf_all)
    yield

    qa = q_ref[...].reshape(RT, H * DK)
    ka = k_ref[...].reshape(RT, H * DK)
    va = v_ref[...].reshape(RT, H * DV)
    oa = og_ref[...].reshape(RT, H * DV)

    hs = range(H)
    q = [qa[:, h * DK:(h + 1) * DK] * (DK ** -0.5) for h in hs]
    k = [ka[:, h * DK:(h + 1) * DK] for h in hs]
    v = [va[:, h * DV:(h + 1) * DV] for h in hs]
    bc = [b_all[:, off_f + h:off_f + h + 1] for h in hs]
    ic = [pre[:, off_i + h:off_i + h + 1] for h in hs]
    dm = [jnp.where(tril, b + _row_form(i - b, eye), -jnp.inf) for b, i in zip(bc, ic)]
    yield

    if R == 1:
        m_rows = [m_ref[0, :, h:h + 1] for h in hs]
    else:
        m_rows = []
        for h in hs:
            mr = jnp.zeros((RT, 1), F32)
            for s in range(R):
                mr = jnp.where(rowseq == s, m_ref[s, :, h:h + 1], mr)
            m_rows.append(mr)
    m_new = [jnp.maximum(b + m, jnp.max(d, axis=1, keepdims=True)) for b, m, d in zip(bc, m_rows, dm)]
    inter = [jnp.exp(b + m - mn) for b, m, mn in zip(bc, m_rows, m_new)]
    yield
    smat = [_dot_nt(a, b) * jnp.exp(d - mn) for a, b, d, mn in zip(q, k, dm, m_new)]
    yield

    if R == 1:
        qc = [_dot(q[h], c_ref[0, h]) for h in hs]
        qn = [jnp.sum(q[h] * n_ref[0, h:h + 1, :], axis=1, keepdims=True) for h in hs]
    else:
        qc, qn = [], []
        for h in hs:
            rows = [slice(s * T, (s + 1) * T) for s in range(R)]
            qc.append(jnp.concatenate([_dot(q[h][r], c_ref[s, h]) for s, r in enumerate(rows)], axis=0))
            qn.append(jnp.concatenate(
                [jnp.sum(q[h][r] * n_ref[s, h:h + 1, :], axis=1, keepdims=True)
                 for s, r in enumerate(rows)], axis=0))
            yield
    yield
    num = [i * c + _dot(s, a) for i, c, s, a in zip(inter, qc, smat, v)]
    den = [i * n + jnp.sum(s, axis=1, keepdims=True) for i, n, s in zip(inter, qn, smat)]
    yield
    hh = [a / jnp.maximum(jnp.abs(b), jnp.exp(-mn)) for a, b, mn in zip(num, den, m_new)]

    b_last = [_at_last(_row_form(b, eye), lastsel) for b in bc]
    m_end = [_at_last(_row_form(mn, eye), lastsel) for mn in m_new]
    yield
    kw = [a * jnp.exp(bl - b + i - me) for a, bl, b, i, me in zip(k, b_last, bc, ic, m_end)]
    cs = [jnp.exp(bl + m - me) for bl, m, me in zip(b_last, m_rows, m_end)]
    yield
    for h in hs:
        for s in range(R):
            kw_s = kw[h] if R == 1 else jnp.where(rowseq == s, kw[h], 0.0)
            cs_s = cs[h][s * T:s * T + 1, :]
            c_ref[s, h] = cs_s * c_ref[s, h] + _dot_tn(kw_s, v[h])
            n_ref[s, h:h + 1, :] = cs_s * n_ref[s, h:h + 1, :] + jnp.sum(kw_s, axis=0, keepdims=True)
            m_ref[s, :, h:h + 1] = m_end[h][s * T:s * T + 1, :]
        yield

    for h in hs:
        og = oa[:, h * DV:(h + 1) * DV]
        out = _rms(hh[h], ng_ref[:, h * DV:(h + 1) * DV]) * _sigmoid(og)
        o_ref[:, h * DV:(h + 1) * DV] = out.astype(o_ref.dtype)
    yield


def _cast_steps(pairs):
    for src_ref, dst_ref in pairs:
        dst_ref[...] = src_ref[...].astype(BF16)
        yield


def _mixers_kernel(*refs, n_cast, R, T, Hd, Hm, DK, DV, MDK, MDV, width, off_i, off_f):
    (qkv_ref, z_ref, mq_ref, mk_ref, mv_ref, mo_ref, sm_ref, cst_ref, s0_ref, c0_ref, n0_ref, m0_ref,
     cw_ref, par_ref, dng_ref, mng_ref) = refs[:16]
    cast_in = refs[16:16 + n_cast]
    odn_ref, s_ref, oml_ref, c_ref, n_ref, m_ref = refs[16 + n_cast:22 + n_cast]
    cast_out = refs[22 + n_cast:22 + 2 * n_cast]
    ext_ref, = refs[22 + 2 * n_cast:]

    @pl.when(pl.program_id(1) == 0)
    def _():
        ext_ref[:, SUBLANES - (width - 1):SUBLANES, :] = cst_ref[...]
        s_ref[...] = s0_ref[...]
        c_ref[...] = c0_ref[...]
        n_ref[...] = n0_ref[...]
        m_ref[...] = m0_ref[...]

    masks = _seq_masks(R, T)
    streams = [
        _gdn_steps(qkv_ref, z_ref, sm_ref, cw_ref, par_ref, dng_ref, odn_ref, s_ref, ext_ref, masks,
                   R=R, T=T, H=Hd, DK=DK, DV=DV, width=width),
        _mlstm_steps(mq_ref, mk_ref, mv_ref, mo_ref, sm_ref, par_ref, mng_ref, oml_ref, c_ref, n_ref,
                     m_ref, masks, R=R, T=T, H=Hm, DK=MDK, DV=MDV, off_i=off_i, off_f=off_f),
        _cast_steps(list(zip(cast_in, cast_out))),
    ]
    turns = [2, 1, 1]
    while streams:
        for gen, n in list(zip(streams, turns)):
            for _ in range(n):
                if gen in streams and next(gen, "done") == "done":
                    turns.pop(streams.index(gen))
                    streams.remove(gen)


def _cast_slab_rows(k, steps):
    return next((r for r in range(16, k + 1, 16) if k % r == 0 and k // r <= steps), None)


def _mixers(proj3, small3, conv_state, s0, c0, n0, m0, conv_w, par, dn_norm_g, ml_norm_g,
            *, R, T, N, off_i, off_f, casts=()):
    B, Hd, DK, DV = s0.shape
    _, Hm, MDK, MDV = c0.shape
    QKV = conv_state.shape[-1]
    VW, MQK, MVW = Hd * DV, Hm * MDK, Hm * MDV
    width = conv_w.shape[0]
    col0 = QKV + VW
    assert QKV == 2 * Hd * DK + VW and QKV % VW == 0 and VW % LANES == 0
    assert col0 % MQK == 0 and (col0 + 2 * MQK) % MVW == 0 and MQK % LANES == 0
    rows = proj3.shape[0] * T
    qb, vb = col0 // MQK, (col0 + 2 * MQK) // MVW
    kern = functools.partial(_mixers_kernel, n_cast=len(casts), R=R, T=T, Hd=Hd, Hm=Hm, DK=DK, DV=DV,
                             MDK=MDK, MDV=MDV, width=width, off_i=off_i, off_f=off_f)
    tok = lambda w, c: pl.BlockSpec((R, T, w), lambda b, n: (b * N + n, 0, c))
    const = lambda shape: pl.BlockSpec(shape, lambda b, n: (0,) * len(shape))
    per_seq = lambda *dims: pl.BlockSpec((R,) + dims, lambda b, n: (b,) + (0,) * len(dims))
    state_specs = [per_seq(Hd, DK, DV), per_seq(Hm, MDK, MDV), per_seq(Hm, MDK), per_seq(1, Hm)]

    def slab(w):
        r = _cast_slab_rows(w.shape[0], (B // R) * N)
        return pl.BlockSpec((r, w.shape[1]), lambda b, n: (jnp.minimum(b * N + n, w.shape[0] // r - 1), 0))

    cast_specs = [slab(w) for w in casts]
    odn, s_new, oml, c_new, n_new, m_new, *cast = pl.pallas_call(
        kern,
        grid=(B // R, N),
        in_specs=[
            tok(QKV, 0), tok(VW, QKV // VW), tok(MQK, qb), tok(MQK, qb + 1), tok(MVW, vb),
            tok(MVW, vb + 1), tok(LANES, 0),
            per_seq(width - 1, QKV), *state_specs,
            const((width, QKV)), const((SUBLANES, LANES)), const((1, DV)), const((1, MVW)),
            *cast_specs,
        ],
        out_specs=[pl.BlockSpec((R * T, VW), lambda b, n: (b * N + n, 0)), state_specs[0],
                   pl.BlockSpec((R * T, MVW), lambda b, n: (b * N + n, 0)), *state_specs[1:],
                   *cast_specs],
        out_shape=[jax.ShapeDtypeStruct((rows, VW), BF16), jax.ShapeDtypeStruct(s0.shape, F32),
                   jax.ShapeDtypeStruct((rows, MVW), BF16), jax.ShapeDtypeStruct(c0.shape, F32),
                   jax.ShapeDtypeStruct(n0.shape, F32), jax.ShapeDtypeStruct(m0.shape, F32),
                   *[jax.ShapeDtypeStruct(w.shape, BF16) for w in casts]],
        scratch_shapes=[pltpu.VMEM((R, SUBLANES + T, QKV), F32)],
        compiler_params=_cparams(("arbitrary", "arbitrary")),
        name="mixers",
    )(proj3, proj3, proj3, proj3, proj3, proj3, small3, conv_state, s0, c0, n0, m0, conv_w, par,
      dn_norm_g.reshape(1, DV), ml_norm_g.reshape(1, MVW), *casts)
    return odn, s_new, oml, c_new, n_new, m_new, cast


def _outproj_kernel(x_ref, a_ref, b_ref, wa_ref, wb_ref, o_ref):
    o_ref[...] = (x_ref[...] + jnp.dot(a_ref[...], wa_ref[...], preferred_element_type=F32)
                  + jnp.dot(b_ref[...], wb_ref[...], preferred_element_type=F32))


def _outproj(x, a, b, w_out):
    m, d = x.shape
    ka, kb = a.shape[1], b.shape[1]
    assert ka == kb
    tm = _pick(m, (512, 256, 128, 64, 32, 16, 8))
    return pl.pallas_call(
        _outproj_kernel,
        grid=(m // tm,),
        in_specs=[
            pl.BlockSpec((tm, d), lambda i: (i, 0)),
            pl.BlockSpec((tm, ka), lambda i: (i, 0)),
            pl.BlockSpec((tm, kb), lambda i: (i, 0)),
            pl.BlockSpec((ka, d), lambda i: (0, 0), pipeline_mode=pl.Buffered(1)),
            pl.BlockSpec((kb, d), lambda i: (1, 0), pipeline_mode=pl.Buffered(1)),
        ],
        out_specs=pl.BlockSpec((tm, d), lambda i: (i, 0)),
        out_shape=jax.ShapeDtypeStruct((m, d), F32),
        compiler_params=_cparams(("parallel",)),
        name="outproj",
    )(x, a, b, w_out, w_out)


def _ffn_kernel(x_ref, g_ref, wg_ref, wu_ref, cw_ref, cb_ref, wd_ref, stg_ref, stu_ref, gf_ref,
                y_ref, nsg_ref, nsu_ref, h_ref, act_ref, ext_ref, *carry,
                tm, tf, ff, shift, pad, tps, width, nj, wide_state, final_norm):
    i = pl.program_id(0)
    j = pl.program_id(1)
    hist = (width - 1) * shift

    def activate_tile():
        conv = []
        for idx, (w_ref, st_ref, ns_ref) in enumerate(
                ((wg_ref, stg_ref, nsg_ref), (wu_ref, stu_ref, nsu_ref))):
            col = pl.ds(pl.multiple_of(idx * ff + j * tf, LANES), tf)
            st_col = col if wide_state else slice(None)
            u = jnp.dot(h_ref[...], w_ref[...], preferred_element_type=F32)
            ext_ref[idx, pad:pad + tm, :] = u
            if tps == 1:
                ext_ref[idx, pad - hist:pad, :] = st_ref[0, :, st_col]
            else:
                ext_ref[idx, pad - hist:pad, :] = jnp.where(i % tps == 0, st_ref[0, :, st_col],
                                                            carry[0][j, idx])
            cw = cw_ref[:, col]
            c = u * cw[width - 1:width, :] + cb_ref[:, col]
            for t in range(width - 1):
                off = pad - hist + t * shift
                c = c + ext_ref[idx, off:off + tm, :] * cw[t:t + 1, :]
            new = u[tm - hist:, :]
            ns_ref[0, :, st_col] = new
            if tps != 1:
                carry[0][j, idx] = new
            conv.append(c)
        act_ref[...] = (conv[0] * _sigmoid(conv[0]) * conv[1]).astype(BF16)

    def down_tile():
        return jnp.dot(act_ref[...], wd_ref[...], preferred_element_type=F32)

    @pl.when(j == 0)
    def _():
        h_ref[...] = _rms(x_ref[...], g_ref[...]).astype(BF16)
        y_ref[...] = jnp.zeros_like(y_ref)
        activate_tile()

    @pl.when((j > 0) & (j < nj))
    def _():
        y_ref[...] += down_tile()
        activate_tile()

    @pl.when(j == nj)
    def _():
        xo = x_ref[...] + y_ref[...] + down_tile()
        y_ref[...] = _rms(xo, gf_ref[...]) if final_norm else xo


def _ffn(x, norm_g, w_up, conv_w, conv_b, w_down, state, final_g, *, tm, shift, tps, final_norm):
    m, d = x.shape
    ff = w_down.shape[0]
    width = conv_w.shape[0]
    hist = (width - 1) * shift
    pad = -(-hist // SUBLANES) * SUBLANES
    tf = _pick(ff, (512, 256, 128))
    nj = ff // tf
    n_tiles = m // tm
    wide_state = hist * 2 * ff * 4 <= WIDE_STATE_BYTES
    kern = functools.partial(_ffn_kernel, tm=tm, tf=tf, ff=ff, shift=shift, pad=pad, tps=tps,
                             width=width, nj=nj, wide_state=wide_state, final_norm=final_norm)
    scratch = [pltpu.VMEM((tm, d), BF16), pltpu.VMEM((tm, tf), BF16), pltpu.VMEM((2, pad + tm, tf), F32)]
    cur = lambda j: jnp.minimum(j, nj - 1)
    prv = lambda j: jnp.maximum(j - 1, 0)
    if tps != 1:
        scratch.append(pltpu.VMEM((nj, 2, hist, tf), F32))
    if wide_state:
        st_in = [pl.BlockSpec((1, hist, 2 * ff), lambda i, j: (i // tps, 0, 0))] * 2
        st_out = [pl.BlockSpec((1, hist, 2 * ff), lambda i, j: (i, 0, 0))]
        st_shape = [jax.ShapeDtypeStruct((n_tiles, hist, 2 * ff), F32)]
    else:
        st_in = [pl.BlockSpec((1, hist, tf), lambda i, j: (i // tps, 0, cur(j))),
                 pl.BlockSpec((1, hist, tf), lambda i, j: (i // tps, 0, nj + cur(j)))]
        st_out = [pl.BlockSpec((1, hist, tf), lambda i, j: (i, 0, cur(j)))] * 2
        st_shape = [jax.ShapeDtypeStruct((n_tiles, hist, ff), F32)] * 2
    call = pl.pallas_call(
        kern if not wide_state else _single_history(kern),
        grid=(n_tiles, nj + 1),
        in_specs=[
            pl.BlockSpec((tm, d), lambda i, j: (i, 0), pipeline_mode=pl.Buffered(1)),
            pl.BlockSpec((1, d), lambda i, j: (0, 0)),
            pl.BlockSpec((d, tf), lambda i, j: (0, cur(j))),
            pl.BlockSpec((d, tf), lambda i, j: (0, nj + cur(j))),
            pl.BlockSpec((width, 2 * ff), lambda i, j: (0, 0)),
            pl.BlockSpec((1, 2 * ff), lambda i, j: (0, 0)),
            pl.BlockSpec((tf, d), lambda i, j: (prv(j), 0)),
            *st_in,
            pl.BlockSpec((1, d), lambda i, j: (0, 0)),
        ],
        out_specs=[pl.BlockSpec((tm, d), lambda i, j: (i, 0)), *st_out],
        out_shape=[jax.ShapeDtypeStruct((m, d), F32), *st_shape],
        scratch_shapes=scratch,
        compiler_params=_cparams(("arbitrary", "arbitrary")),
        name="convffn",
    )
    y, *ns = call(x, norm_g.reshape(1, d), w_up, w_up, conv_w, conv_b.reshape(1, 2 * ff), w_down,
                  state, state, final_g.reshape(1, d))
    new = ns[0] if wide_state else jnp.concatenate(ns, axis=-1)
    return y, new[tps - 1::tps]


def _single_history(kern):
    def wrapped(*refs):
        ins, (y_ref, ns_ref), scratch = refs[:10], refs[10:12], refs[12:]
        return kern(*ins, y_ref, ns_ref, ns_ref, *scratch)
    return wrapped


def _layer(x, st, p, final_g, final_norm, decode):
    conv_buf, s0, c0, n0, m0, ffn_buf = st
    B, L, D = x.shape
    Hd, Hm = s0.shape[1], c0.shape[1]
    dn_qkv = conv_buf.shape[-1]
    dn_vw = Hd * s0.shape[3]
    ff = p["w_down"].shape[0]

    T = next((c for c in (MIX_CHUNK, CHUNK) if L % c == 0), L)
    N = L // T
    R = _pick(B, (8, 4, 2, 1)) if decode else 1
    if R * T > 128:
        R = 1

    x2 = x.reshape(B * L, D)
    big, small = _inproj(x2, p["norm_mix_g"], p["w_big"], p["w_small"])
    big3 = big.reshape(B * N, T, big.shape[1])
    small3 = small.reshape(B * N, T, LANES)

    pending = [k for k in ("w_out", "w_up", "w_down") if p[k].dtype != BF16]
    if not all(_cast_slab_rows(p[k].shape[0], (B // R) * N) for k in pending):
        p = dict(p, **{k: p[k].astype(BF16) for k in pending})
        pending = []
    o_dn, s_new, o_ml, c_new, n_new, m_new, rounded = _mixers(
        big3, small3, conv_buf, s0, c0, n0, m0.reshape(B, 1, Hm), p["dn_conv_w"], p["par"],
        p["dn_norm_g"], p["ml_norm_g"], R=R, T=T, N=N, off_i=2 * Hd, off_f=2 * Hd + Hm,
        casts=tuple(p[k] for k in pending))
    p = dict(p, **dict(zip(pending, rounded)))
    width = p["dn_conv_w"].shape[0]
    big4 = big.reshape(B, L, big.shape[1])
    if L >= width - 1:
        conv_new = big4[:, L - (width - 1):, :dn_qkv]
    else:
        conv_new = jnp.concatenate([conv_buf, big4[:, :, :dn_qkv]], axis=1)[:, L:]

    x1 = _outproj(x2, o_dn, o_ml, p["w_out"])

    fw = p["ffn_conv_w"].shape[0]
    if decode:
        x1t = x1.reshape(B, L, D).transpose(1, 0, 2).reshape(L * B, D)
        state = ffn_buf.transpose(1, 0, 2).reshape(1, (fw - 1) * B, 2 * ff)
        y, new = _ffn(x1t, p["norm_ffn_g"], p["w_up"], p["ffn_conv_w"], p["ffn_conv_b"],
                      p["w_down"], state, final_g, tm=L * B, shift=B, tps=1, final_norm=final_norm)
        y = y.reshape(L, B, D).transpose(1, 0, 2)
        ffn_new = new.reshape(fw - 1, B, 2 * ff).transpose(1, 0, 2)
    else:
        tm = _pick(L, (1024, 512, 256, 128, 64, 32, 16, 8))
        y, ffn_new = _ffn(x1, p["norm_ffn_g"], p["w_up"], p["ffn_conv_w"], p["ffn_conv_b"],
                          p["w_down"], ffn_buf, final_g, tm=tm, shift=1, tps=L // tm,
                          final_norm=final_norm)
        y = y.reshape(B, L, D)
    return y, (conv_new, s_new, c_new, n_new, m_new.reshape(B, Hm), ffn_new), p


def _regroup_kernel(w_ref, big_ref, small_ref, *, c0, c1, c2):
    n = w_ref.shape[0]
    big_ref[:c0, :] = w_ref[:c0, :].astype(BF16)
    big_ref[c0:, :] = w_ref[c1:c2, :].astype(BF16)
    narrow = jnp.concatenate(
        [w_ref[c0:c1, :], w_ref[c2:, :], jnp.zeros((LANES - (c1 - c0) - (n - c2), w_ref.shape[1]), F32)],
        axis=0)
    small_ref[...] = narrow.astype(BF16)


def _regroup(w_t, c0, c1, c2):
    n, k = w_t.shape
    tk = _pick(k, (256, 128))
    nb = c0 + c2 - c1
    assert c0 % 16 == 0 and c1 % SUBLANES == 0 and c2 % SUBLANES == 0
    return pl.pallas_call(
        functools.partial(_regroup_kernel, c0=c0, c1=c1, c2=c2),
        grid=(k // tk,),
        in_specs=[pl.BlockSpec((n, tk), lambda i: (0, i))],
        out_specs=[pl.BlockSpec((nb, tk), lambda i: (0, i)), pl.BlockSpec((LANES, tk), lambda i: (0, i))],
        out_shape=[jax.ShapeDtypeStruct((nb, k), BF16), jax.ShapeDtypeStruct((LANES, k), BF16)],
        compiler_params=_cparams(("parallel",)),
        name="regroup",
    )(w_t)


def _prep_params(l, Hd, Hm, dn_qkv, dn_vw, ml_qk, ml_vw, norm_mix_g, w_in, dn_conv_w, dn_A_log,
                 dn_dt_bias, dn_norm_g, ml_i_bias, ml_f_bias, ml_norm_g, w_out, norm_ffn_g, w_up,
                 ffn_conv_w, ffn_conv_b, w_down):
    w = w_in[l]
    c0 = dn_qkv + dn_vw
    c1 = c0 + 2 * Hd
    c2 = c1 + 2 * ml_qk + 2 * ml_vw
    nsmall = 2 * Hd + 2 * Hm
    assert nsmall <= LANES and w.shape[1] == c2 + 2 * Hm
    w_big, w_small = _regroup(w.T, c0, c1, c2)
    z = lambda n: jnp.zeros((n,), F32)
    bias = jnp.concatenate([z(Hd), dn_dt_bias[l], ml_i_bias[l], ml_f_bias[l], z(LANES - nsmall)])
    alog = jnp.concatenate([z(Hd), dn_A_log[l], z(LANES - 2 * Hd)])
    par = jnp.zeros((SUBLANES, LANES), F32).at[0].set(bias).at[1].set(alog)
    return dict(norm_mix_g=norm_mix_g[l], w_big=w_big, w_small=w_small, dn_conv_w=dn_conv_w[l],
                par=par, dn_norm_g=dn_norm_g[l], ml_norm_g=ml_norm_g[l], w_out=w_out[l],
                norm_ffn_g=norm_ffn_g[l], w_up=w_up[l], ffn_conv_w=ffn_conv_w[l],
                ffn_conv_b=ffn_conv_b[l], w_down=w_down[l])


def kernel(x_prompt, x_sample, state_dn_conv, state_dn_S, state_ml_C, state_ml_n, state_ml_m,
           state_ffn_conv, norm_mix_g, w_in, dn_conv_w, dn_A_log, dn_dt_bias, dn_norm_g,
           ml_i_bias, ml_f_bias, ml_norm_g, w_out, norm_ffn_g, w_up, ffn_conv_w, ffn_conv_b,
           w_down, norm_final_g):
    states = (state_dn_conv, state_dn_S, state_ml_C, state_ml_n, state_ml_m, state_ffn_conv)
    depth = w_in.shape[0]
    batch = x_prompt.shape[0]
    Hd, dk, dv = state_dn_S.shape[2:]
    Hm, mdk, mdv = state_ml_C.shape[2:]
    xp, xs = x_prompt, x_sample
    p_new = [[] for _ in states]
    s_new = [[] for _ in states]
    for l in range(depth):
        p = _prep_params(l, Hd, Hm, state_dn_conv.shape[-1], Hd * dv, Hm * mdk, Hm * mdv,
                         norm_mix_g, w_in, dn_conv_w, dn_A_log, dn_dt_bias, dn_norm_g, ml_i_bias,
                         ml_f_bias, ml_norm_g, w_out, norm_ffn_g, w_up, ffn_conv_w, ffn_conv_b,
                         w_down)
        last = l == depth - 1
        st_p = tuple(jnp.zeros((batch,) + s.shape[2:], s.dtype) for s in states)
        st_s = tuple(s[l] for s in states)
        xp, np_st, p = _layer(xp, st_p, p, norm_final_g, last, decode=False)
        xs, ns_st, p = _layer(xs, st_s, p, norm_final_g, last, decode=True)
        for i in range(len(states)):
            p_new[i].append(np_st[i])
            s_new[i].append(ns_st[i])
    outs_p = tuple(jnp.stack(a, axis=0) for a in p_new)
    outs_s = tuple(jnp.stack(a, axis=0) for a in s_new)
    return (xp, xs) + outs_p + outs_s
```

```python
import functools

import jax
import jax.numpy as jnp
from jax import lax
from jax.experimental import pallas as pl
from jax.experimental.pallas import tpu as pltpu

EPS = 1e-6
CHUNK = 64
MIX_CHUNK = 128
F32 = jnp.float32
BF16 = jnp.bfloat16
LANES = 128
SUBLANES = 8
VMEM_LIMIT = 56 * 1024 * 1024
WIDE_STATE_BYTES = 2 * 1024 * 1024


def _cparams(sem, vmem=VMEM_LIMIT, flags=None):
    return pltpu.CompilerParams(dimension_semantics=sem, vmem_limit_bytes=vmem, flags=flags)


def _dot(a, b):
    return jnp.dot(a.astype(BF16), b.astype(BF16), preferred_element_type=F32)


def _dot_nt(a, b):
    return lax.dot_general(a.astype(BF16), b.astype(BF16), (((1,), (1,)), ((), ())),
                           preferred_element_type=F32)


def _dot_tn(a, b):
    return lax.dot_general(a.astype(BF16), b.astype(BF16), (((0,), (0,)), ((), ())),
                           preferred_element_type=F32)


def _mask_dot_exact(mask_bf16, x):
    hi = x.astype(BF16)
    r1 = x - hi.astype(F32)
    mid = r1.astype(BF16)
    lo = (r1 - mid.astype(F32)).astype(BF16)
    d = functools.partial(jnp.dot, preferred_element_type=F32)
    return (d(mask_bf16, hi) + d(mask_bf16, mid)) + d(mask_bf16, lo)


def _softplus(x):
    return jnp.maximum(x, 0.0) + jnp.log1p(jnp.exp(-jnp.abs(x)))


def _sigmoid(x):
    return 1.0 / (1.0 + jnp.exp(-x))


def _rms(x, g):
    return x * lax.rsqrt(jnp.mean(x * x, axis=-1, keepdims=True) + EPS) * g


def _log2(n):
    assert n & (n - 1) == 0 and n > 0, n
    return n.bit_length() - 1


def _pick(n, cands):
    for c in cands:
        if n % c == 0:
            return c
    return n


def _inproj_kernel(x_ref, g_ref, wb_ref, ws_ref, ob_ref, os_ref, h_ref):
    nt = functools.partial(lax.dot_general, dimension_numbers=(((1,), (1,)), ((), ())),
                           preferred_element_type=F32)

    @pl.when(pl.program_id(1) == 0)
    def _():
        h = _rms(x_ref[...], g_ref[...]).astype(BF16)
        h_ref[...] = h
        os_ref[...] = nt(h, ws_ref[...])

    ob_ref[...] = nt(h_ref[...], wb_ref[...])


def _inproj(x, g, w_big_t, w_small_t):
    m, d = x.shape
    nb = w_big_t.shape[0]
    tm = _pick(m, (1024, 512, 256, 128, 64, 32, 16, 8))
    tn = _pick(nb, (1024, 512, 256, 128))
    return pl.pallas_call(
        _inproj_kernel,
        grid=(m // tm, nb // tn),
        in_specs=[
            pl.BlockSpec((tm, d), lambda i, j: (i, 0)),
            pl.BlockSpec((1, d), lambda i, j: (0, 0)),
            pl.BlockSpec((tn, d), lambda i, j: (j, 0)),
            pl.BlockSpec((LANES, d), lambda i, j: (0, 0)),
        ],
        out_specs=[
            pl.BlockSpec((tm, tn), lambda i, j: (i, j)),
            pl.BlockSpec((tm, LANES), lambda i, j: (i, 0)),
        ],
        out_shape=[jax.ShapeDtypeStruct((m, nb), F32), jax.ShapeDtypeStruct((m, LANES), F32)],
        scratch_shapes=[pltpu.VMEM((tm, d), BF16)],
        compiler_params=_cparams(("parallel", "arbitrary")),
        name="inproj",
    )(x, g.reshape(1, d), w_big_t, w_small_t)


def _seq_masks(R, T):
    RT = R * T
    sh = _log2(T)
    ri = lax.broadcasted_iota(jnp.int32, (RT, RT), 0)
    ci = lax.broadcasted_iota(jnp.int32, (RT, RT), 1)
    same = (ri >> sh) == (ci >> sh)
    tril = same & (ci <= ri)
    strict = same & (ci < ri)
    eye = ri == ci
    lastsel = same & ((ci & (T - 1)) == T - 1)
    rowseq = lax.broadcasted_iota(jnp.int32, (RT, 1), 0) >> sh
    return ri, ci, tril, strict, eye, lastsel, rowseq


def _row_form(col, eye):
    return jnp.sum(jnp.where(eye, col, 0.0), axis=0, keepdims=True)


def _at_last(row, lastsel):
    return jnp.sum(jnp.where(lastsel, row, 0.0), axis=1, keepdims=True)


def _unit_lower_inverses(lms, ri, ci, T):
    ds = None
    s = 1
    while s < T:
        sh = _log2(2 * s)
        blk = ((ri >> sh) == (ci >> sh)) & ((ri & (2 * s - 1)) >= s) & ((ci & (2 * s - 1)) < s)
        ms = [jnp.where(blk, lm, 0.0) for lm in lms]
        if ds is None:
            eye = jnp.where(ri == ci, 1.0, 0.0)
            ds = [eye - m for m in ms]
        else:
            dms = [_dot(d, m) for d, m in zip(ds, ms)]
            yield
            ds = [d - _dot(dm, d) for d, dm in zip(ds, dms)]
        yield
        s *= 2
    return ds


def _causal_conv(ext_ref, x_ref, cw_ref, T, width, cols):
    lo = SUBLANES - (width - 1)
    ext_ref[:, SUBLANES:SUBLANES + T, cols] = x_ref[:, :, cols]
    cw = cw_ref[:, cols]
    acc = None
    for j in range(width):
        term = ext_ref[:, lo + j:lo + j + T, cols] * cw[j:j + 1, :]
        acc = term if acc is None else acc + term
    ext_ref[:, lo:SUBLANES, cols] = ext_ref[:, lo + T:SUBLANES + T, cols]
    return acc


def _gdn_steps(qkv_ref, z_ref, sm_ref, cw_ref, par_ref, ng_ref, o_ref, s_ref, ext_ref, masks,
               *, R, T, H, DK, DV, width):
    RT = R * T
    QK = H * DK

    groups = []
    for c0, c1 in ((0, QK), (QK, 2 * QK), (2 * QK, 2 * QK + H * DV)):
        a = _causal_conv(ext_ref, qkv_ref, cw_ref, T, width, slice(c0, c1)).reshape(RT, c1 - c0)
        groups.append(a * _sigmoid(a))
        if R > 1:
            yield
    yq, yk, yv = groups
    if R == 1:
        yield
    z = z_ref[...].reshape(RT, H * DV)

    sm = sm_ref[...].reshape(RT, LANES)
    beta_all = _sigmoid(sm)
    g_all = -jnp.exp(par_ref[1:2, :]) * _softplus(sm + par_ref[0:1, :])

    ri, ci, tril, strict, eye, lastsel, rowseq = masks
    g_cum = _mask_dot_exact(jnp.where(tril, 1.0, 0.0).astype(BF16), g_all)
    yield

    hs = range(H)
    q = [yq[:, h * DK:(h + 1) * DK] for h in hs]
    k = [yk[:, h * DK:(h + 1) * DK] for h in hs]
    v = [yv[:, h * DV:(h + 1) * DV] for h in hs]
    q = [a * lax.rsqrt(jnp.sum(a * a, axis=-1, keepdims=True) + EPS) * (DK ** -0.5) for a in q]
    yield
    k = [a * lax.rsqrt(jnp.sum(a * a, axis=-1, keepdims=True) + EPS) for a in k]
    yield
    beta = [beta_all[:, h:h + 1] for h in hs]
    gc = [g_cum[:, H + h:H + h + 1] for h in hs]
    gr = [_row_form(a, eye) for a in gc]
    glast = [_at_last(a, lastsel) for a in gr]
    yield
    decay = [jnp.exp(jnp.where(tril, c - r, -jnp.inf)) for c, r in zip(gc, gr)]
    kb = [a * b for a, b in zip(k, beta)]
    yield
    lm = [jnp.where(strict, _dot_nt(a, b) * d, 0.0) for a, b, d in zip(kb, k, decay)]
    yield
    tinv = yield from _unit_lower_inverses(lm, ri, ci, T)
    eg = [jnp.exp(a) for a in gc]
    uw = [_dot(t, jnp.concatenate([a * b, c * e], axis=1))
          for t, a, b, c, e in zip(tinv, v, beta, kb, eg)]
    yield
    u = [a[:, :DV] for a in uw]
    w = [a[:, DV:] for a in uw]
    qk = [_dot_nt(a, b) * d for a, b, d in zip(q, k, decay)]
    yield
    qe = [a * e for a, e in zip(q, eg)]
    kd = [a * jnp.exp(l - c) for a, l, c in zip(k, glast, gc)]
    gl = [jnp.exp(a) for a in glast]
    yield

    if R == 1:
        s_old = [s_ref[0, h] for h in hs]
        ws = [_dot(jnp.concatenate([a, b], axis=0), c) for a, b, c in zip(w, qe, s_old)]
        yield
        v_new = [a - b[:RT] for a, b in zip(u, ws)]
        o = [a[RT:] + _dot(b, c) for a, b, c in zip(ws, qk, v_new)]
        yield
        for h in hs:
            s_ref[0, h] = s_old[h] * gl[h][0:1, :] + _dot_tn(kd[h], v_new[h])
        yield
    else:
        ws = [[_dot(jnp.concatenate([w[h][s * T:(s + 1) * T], qe[h][s * T:(s + 1) * T]], axis=0),
                    s_ref[s, h]) for s in range(R)] for h in hs]
        yield
        v_new = [u[h] - jnp.concatenate([a[:T] for a in ws[h]], axis=0) for h in hs]
        o = [jnp.concatenate([a[T:] for a in ws[h]], axis=0) + _dot(qk[h], v_new[h]) for h in hs]
        yield
        for h in hs:
            for s in range(R):
                kd_s = jnp.where(rowseq == s, kd[h], 0.0)
                s_ref[s, h] = s_ref[s, h] * gl[h][s * T:s * T + 1, :] + _dot_tn(kd_s, v_new[h])
            yield

    for h in hs:
        zh = z[:, h * DV:(h + 1) * DV]
        out = _rms(o[h], ng_ref[...]) * (zh * _sigmoid(zh))
        o_ref[:, h * DV:(h + 1) * DV] = out.astype(o_ref.dtype)
    yield


def _mlstm_steps(q_ref, k_ref, v_ref, og_ref, sm_ref, par_ref, ng_ref, o_ref, c_ref, n_ref, m_ref,
                 masks, *, R, T, H, DK, DV, off_i, off_f):
    RT = R * T
    ri, ci, tril, strict, eye, lastsel, rowseq = masks

    pre = sm_ref[...].reshape(RT, LANES) + par_ref[0:1, :]
    logf_all = -_softplus(-pre)
    b_all = _mask_dot_exact(jnp.where(tril, 1.0, 0.0).astype(BF16), logf_all)
    yield

    qa = q_ref[...].reshape(RT, H * DK)
    ka = k_ref[...].reshape(RT, H * DK)
    va = v_ref[...].reshape(RT, H * DV)
    oa = og_ref[...].reshape(RT, H * DV)

    hs = range(H)
    q = [qa[:, h * DK:(h + 1) * DK] * (DK ** -0.5) for h in hs]
    k = [ka[:, h * DK:(h + 1) * DK] for h in hs]
    v = [va[:, h * DV:(h + 1) * DV] for h in hs]
    bc = [b_all[:, off_f + h:off_f + h + 1] for h in hs]
    ic = [pre[:, off_i + h:off_i + h + 1] for h in hs]
    dm = [jnp.where(tril, b + _row_form(i - b, eye), -jnp.inf) for b, i in zip(bc, ic)]
    yield

    if R == 1:
        m_rows = [m_ref[0, :, h:h + 1] for h in hs]
    else:
        m_rows = []
        for h in hs:
            mr = jnp.zeros((RT, 1), F32)
            for s in range(R):
                mr = jnp.where(rowseq == s, m_ref[s, :, h:h + 1], mr)
            m_rows.append(mr)
    m_new = [jnp.maximum(b + m, jnp.max(d, axis=1, keepdims=True)) for b, m, d in zip(bc, m_rows, dm)]
    inter = [jnp.exp(b + m - mn) for b, m, mn in zip(bc, m_rows, m_new)]
    yield
    smat = [_dot_nt(a, b) * jnp.exp(d - mn) for a, b, d, mn in zip(q, k, dm, m_new)]
    yield

    if R == 1:
        qc = [_dot(q[h], c_ref[0, h]) for h in hs]
        qn = [jnp.sum(q[h] * n_ref[0, h:h + 1, :], axis=1, keepdims=True) for h in hs]
    else:
        qc, qn = [], []
        for h in hs:
            rows = [slice(s * T, (s + 1) * T) for s in range(R)]
            qc.append(jnp.concatenate([_dot(q[h][r], c_ref[s, h]) for s, r in enumerate(rows)], axis=0))
            qn.append(jnp.concatenate(
                [jnp.sum(q[h][r] * n_ref[s, h:h + 1, :], axis=1, keepdims=True)
                 for s, r in enumerate(rows)], axis=0))
            yield
    yield
    num = [i * c + _dot(s, a) for i, c, s, a in zip(inter, qc, smat, v)]
    den = [i * n + jnp.sum(s, axis=1, keepdims=True) for i, n, s in zip(inter, qn, smat)]
    yield
    hh = [a / jnp.maximum(jnp.abs(b), jnp.exp(-mn)) for a, b, mn in zip(num, den, m_new)]

    b_last = [_at_last(_row_form(b, eye), lastsel) for b in bc]
    m_end = [_at_last(_row_form(mn, eye), lastsel) for mn in m_new]
    yield
    kw = [a * jnp.exp(bl - b + i - me) for a, bl, b, i, me in zip(k, b_last, bc, ic, m_end)]
    cs = [jnp.exp(bl + m - me) for bl, m, me in zip(b_last, m_rows, m_end)]
    yield
    for h in hs:
        for s in range(R):
            kw_s = kw[h] if R == 1 else jnp.where(rowseq == s, kw[h], 0.0)
            cs_s = cs[h][s * T:s * T + 1, :]
            c_ref[s, h] = cs_s * c_ref[s, h] + _dot_tn(kw_s, v[h])
            n_ref[s, h:h + 1, :] = cs_s * n_ref[s, h:h + 1, :] + jnp.sum(kw_s, axis=0, keepdims=True)
            m_ref[s, :, h:h + 1] = m_end[h][s * T:s * T + 1, :]
        yield

    for h in hs:
        og = oa[:, h * DV:(h + 1) * DV]
        out = _rms(hh[h], ng_ref[:, h * DV:(h + 1) * DV]) * _sigmoid(og)
        o_ref[:, h * DV:(h + 1) * DV] = out.astype(o_ref.dtype)
    yield


def _cast_steps(pairs):
    for src_ref, dst_ref in pairs:
        dst_ref[...] = src_ref[...].astype(BF16)
        yield


def _mixers_kernel(*refs, n_cast, R, T, Hd, Hm, DK, DV, MDK, MDV, width, off_i, off_f):
    (qkv_ref, z_ref, mq_ref, mk_ref, mv_ref, mo_ref, sm_ref, cst_ref, s0_ref, c0_ref, n0_ref, m0_ref,
     cw_ref, par_ref, dng_ref, mng_ref) = refs[:16]
    cast_in = refs[16:16 + n_cast]
    odn_ref, s_ref, oml_ref, c_ref, n_ref, m_ref = refs[16 + n_cast:22 + n_cast]
    cast_out = refs[22 + n_cast:22 + 2 * n_cast]
    ext_ref, = refs[22 + 2 * n_cast:]

    @pl.when(pl.program_id(1) == 0)
    def _():
        ext_ref[:, SUBLANES - (width - 1):SUBLANES, :] = cst_ref[...]
        s_ref[...] = s0_ref[...]
        c_ref[...] = c0_ref[...]
        n_ref[...] = n0_ref[...]
        m_ref[...] = m0_ref[...]

    masks = _seq_masks(R, T)
    streams = [
        _gdn_steps(qkv_ref, z_ref, sm_ref, cw_ref, par_ref, dng_ref, odn_ref, s_ref, ext_ref, masks,
                   R=R, T=T, H=Hd, DK=DK, DV=DV, width=width),
        _mlstm_steps(mq_ref, mk_ref, mv_ref, mo_ref, sm_ref, par_ref, mng_ref, oml_ref, c_ref, n_ref,
                     m_ref, masks, R=R, T=T, H=Hm, DK=MDK, DV=MDV, off_i=off_i, off_f=off_f),
        _cast_steps(list(zip(cast_in, cast_out))),
    ]
    turns = [2, 1, 1]
    while streams:
        for gen, n in list(zip(streams, turns)):
            for _ in range(n):
                if gen in streams and next(gen, "done") == "done":
                    turns.pop(streams.index(gen))
                    streams.remove(gen)


def _cast_slab_rows(k, steps):
    return next((r for r in range(16, k + 1, 16) if k % r == 0 and k // r <= steps), None)


def _mixers(proj3, small3, conv_state, s0, c0, n0, m0, conv_w, par, dn_norm_g, ml_norm_g,
            *, R, T, N, off_i, off_f, casts=()):
    B, Hd, DK, DV = s0.shape
    _, Hm, MDK, MDV = c0.shape
    QKV = conv_state.shape[-1]
    VW, MQK, MVW = Hd * DV, Hm * MDK, Hm * MDV
    width = conv_w.shape[0]
    col0 = QKV + VW
    assert QKV == 2 * Hd * DK + VW and QKV % VW == 0 and VW % LANES == 0
    assert col0 % MQK == 0 and (col0 + 2 * MQK) % MVW == 0 and MQK % LANES == 0
    rows = proj3.shape[0] * T
    qb, vb = col0 // MQK, (col0 + 2 * MQK) // MVW
    kern = functools.partial(_mixers_kernel, n_cast=len(casts), R=R, T=T, Hd=Hd, Hm=Hm, DK=DK, DV=DV,
                             MDK=MDK, MDV=MDV, width=width, off_i=off_i, off_f=off_f)
    tok = lambda w, c: pl.BlockSpec((R, T, w), lambda b, n: (b * N + n, 0, c))
    const = lambda shape: pl.BlockSpec(shape, lambda b, n: (0,) * len(shape))
    per_seq = lambda *dims: pl.BlockSpec((R,) + dims, lambda b, n: (b,) + (0,) * len(dims))
    state_specs = [per_seq(Hd, DK, DV), per_seq(Hm, MDK, MDV), per_seq(Hm, MDK), per_seq(1, Hm)]

    def slab(w):
        r = _cast_slab_rows(w.shape[0], (B // R) * N)
        return pl.BlockSpec((r, w.shape[1]), lambda b, n: (jnp.minimum(b * N + n, w.shape[0] // r - 1), 0))

    cast_specs = [slab(w) for w in casts]
    odn, s_new, oml, c_new, n_new, m_new, *cast = pl.pallas_call(
        kern,
        grid=(B // R, N),
        in_specs=[
            tok(QKV, 0), tok(VW, QKV // VW), tok(MQK, qb), tok(MQK, qb + 1), tok(MVW, vb),
            tok(MVW, vb + 1), tok(LANES, 0),
            per_seq(width - 1, QKV), *state_specs,
            const((width, QKV)), const((SUBLANES, LANES)), const((1, DV)), const((1, MVW)),
            *cast_specs,
        ],
        out_specs=[pl.BlockSpec((R * T, VW), lambda b, n: (b * N + n, 0)), state_specs[0],
                   pl.BlockSpec((R * T, MVW), lambda b, n: (b * N + n, 0)), *state_specs[1:],
                   *cast_specs],
        out_shape=[jax.ShapeDtypeStruct((rows, VW), BF16), jax.ShapeDtypeStruct(s0.shape, F32),
                   jax.ShapeDtypeStruct((rows, MVW), BF16), jax.ShapeDtypeStruct(c0.shape, F32),
                   jax.ShapeDtypeStruct(n0.shape, F32), jax.ShapeDtypeStruct(m0.shape, F32),
                   *[jax.ShapeDtypeStruct(w.shape, BF16) for w in casts]],
        scratch_shapes=[pltpu.VMEM((R, SUBLANES + T, QKV), F32)],
        compiler_params=_cparams(("arbitrary", "arbitrary")),
        name="mixers",
    )(proj3, proj3, proj3, proj3, proj3, proj3, small3, conv_state, s0, c0, n0, m0, conv_w, par,
      dn_norm_g.reshape(1, DV), ml_norm_g.reshape(1, MVW), *casts)
    return odn, s_new, oml, c_new, n_new, m_new, cast


def _outproj_kernel(x_ref, a_ref, b_ref, wa_ref, wb_ref, o_ref):
    o_ref[...] = (x_ref[...] + jnp.dot(a_ref[...], wa_ref[...], preferred_element_type=F32)
                  + jnp.dot(b_ref[...], wb_ref[...], preferred_element_type=F32))


def _outproj(x, a, b, w_out):
    m, d = x.shape
    ka, kb = a.shape[1], b.shape[1]
    assert ka == kb
    tm = _pick(m, (512, 256, 128, 64, 32, 16, 8))
    return pl.pallas_call(
        _outproj_kernel,
        grid=(m // tm,),
        in_specs=[
            pl.BlockSpec((tm, d), lambda i: (i, 0)),
            pl.BlockSpec((tm, ka), lambda i: (i, 0)),
            pl.BlockSpec((tm, kb), lambda i: (i, 0)),
            pl.BlockSpec((ka, d), lambda i: (0, 0), pipeline_mode=pl.Buffered(1)),
            pl.BlockSpec((kb, d), lambda i: (1, 0), pipeline_mode=pl.Buffered(1)),
        ],
        out_specs=pl.BlockSpec((tm, d), lambda i: (i, 0)),
        out_shape=jax.ShapeDtypeStruct((m, d), F32),
        compiler_params=_cparams(("parallel",)),
        name="outproj",
    )(x, a, b, w_out, w_out)


def _ffn_kernel(x_ref, g_ref, wg_ref, wu_ref, cw_ref, cb_ref, wd_ref, stg_ref, stu_ref, gf_ref,
                y_ref, nsg_ref, nsu_ref, h_ref, act_ref, ext_ref, *carry,
                tm, tf, ff, shift, pad, tps, width, nj, wide_state, final_norm):
    i = pl.program_id(0)
    j = pl.program_id(1)
    hist = (width - 1) * shift

    def activate_tile():
        conv = []
        for idx, (w_ref, st_ref, ns_ref) in enumerate(
                ((wg_ref, stg_ref, nsg_ref), (wu_ref, stu_ref, nsu_ref))):
            col = pl.ds(pl.multiple_of(idx * ff + j * tf, LANES), tf)
            st_col = col if wide_state else slice(None)
            u = jnp.dot(h_ref[...], w_ref[...], preferred_element_type=F32)
            ext_ref[idx, pad:pad + tm, :] = u
            if tps == 1:
                ext_ref[idx, pad - hist:pad, :] = st_ref[0, :, st_col]
            else:
                ext_ref[idx, pad - hist:pad, :] = jnp.where(i % tps == 0, st_ref[0, :, st_col],
                                                            carry[0][j, idx])
            cw = cw_ref[:, col]
            c = u * cw[width - 1:width, :] + cb_ref[:, col]
            for t in range(width - 1):
                off = pad - hist + t * shift
                c = c + ext_ref[idx, off:off + tm, :] * cw[t:t + 1, :]
            new = u[tm - hist:, :]
            ns_ref[0, :, st_col] = new
            if tps != 1:
                carry[0][j, idx] = new
            conv.append(c)
        act_ref[...] = (conv[0] * _sigmoid(conv[0]) * conv[1]).astype(BF16)

    def down_tile():
        return jnp.dot(act_ref[...], wd_ref[...], preferred_element_type=F32)

    @pl.when(j == 0)
    def _():
        h_ref[...] = _rms(x_ref[...], g_ref[...]).astype(BF16)
        y_ref[...] = jnp.zeros_like(y_ref)
        activate_tile()

    @pl.when((j > 0) & (j < nj))
    def _():
        y_ref[...] += down_tile()
        activate_tile()

    @pl.when(j == nj)
    def _():
        xo = x_ref[...] + y_ref[...] + down_tile()
        y_ref[...] = _rms(xo, gf_ref[...]) if final_norm else xo


def _ffn(x, norm_g, w_up, conv_w, conv_b, w_down, state, final_g, *, tm, shift, tps, final_norm):
    m, d = x.shape
    ff = w_down.shape[0]
    width = conv_w.shape[0]
    hist = (width - 1) * shift
    pad = -(-hist // SUBLANES) * SUBLANES
    tf = _pick(ff, (512, 256, 128))
    nj = ff // tf
    n_tiles = m // tm
    wide_state = hist * 2 * ff * 4 <= WIDE_STATE_BYTES
    kern = functools.partial(_ffn_kernel, tm=tm, tf=tf, ff=ff, shift=shift, pad=pad, tps=tps,
                             width=width, nj=nj, wide_state=wide_state, final_norm=final_norm)
    scratch = [pltpu.VMEM((tm, d), BF16), pltpu.VMEM((tm, tf), BF16), pltpu.VMEM((2, pad + tm, tf), F32)]
    cur = lambda j: jnp.minimum(j, nj - 1)
    prv = lambda j: jnp.maximum(j - 1, 0)
    if tps != 1:
        scratch.append(pltpu.VMEM((nj, 2, hist, tf), F32))
    if wide_state:
        st_in = [pl.BlockSpec((1, hist, 2 * ff), lambda i, j: (i // tps, 0, 0))] * 2
        st_out = [pl.BlockSpec((1, hist, 2 * ff), lambda i, j: (i, 0, 0))]
        st_shape = [jax.ShapeDtypeStruct((n_tiles, hist, 2 * ff), F32)]
    else:
        st_in = [pl.BlockSpec((1, hist, tf), lambda i, j: (i // tps, 0, cur(j))),
                 pl.BlockSpec((1, hist, tf), lambda i, j: (i // tps, 0, nj + cur(j)))]
        st_out = [pl.BlockSpec((1, hist, tf), lambda i, j: (i, 0, cur(j)))] * 2
        st_shape = [jax.ShapeDtypeStruct((n_tiles, hist, ff), F32)] * 2
    call = pl.pallas_call(
        kern if not wide_state else _single_history(kern),
        grid=(n_tiles, nj + 1),
        in_specs=[
            pl.BlockSpec((tm, d), lambda i, j: (i, 0), pipeline_mode=pl.Buffered(1)),
            pl.BlockSpec((1, d), lambda i, j: (0, 0)),
            pl.BlockSpec((d, tf), lambda i, j: (0, cur(j))),
            pl.BlockSpec((d, tf), lambda i, j: (0, nj + cur(j))),
            pl.BlockSpec((width, 2 * ff), lambda i, j: (0, 0)),
            pl.BlockSpec((1, 2 * ff), lambda i, j: (0, 0)),
            pl.BlockSpec((tf, d), lambda i, j: (prv(j), 0)),
            *st_in,
            pl.BlockSpec((1, d), lambda i, j: (0, 0)),
        ],
        out_specs=[pl.BlockSpec((tm, d), lambda i, j: (i, 0)), *st_out],
        out_shape=[jax.ShapeDtypeStruct((m, d), F32), *st_shape],
        scratch_shapes=scratch,
        compiler_params=_cparams(("arbitrary", "arbitrary")),
        name="convffn",
    )
    y, *ns = call(x, norm_g.reshape(1, d), w_up, w_up, conv_w, conv_b.reshape(1, 2 * ff), w_down,
                  state, state, final_g.reshape(1, d))
    return y, [a[tps - 1::tps] for a in ns]


def _single_history(kern):
    def wrapped(*refs):
        ins, (y_ref, ns_ref), scratch = refs[:10], refs[10:12], refs[12:]
        return kern(*ins, y_ref, ns_ref, ns_ref, *scratch)
    return wrapped


def _layer(x, st, p, final_g, final_norm, decode):
    conv_buf, s0, c0, n0, m0, ffn_buf = st
    B, L, D = x.shape
    Hd, Hm = s0.shape[1], c0.shape[1]
    dn_qkv = conv_buf.shape[-1]
    dn_vw = Hd * s0.shape[3]
    ff = p["w_down"].shape[0]

    T = next((c for c in (MIX_CHUNK, CHUNK) if L % c == 0), L)
    N = L // T
    R = _pick(B, (8, 4, 2, 1)) if decode else 1
    if R * T > 128:
        R = 1

    x2 = x.reshape(B * L, D)
    big, small = _inproj(x2, p["norm_mix_g"], p["w_big"], p["w_small"])
    big3 = big.reshape(B * N, T, big.shape[1])
    small3 = small.reshape(B * N, T, LANES)

    pending = [k for k in ("w_out", "w_up", "w_down") if p[k].dtype != BF16]
    if not all(_cast_slab_rows(p[k].shape[0], (B // R) * N) for k in pending):
        p = dict(p, **{k: p[k].astype(BF16) for k in pending})
        pending = []
    o_dn, s_new, o_ml, c_new, n_new, m_new, rounded = _mixers(
        big3, small3, conv_buf, s0, c0, n0, m0.reshape(B, 1, Hm), p["dn_conv_w"], p["par"],
        p["dn_norm_g"], p["ml_norm_g"], R=R, T=T, N=N, off_i=2 * Hd, off_f=2 * Hd + Hm,
        casts=tuple(p[k] for k in pending))
    p = dict(p, **dict(zip(pending, rounded)))
    width = p["dn_conv_w"].shape[0]
    big4 = big.reshape(B, L, big.shape[1])
    if L >= width - 1:
        conv_new = big4[:, L - (width - 1):, :dn_qkv]
    else:
        conv_new = jnp.concatenate([conv_buf, big4[:, :, :dn_qkv]], axis=1)[:, L:]

    x1 = _outproj(x2, o_dn, o_ml, p["w_out"])

    fw = p["ffn_conv_w"].shape[0]
    if decode:
        x1t = x1.reshape(B, L, D).transpose(1, 0, 2).reshape(L * B, D)
        state = ffn_buf.transpose(1, 0, 2).reshape(1, (fw - 1) * B, 2 * ff)
        y, parts = _ffn(x1t, p["norm_ffn_g"], p["w_up"], p["ffn_conv_w"], p["ffn_conv_b"],
                        p["w_down"], state, final_g, tm=L * B, shift=B, tps=1, final_norm=final_norm)
        y = y.reshape(L, B, D).transpose(1, 0, 2)
        ffn_new = jnp.concatenate(
            [a.reshape(fw - 1, B, a.shape[-1]).transpose(1, 0, 2) for a in parts], axis=-1)
    else:
        tm = _pick(L, (1024, 512, 256, 128, 64, 32, 16, 8))
        y, parts = _ffn(x1, p["norm_ffn_g"], p["w_up"], p["ffn_conv_w"], p["ffn_conv_b"],
                        p["w_down"], ffn_buf, final_g, tm=tm, shift=1, tps=L // tm,
                        final_norm=final_norm)
        y = y.reshape(B, L, D)
        ffn_new = jnp.concatenate(parts, axis=-1)
    return y, (conv_new, s_new, c_new, n_new, m_new.reshape(B, Hm), ffn_new), p


def _regroup_kernel(w_ref, big_ref, small_ref, *, c0, c1, c2):
    n = w_ref.shape[0]
    big_ref[:c0, :] = w_ref[:c0, :].astype(BF16)
    big_ref[c0:, :] = w_ref[c1:c2, :].astype(BF16)
    narrow = jnp.concatenate(
        [w_ref[c0:c1, :], w_ref[c2:, :], jnp.zeros((LANES - (c1 - c0) - (n - c2), w_ref.shape[1]), F32)],
        axis=0)
    small_ref[...] = narrow.astype(BF16)


def _regroup(w_t, c0, c1, c2):
    n, k = w_t.shape
    tk = _pick(k, (256, 128))
    nb = c0 + c2 - c1
    assert c0 % 16 == 0 and c1 % SUBLANES == 0 and c2 % SUBLANES == 0
    return pl.pallas_call(
        functools.partial(_regroup_kernel, c0=c0, c1=c1, c2=c2),
        grid=(k // tk,),
        in_specs=[pl.BlockSpec((n, tk), lambda i: (0, i))],
        out_specs=[pl.BlockSpec((nb, tk), lambda i: (0, i)), pl.BlockSpec((LANES, tk), lambda i: (0, i))],
        out_shape=[jax.ShapeDtypeStruct((nb, k), BF16), jax.ShapeDtypeStruct((LANES, k), BF16)],
        compiler_params=_cparams(("parallel",)),
        name="regroup",
    )(w_t)


def _prep_params(l, Hd, Hm, dn_qkv, dn_vw, ml_qk, ml_vw, norm_mix_g, w_in, dn_conv_w, dn_A_log,
                 dn_dt_bias, dn_norm_g, ml_i_bias, ml_f_bias, ml_norm_g, w_out, norm_ffn_g, w_up,
                 ffn_conv_w, ffn_conv_b, w_down):
    w = w_in[l]
    c0 = dn_qkv + dn_vw
    c1 = c0 + 2 * Hd
    c2 = c1 + 2 * ml_qk + 2 * ml_vw
    nsmall = 2 * Hd + 2 * Hm
    assert nsmall <= LANES and w.shape[1] == c2 + 2 * Hm
    w_big, w_small = _regroup(w.T, c0, c1, c2)
    z = lambda n: jnp.zeros((n,), F32)
    bias = jnp.concatenate([z(Hd), dn_dt_bias[l], ml_i_bias[l], ml_f_bias[l], z(LANES - nsmall)])
    alog = jnp.concatenate([z(Hd), dn_A_log[l], z(LANES - 2 * Hd)])
    par = jnp.zeros((SUBLANES, LANES), F32).at[0].set(bias).at[1].set(alog)
    return dict(norm_mix_g=norm_mix_g[l], w_big=w_big, w_small=w_small, dn_conv_w=dn_conv_w[l],
                par=par, dn_norm_g=dn_norm_g[l], ml_norm_g=ml_norm_g[l], w_out=w_out[l],
                norm_ffn_g=norm_ffn_g[l], w_up=w_up[l], ffn_conv_w=ffn_conv_w[l],
                ffn_conv_b=ffn_conv_b[l], w_down=w_down[l])


def kernel(x_prompt, x_sample, state_dn_conv, state_dn_S, state_ml_C, state_ml_n, state_ml_m,
           state_ffn_conv, norm_mix_g, w_in, dn_conv_w, dn_A_log, dn_dt_bias, dn_norm_g,
           ml_i_bias, ml_f_bias, ml_norm_g, w_out, norm_ffn_g, w_up, ffn_conv_w, ffn_conv_b,
           w_down, norm_final_g):
    states = (state_dn_conv, state_dn_S, state_ml_C, state_ml_n, state_ml_m, state_ffn_conv)
    depth = w_in.shape[0]
    batch = x_prompt.shape[0]
    Hd, dk, dv = state_dn_S.shape[2:]
    Hm, mdk, mdv = state_ml_C.shape[2:]
    xp, xs = x_prompt, x_sample
    p_new = [[] for _ in states]
    s_new = [[] for _ in states]
    for l in range(depth):
        p = _prep_params(l, Hd, Hm, state_dn_conv.shape[-1], Hd * dv, Hm * mdk, Hm * mdv,
                         norm_mix_g, w_in, dn_conv_w, dn_A_log, dn_dt_bias, dn_norm_g, ml_i_bias,
                         ml_f_bias, ml_norm_g, w_out, norm_ffn_g, w_up, ffn_conv_w, ffn_conv_b,
                         w_down)
        last = l == depth - 1
        st_p = tuple(jnp.zeros((batch,) + s.shape[2:], s.dtype) for s in states)
        st_s = tuple(s[l] for s in states)
        xp, np_st, p = _layer(xp, st_p, p, norm_final_g, last, decode=False)
        xs, ns_st, p = _layer(xs, st_s, p, norm_final_g, last, decode=True)
        for i in range(len(states)):
            p_new[i].append(np_st[i])
            s_new[i].append(ns_st[i])
    outs_p = tuple(jnp.stack(a, axis=0) for a in p_new)
    outs_s = tuple(jnp.stack(a, axis=0) for a in s_new)
    return (xp, xs) + outs_p + outs_s
```

```python
import functools

import jax
import jax.numpy as jnp
from jax import lax
from jax.experimental import pallas as pl
from jax.experimental.pallas import tpu as pltpu

EPS = 1e-6
CHUNK = 64
MIX_CHUNK = 128
F32 = jnp.float32
BF16 = jnp.bfloat16
LANES = 128
SUBLANES = 8
VMEM_LIMIT = 56 * 1024 * 1024
WIDE_STATE_BYTES = 2 * 1024 * 1024


def _cparams(sem):
    return pltpu.CompilerParams(dimension_semantics=sem, vmem_limit_bytes=VMEM_LIMIT)


def _dot(a, b):
    return jnp.dot(a.astype(BF16), b.astype(BF16), preferred_element_type=F32)


def _dot_nt(a, b):
    return lax.dot_general(a.astype(BF16), b.astype(BF16), (((1,), (1,)), ((), ())),
                           preferred_element_type=F32)


def _dot_tn(a, b):
    return lax.dot_general(a.astype(BF16), b.astype(BF16), (((0,), (0,)), ((), ())),
                           preferred_element_type=F32)


def _mask_dot_exact(mask_bf16, x):
    hi = x.astype(BF16)
    r1 = x - hi.astype(F32)
    mid = r1.astype(BF16)
    lo = (r1 - mid.astype(F32)).astype(BF16)
    d = functools.partial(jnp.dot, preferred_element_type=F32)
    return (d(mask_bf16, hi) + d(mask_bf16, mid)) + d(mask_bf16, lo)


def _softplus(x):
    return jnp.maximum(x, 0.0) + jnp.log1p(jnp.exp(-jnp.abs(x)))


def _sigmoid(x):
    return 1.0 / (1.0 + jnp.exp(-x))


def _rms(x, g):
    return x * lax.rsqrt(jnp.mean(x * x, axis=-1, keepdims=True) + EPS) * g


def _log2(n):
    assert n & (n - 1) == 0 and n > 0, n
    return n.bit_length() - 1


def _pick(n, cands):
    for c in cands:
        if n % c == 0:
            return c
    return n


def _inproj_kernel(x_ref, g_ref, wb_ref, ws_ref, ob_ref, os_ref, h_ref):
    nt = functools.partial(lax.dot_general, dimension_numbers=(((1,), (1,)), ((), ())),
                           preferred_element_type=F32)

    @pl.when(pl.program_id(1) == 0)
    def _():
        h = _rms(x_ref[...], g_ref[...]).astype(BF16)
        h_ref[...] = h
        os_ref[...] = nt(h, ws_ref[...])

    ob_ref[...] = nt(h_ref[...], wb_ref[...])


def _inproj(x, g, w_big_t, w_small_t):
    m, d = x.shape
    nb = w_big_t.shape[0]
    tm = _pick(m, (1024, 512, 256, 128, 64, 32, 16, 8))
    tn = _pick(nb, (1024, 512, 256, 128))
    return pl.pallas_call(
        _inproj_kernel,
        grid=(m // tm, nb // tn),
        in_specs=[
            pl.BlockSpec((tm, d), lambda i, j: (i, 0)),
            pl.BlockSpec((1, d), lambda i, j: (0, 0)),
            pl.BlockSpec((tn, d), lambda i, j: (j, 0)),
            pl.BlockSpec((LANES, d), lambda i, j: (0, 0)),
        ],
        out_specs=[
            pl.BlockSpec((tm, tn), lambda i, j: (i, j)),
            pl.BlockSpec((tm, LANES), lambda i, j: (i, 0)),
        ],
        out_shape=[jax.ShapeDtypeStruct((m, nb), F32), jax.ShapeDtypeStruct((m, LANES), F32)],
        scratch_shapes=[pltpu.VMEM((tm, d), BF16)],
        compiler_params=_cparams(("parallel", "arbitrary")),
        name="inproj",
    )(x, g.reshape(1, d), w_big_t, w_small_t)


def _seq_masks(R, T):
    RT = R * T
    sh = _log2(T)
    ri = lax.broadcasted_iota(jnp.int32, (RT, RT), 0)
    ci = lax.broadcasted_iota(jnp.int32, (RT, RT), 1)
    same = (ri >> sh) == (ci >> sh)
    tril = same & (ci <= ri)
    strict = same & (ci < ri)
    eye = ri == ci
    lastsel = same & ((ci & (T - 1)) == T - 1)
    rowseq = lax.broadcasted_iota(jnp.int32, (RT, 1), 0) >> sh
    return ri, ci, tril, strict, eye, lastsel, rowseq


def _row_form(col, eye):
    return jnp.sum(jnp.where(eye, col, 0.0), axis=0, keepdims=True)


def _at_last(row, lastsel):
    return jnp.sum(jnp.where(lastsel, row, 0.0), axis=1, keepdims=True)


def _unit_lower_inverses(lms, ri, ci, T):
    ds = None
    s = 1
    while s < T:
        sh = _log2(2 * s)
        blk = ((ri >> sh) == (ci >> sh)) & ((ri & (2 * s - 1)) >= s) & ((ci & (2 * s - 1)) < s)
        ms = [jnp.where(blk, lm, 0.0) for lm in lms]
        if ds is None:
            eye = jnp.where(ri == ci, 1.0, 0.0)
            ds = [eye - m for m in ms]
        else:
            dms = [_dot(d, m) for d, m in zip(ds, ms)]
            yield
            ds = [d - _dot(dm, d) for d, dm in zip(ds, dms)]
        yield
        s *= 2
    return ds


def _causal_conv(ext_ref, x_ref, cw_ref, T, width, cols):
    lo = SUBLANES - (width - 1)
    ext_ref[:, SUBLANES:SUBLANES + T, cols] = x_ref[:, :, cols]
    cw = cw_ref[:, cols]
    acc = None
    for j in range(width):
        term = ext_ref[:, lo + j:lo + j + T, cols] * cw[j:j + 1, :]
        acc = term if acc is None else acc + term
    ext_ref[:, lo:SUBLANES, cols] = ext_ref[:, lo + T:SUBLANES + T, cols]
    return acc


def _gdn_steps(qkv_ref, z_ref, sm_ref, cw_ref, par_ref, ng_ref, o_ref, s_ref, ext_ref, masks,
               *, R, T, H, DK, DV, width):
    RT = R * T
    QK = H * DK

    groups = []
    for c0, c1 in ((0, QK), (QK, 2 * QK), (2 * QK, 2 * QK + H * DV)):
        a = _causal_conv(ext_ref, qkv_ref, cw_ref, T, width, slice(c0, c1)).reshape(RT, c1 - c0)
        groups.append(a * _sigmoid(a))
        if R > 1:
            yield
    yq, yk, yv = groups
    if R == 1:
        yield
    z = z_ref[...].reshape(RT, H * DV)

    sm = sm_ref[...].reshape(RT, LANES)
    beta_all = _sigmoid(sm)
    g_all = -jnp.exp(par_ref[1:2, :]) * _softplus(sm + par_ref[0:1, :])

    ri, ci, tril, strict, eye, lastsel, rowseq = masks
    g_cum = _mask_dot_exact(jnp.where(tril, 1.0, 0.0).astype(BF16), g_all)
    yield

    hs = range(H)
    q = [yq[:, h * DK:(h + 1) * DK] for h in hs]
    k = [yk[:, h * DK:(h + 1) * DK] for h in hs]
    v = [yv[:, h * DV:(h + 1) * DV] for h in hs]
    q = [a * lax.rsqrt(jnp.sum(a * a, axis=-1, keepdims=True) + EPS) * (DK ** -0.5) for a in q]
    yield
    k = [a * lax.rsqrt(jnp.sum(a * a, axis=-1, keepdims=True) + EPS) for a in k]
    yield
    beta = [beta_all[:, h:h + 1] for h in hs]
    gc = [g_cum[:, H + h:H + h + 1] for h in hs]
    gr = [_row_form(a, eye) for a in gc]
    if R == 1:
        glast = [a[RT - 1:RT, :] for a in gc]
    else:
        glast = [_at_last(a, lastsel) for a in gr]
    yield
    decay = [jnp.exp(jnp.where(tril, c - r, -jnp.inf)) for c, r in zip(gc, gr)]
    kb = [a * b for a, b in zip(k, beta)]
    yield
    lm = [jnp.where(strict, _dot_nt(a, b) * d, 0.0) for a, b, d in zip(kb, k, decay)]
    yield
    tinv = yield from _unit_lower_inverses(lm, ri, ci, T)
    eg = [jnp.exp(a) for a in gc]
    uw = [_dot(t, jnp.concatenate([a * b, c * e], axis=1))
          for t, a, b, c, e in zip(tinv, v, beta, kb, eg)]
    yield
    u = [a[:, :DV] for a in uw]
    w = [a[:, DV:] for a in uw]
    qk = [_dot_nt(a, b) * d for a, b, d in zip(q, k, decay)]
    yield
    qe = [a * e for a, e in zip(q, eg)]
    kd = [a * jnp.exp(l - c) for a, l, c in zip(k, glast, gc)]
    gl = [jnp.exp(a) for a in glast]
    yield

    if R == 1:
        s_old = [s_ref[0, h] for h in hs]
        ws = [_dot(jnp.concatenate([a, b], axis=0), c) for a, b, c in zip(w, qe, s_old)]
        yield
        v_new = [a - b[:RT] for a, b in zip(u, ws)]
        o = [a[RT:] + _dot(b, c) for a, b, c in zip(ws, qk, v_new)]
        yield
        for h in hs:
            s_ref[0, h] = s_old[h] * gl[h][0:1, :] + _dot_tn(kd[h], v_new[h])
        yield
    else:
        ws = [[_dot(jnp.concatenate([w[h][s * T:(s + 1) * T], qe[h][s * T:(s + 1) * T]], axis=0),
                    s_ref[s, h]) for s in range(R)] for h in hs]
        yield
        v_new = [u[h] - jnp.concatenate([a[:T] for a in ws[h]], axis=0) for h in hs]
        o = [jnp.concatenate([a[T:] for a in ws[h]], axis=0) + _dot(qk[h], v_new[h]) for h in hs]
        yield
        for h in hs:
            for s in range(R):
                kd_s = jnp.where(rowseq == s, kd[h], 0.0)
                s_ref[s, h] = s_ref[s, h] * gl[h][s * T:s * T + 1, :] + _dot_tn(kd_s, v_new[h])
            yield

    for h in hs:
        zh = z[:, h * DV:(h + 1) * DV]
        out = _rms(o[h], ng_ref[...]) * (zh * _sigmoid(zh))
        o_ref[:, h * DV:(h + 1) * DV] = out.astype(o_ref.dtype)
    yield


def _mlstm_steps(q_ref, k_ref, v_ref, og_ref, sm_ref, par_ref, ng_ref, o_ref, c_ref, n_ref, m_ref,
                 masks, *, R, T, H, DK, DV, off_i, off_f):
    RT = R * T
    ri, ci, tril, strict, eye, lastsel, rowseq = masks

    pre = sm_ref[...].reshape(RT, LANES) + par_ref[0:1, :]
    logf_all = -_softplus(-pre)
    b_all = _mask_dot_exact(jnp.where(tril, 1.0, 0.0).astype(BF16), logf_all)
    yield

    qa = q_ref[...].reshape(RT, H * DK)
    ka = k_ref[...].reshape(RT, H * DK)
    va = v_ref[...].reshape(RT, H * DV)
    oa = og_ref[...].reshape(RT, H * DV)

    hs = range(H)
    q = [qa[:, h * DK:(h + 1) * DK] * (DK ** -0.5) for h in hs]
    k = [ka[:, h * DK:(h + 1) * DK] for h in hs]
    v = [va[:, h * DV:(h + 1) * DV] for h in hs]
    bc = [b_all[:, off_f + h:off_f + h + 1] for h in hs]
    ic = [pre[:, off_i + h:off_i + h + 1] for h in hs]
    dm = [jnp.where(tril, b + _row_form(i - b, eye), -jnp.inf) for b, i in zip(bc, ic)]
    yield

    if R == 1:
        m_rows = [m_ref[0, :, h:h + 1] for h in hs]
    else:
        m_rows = []
        for h in hs:
            mr = jnp.zeros((RT, 1), F32)
            for s in range(R):
                mr = jnp.where(rowseq == s, m_ref[s, :, h:h + 1], mr)
            m_rows.append(mr)
    m_new = [jnp.maximum(b + m, jnp.max(d, axis=1, keepdims=True)) for b, m, d in zip(bc, m_rows, dm)]
    inter = [jnp.exp(b + m - mn) for b, m, mn in zip(bc, m_rows, m_new)]
    yield
    smat = [_dot_nt(a, b) * jnp.exp(d - mn) for a, b, d, mn in zip(q, k, dm, m_new)]
    yield

    if R == 1:
        qc = [_dot(q[h], c_ref[0, h]) for h in hs]
        qn = [jnp.sum(q[h] * n_ref[0, h:h + 1, :], axis=1, keepdims=True) for h in hs]
    else:
        qc, qn = [], []
        for h in hs:
            rows = [slice(s * T, (s + 1) * T) for s in range(R)]
            qc.append(jnp.concatenate([_dot(q[h][r], c_ref[s, h]) for s, r in enumerate(rows)], axis=0))
            qn.append(jnp.concatenate(
                [jnp.sum(q[h][r] * n_ref[s, h:h + 1, :], axis=1, keepdims=True)
                 for s, r in enumerate(rows)], axis=0))
            yield
    yield
    num = [i * c + _dot(s, a) for i, c, s, a in zip(inter, qc, smat, v)]
    den = [i * n + jnp.sum(s, axis=1, keepdims=True) for i, n, s in zip(inter, qn, smat)]
    yield
    hh = [a / jnp.maximum(jnp.abs(b), jnp.exp(-mn)) for a, b, mn in zip(num, den, m_new)]

    if R == 1:
        b_last = [b[RT - 1:RT, :] for b in bc]
        m_end = [mn[RT - 1:RT, :] for mn in m_new]
    else:
        b_last = [_at_last(_row_form(b, eye), lastsel) for b in bc]
        m_end = [_at_last(_row_form(mn, eye), lastsel) for mn in m_new]
    yield
    kw = [a * jnp.exp(bl - b + i - me) for a, bl, b, i, me in zip(k, b_last, bc, ic, m_end)]
    cs = [jnp.exp(bl + m - me) for bl, m, me in zip(b_last, m_rows, m_end)]
    yield
    for h in hs:
        for s in range(R):
            kw_s = kw[h] if R == 1 else jnp.where(rowseq == s, kw[h], 0.0)
            cs_s = cs[h][s * T:s * T + 1, :]
            c_ref[s, h] = cs_s * c_ref[s, h] + _dot_tn(kw_s, v[h])
            n_ref[s, h:h + 1, :] = cs_s * n_ref[s, h:h + 1, :] + jnp.sum(kw_s, axis=0, keepdims=True)
            m_ref[s, :, h:h + 1] = m_end[h][s * T:s * T + 1, :]
        yield

    for h in hs:
        og = oa[:, h * DV:(h + 1) * DV]
        out = _rms(hh[h], ng_ref[:, h * DV:(h + 1) * DV]) * _sigmoid(og)
        o_ref[:, h * DV:(h + 1) * DV] = out.astype(o_ref.dtype)
    yield


def _cast_steps(pairs):
    for src_ref, dst_ref in pairs:
        dst_ref[...] = src_ref[...].astype(BF16)
        yield


def _mixers_kernel(*refs, n_cast, R, T, Hd, Hm, DK, DV, MDK, MDV, width, off_i, off_f):
    (qkv_ref, z_ref, mq_ref, mk_ref, mv_ref, mo_ref, sm_ref, cst_ref, s0_ref, c0_ref, n0_ref, m0_ref,
     cw_ref, par_ref, dng_ref, mng_ref) = refs[:16]
    cast_in = refs[16:16 + n_cast]
    odn_ref, s_ref, oml_ref, c_ref, n_ref, m_ref = refs[16 + n_cast:22 + n_cast]
    cast_out = refs[22 + n_cast:22 + 2 * n_cast]
    ext_ref, = refs[22 + 2 * n_cast:]

    @pl.when(pl.program_id(1) == 0)
    def _():
        ext_ref[:, SUBLANES - (width - 1):SUBLANES, :] = cst_ref[...]
        s_ref[...] = s0_ref[...]
        c_ref[...] = c0_ref[...]
        n_ref[...] = n0_ref[...]
        m_ref[...] = m0_ref[...]

    masks = _seq_masks(R, T)
    streams = [
        _gdn_steps(qkv_ref, z_ref, sm_ref, cw_ref, par_ref, dng_ref, odn_ref, s_ref, ext_ref, masks,
                   R=R, T=T, H=Hd, DK=DK, DV=DV, width=width),
        _mlstm_steps(mq_ref, mk_ref, mv_ref, mo_ref, sm_ref, par_ref, mng_ref, oml_ref, c_ref, n_ref,
                     m_ref, masks, R=R, T=T, H=Hm, DK=MDK, DV=MDV, off_i=off_i, off_f=off_f),
        _cast_steps(list(zip(cast_in, cast_out))),
    ]
    turns = [2, 1, 1]
    while streams:
        for gen, n in list(zip(streams, turns)):
            for _ in range(n):
                if gen in streams and next(gen, "done") == "done":
                    turns.pop(streams.index(gen))
                    streams.remove(gen)


def _cast_slab_rows(k, steps):
    return next((r for r in range(16, k + 1, 16) if k % r == 0 and k // r <= steps), None)


def _mixers(proj3, small3, conv_state, s0, c0, n0, m0, conv_w, par, dn_norm_g, ml_norm_g,
            *, R, T, N, off_i, off_f, casts=()):
    B, Hd, DK, DV = s0.shape
    _, Hm, MDK, MDV = c0.shape
    QKV = conv_state.shape[-1]
    VW, MQK, MVW = Hd * DV, Hm * MDK, Hm * MDV
    width = conv_w.shape[0]
    col0 = QKV + VW
    assert QKV == 2 * Hd * DK + VW and QKV % VW == 0 and VW % LANES == 0
    assert col0 % MQK == 0 and (col0 + 2 * MQK) % MVW == 0 and MQK % LANES == 0
    rows = proj3.shape[0] * T
    qb, vb = col0 // MQK, (col0 + 2 * MQK) // MVW
    kern = functools.partial(_mixers_kernel, n_cast=len(casts), R=R, T=T, Hd=Hd, Hm=Hm, DK=DK, DV=DV,
                             MDK=MDK, MDV=MDV, width=width, off_i=off_i, off_f=off_f)
    tok = lambda w, c: pl.BlockSpec((R, T, w), lambda b, n: (b * N + n, 0, c))
    const = lambda shape: pl.BlockSpec(shape, lambda b, n: (0,) * len(shape))
    per_seq = lambda *dims: pl.BlockSpec((R,) + dims, lambda b, n: (b,) + (0,) * len(dims))
    state_specs = [per_seq(Hd, DK, DV), per_seq(Hm, MDK, MDV), per_seq(Hm, MDK), per_seq(1, Hm)]

    def slab(w):
        r = _cast_slab_rows(w.shape[0], (B // R) * N)
        return pl.BlockSpec((r, w.shape[1]), lambda b, n: (jnp.minimum(b * N + n, w.shape[0] // r - 1), 0))

    cast_specs = [slab(w) for w in casts]
    odn, s_new, oml, c_new, n_new, m_new, *cast = pl.pallas_call(
        kern,
        grid=(B // R, N),
        in_specs=[
            tok(QKV, 0), tok(VW, QKV // VW), tok(MQK, qb), tok(MQK, qb + 1), tok(MVW, vb),
            tok(MVW, vb + 1), tok(LANES, 0),
            per_seq(width - 1, QKV), *state_specs,
            const((width, QKV)), const((SUBLANES, LANES)), const((1, DV)), const((1, MVW)),
            *cast_specs,
        ],
        out_specs=[pl.BlockSpec((R * T, VW), lambda b, n: (b * N + n, 0)), state_specs[0],
                   pl.BlockSpec((R * T, MVW), lambda b, n: (b * N + n, 0)), *state_specs[1:],
                   *cast_specs],
        out_shape=[jax.ShapeDtypeStruct((rows, VW), BF16), jax.ShapeDtypeStruct(s0.shape, F32),
                   jax.ShapeDtypeStruct((rows, MVW), BF16), jax.ShapeDtypeStruct(c0.shape, F32),
                   jax.ShapeDtypeStruct(n0.shape, F32), jax.ShapeDtypeStruct(m0.shape, F32),
                   *[jax.ShapeDtypeStruct(w.shape, BF16) for w in casts]],
        scratch_shapes=[pltpu.VMEM((R, SUBLANES + T, QKV), F32)],
        compiler_params=_cparams(("arbitrary", "arbitrary")),
        name="mixers",
    )(proj3, proj3, proj3, proj3, proj3, proj3, small3, conv_state, s0, c0, n0, m0, conv_w, par,
      dn_norm_g.reshape(1, DV), ml_norm_g.reshape(1, MVW), *casts)
    return odn, s_new, oml, c_new, n_new, m_new, cast


def _outproj_kernel(x_ref, a_ref, b_ref, wa_ref, wb_ref, o_ref):
    o_ref[...] = (x_ref[...] + jnp.dot(a_ref[...], wa_ref[...], preferred_element_type=F32)
                  + jnp.dot(b_ref[...], wb_ref[...], preferred_element_type=F32))


def _outproj(x, a, b, w_out):
    m, d = x.shape
    ka, kb = a.shape[1], b.shape[1]
    assert ka == kb
    tm = _pick(m, (512, 256, 128, 64, 32, 16, 8))
    return pl.pallas_call(
        _outproj_kernel,
        grid=(m // tm,),
        in_specs=[
            pl.BlockSpec((tm, d), lambda i: (i, 0)),
            pl.BlockSpec((tm, ka), lambda i: (i, 0)),
            pl.BlockSpec((tm, kb), lambda i: (i, 0)),
            pl.BlockSpec((ka, d), lambda i: (0, 0), pipeline_mode=pl.Buffered(1)),
            pl.BlockSpec((kb, d), lambda i: (1, 0), pipeline_mode=pl.Buffered(1)),
        ],
        out_specs=pl.BlockSpec((tm, d), lambda i: (i, 0)),
        out_shape=jax.ShapeDtypeStruct((m, d), F32),
        compiler_params=_cparams(("parallel",)),
        name="outproj",
    )(x, a, b, w_out, w_out)


def _ffn_kernel(x_ref, g_ref, wg_ref, wu_ref, cw_ref, cb_ref, wd_ref, stg_ref, stu_ref, gf_ref,
                y_ref, nsg_ref, nsu_ref, h_ref, act_ref, ext_ref, *carry,
                tm, tf, ff, shift, pad, tps, width, nj, wide_state, final_norm):
    i = pl.program_id(0)
    j = pl.program_id(1)
    hist = (width - 1) * shift

    def activate_tile():
        conv = []
        for idx, (w_ref, st_ref, ns_ref) in enumerate(
                ((wg_ref, stg_ref, nsg_ref), (wu_ref, stu_ref, nsu_ref))):
            col = pl.ds(pl.multiple_of(idx * ff + j * tf, LANES), tf)
            st_col = col if wide_state else slice(None)
            u = jnp.dot(h_ref[...], w_ref[...], preferred_element_type=F32)
            ext_ref[idx, pad:pad + tm, :] = u
            if tps == 1:
                ext_ref[idx, pad - hist:pad, :] = st_ref[0, :, st_col]
            else:
                ext_ref[idx, pad - hist:pad, :] = jnp.where(i % tps == 0, st_ref[0, :, st_col],
                                                            carry[0][j, idx])
            cw = cw_ref[:, col]
            c = u * cw[width - 1:width, :] + cb_ref[:, col]
            for t in range(width - 1):
                off = pad - hist + t * shift
                c = c + ext_ref[idx, off:off + tm, :] * cw[t:t + 1, :]
            new = u[tm - hist:, :]
            ns_ref[0, :, st_col] = new
            if tps != 1:
                carry[0][j, idx] = new
            conv.append(c)
        act_ref[...] = (conv[0] * _sigmoid(conv[0]) * conv[1]).astype(BF16)

    def down_tile():
        return jnp.dot(act_ref[...], wd_ref[...], preferred_element_type=F32)

    @pl.when(j == 0)
    def _():
        h_ref[...] = _rms(x_ref[...], g_ref[...]).astype(BF16)
        y_ref[...] = jnp.zeros_like(y_ref)
        activate_tile()

    @pl.when((j > 0) & (j < nj))
    def _():
        y_ref[...] += down_tile()
        activate_tile()

    @pl.when(j == nj)
    def _():
        xo = x_ref[...] + y_ref[...] + down_tile()
        y_ref[...] = _rms(xo, gf_ref[...]) if final_norm else xo


def _ffn(x, norm_g, w_up, conv_w, conv_b, w_down, state, final_g, *, tm, shift, tps, final_norm):
    m, d = x.shape
    ff = w_down.shape[0]
    width = conv_w.shape[0]
    hist = (width - 1) * shift
    pad = -(-hist // SUBLANES) * SUBLANES
    tf = _pick(ff, (512, 256, 128))
    nj = ff // tf
    n_tiles = m // tm
    wide_state = hist * 2 * ff * 4 <= WIDE_STATE_BYTES
    kern = functools.partial(_ffn_kernel, tm=tm, tf=tf, ff=ff, shift=shift, pad=pad, tps=tps,
                             width=width, nj=nj, wide_state=wide_state, final_norm=final_norm)
    scratch = [pltpu.VMEM((tm, d), BF16), pltpu.VMEM((tm, tf), BF16), pltpu.VMEM((2, pad + tm, tf), F32)]
    cur = lambda j: jnp.minimum(j, nj - 1)
    prv = lambda j: jnp.maximum(j - 1, 0)
    if tps != 1:
        scratch.append(pltpu.VMEM((nj, 2, hist, tf), F32))
    if wide_state:
        st_in = [pl.BlockSpec((1, hist, 2 * ff), lambda i, j: (i // tps, 0, 0))] * 2
        st_out = [pl.BlockSpec((1, hist, 2 * ff), lambda i, j: (i, 0, 0))]
        st_shape = [jax.ShapeDtypeStruct((n_tiles, hist, 2 * ff), F32)]
    else:
        st_in = [pl.BlockSpec((1, hist, tf), lambda i, j: (i // tps, 0, cur(j))),
                 pl.BlockSpec((1, hist, tf), lambda i, j: (i // tps, 0, nj + cur(j)))]
        st_out = [pl.BlockSpec((1, hist, tf), lambda i, j: (i, 0, cur(j)))] * 2
        st_shape = [jax.ShapeDtypeStruct((n_tiles, hist, ff), F32)] * 2
    call = pl.pallas_call(
        kern if not wide_state else _single_history(kern),
        grid=(n_tiles, nj + 1),
        in_specs=[
            pl.BlockSpec((tm, d), lambda i, j: (i, 0), pipeline_mode=pl.Buffered(1)),
            pl.BlockSpec((1, d), lambda i, j: (0, 0)),
            pl.BlockSpec((d, tf), lambda i, j: (0, cur(j))),
            pl.BlockSpec((d, tf), lambda i, j: (0, nj + cur(j))),
            pl.BlockSpec((width, 2 * ff), lambda i, j: (0, 0)),
            pl.BlockSpec((1, 2 * ff), lambda i, j: (0, 0)),
            pl.BlockSpec((tf, d), lambda i, j: (prv(j), 0)),
            *st_in,
            pl.BlockSpec((1, d), lambda i, j: (0, 0)),
        ],
        out_specs=[pl.BlockSpec((tm, d), lambda i, j: (i, 0)), *st_out],
        out_shape=[jax.ShapeDtypeStruct((m, d), F32), *st_shape],
        scratch_shapes=scratch,
        compiler_params=_cparams(("arbitrary", "arbitrary")),
        name="convffn",
    )
    y, *ns = call(x, norm_g.reshape(1, d), w_up, w_up, conv_w, conv_b.reshape(1, 2 * ff), w_down,
                  state, state, final_g.reshape(1, d))
    return y, [a[tps - 1::tps] for a in ns]


def _single_history(kern):
    def wrapped(*refs):
        ins, (y_ref, ns_ref), scratch = refs[:10], refs[10:12], refs[12:]
        return kern(*ins, y_ref, ns_ref, ns_ref, *scratch)
    return wrapped


def _layer(x, st, p, final_g, final_norm, decode):
    conv_buf, s0, c0, n0, m0, ffn_buf = st
    B, L, D = x.shape
    Hd, Hm = s0.shape[1], c0.shape[1]
    dn_qkv = conv_buf.shape[-1]
    dn_vw = Hd * s0.shape[3]
    ff = p["w_down"].shape[0]

    T = next((c for c in (MIX_CHUNK, CHUNK) if L % c == 0), L)
    N = L // T
    R = _pick(B, (8, 4, 2, 1)) if decode else 1
    if R * T > 128:
        R = 1

    x2 = x.reshape(B * L, D)
    big, small = _inproj(x2, p["norm_mix_g"], p["w_big"], p["w_small"])
    big3 = big.reshape(B * N, T, big.shape[1])
    small3 = small.reshape(B * N, T, LANES)

    pending = [k for k in ("w_out", "w_up", "w_down") if p[k].dtype != BF16]
    if not all(_cast_slab_rows(p[k].shape[0], (B // R) * N) for k in pending):
        p = dict(p, **{k: p[k].astype(BF16) for k in pending})
        pending = []
    o_dn, s_new, o_ml, c_new, n_new, m_new, rounded = _mixers(
        big3, small3, conv_buf, s0, c0, n0, m0.reshape(B, 1, Hm), p["dn_conv_w"], p["par"],
        p["dn_norm_g"], p["ml_norm_g"], R=R, T=T, N=N, off_i=2 * Hd, off_f=2 * Hd + Hm,
        casts=tuple(p[k] for k in pending))
    p = dict(p, **dict(zip(pending, rounded)))
    width = p["dn_conv_w"].shape[0]
    big4 = big.reshape(B, L, big.shape[1])
    if L >= width - 1:
        conv_new = big4[:, L - (width - 1):, :dn_qkv]
    else:
        conv_new = jnp.concatenate([conv_buf, big4[:, :, :dn_qkv]], axis=1)[:, L:]

    x1 = _outproj(x2, o_dn, o_ml, p["w_out"])

    fw = p["ffn_conv_w"].shape[0]
    if decode:
        x1t = x1.reshape(B, L, D).transpose(1, 0, 2).reshape(L * B, D)
        state = ffn_buf.transpose(1, 0, 2).reshape(1, (fw - 1) * B, 2 * ff)
        y, parts = _ffn(x1t, p["norm_ffn_g"], p["w_up"], p["ffn_conv_w"], p["ffn_conv_b"],
                        p["w_down"], state, final_g, tm=L * B, shift=B, tps=1, final_norm=final_norm)
        y = y.reshape(L, B, D).transpose(1, 0, 2)
        ffn_new = jnp.concatenate(
            [a.reshape(fw - 1, B, a.shape[-1]).transpose(1, 0, 2) for a in parts], axis=-1)
    else:
        tm = _pick(L, (1024, 512, 256, 128, 64, 32, 16, 8))
        y, parts = _ffn(x1, p["norm_ffn_g"], p["w_up"], p["ffn_conv_w"], p["ffn_conv_b"],
                        p["w_down"], ffn_buf, final_g, tm=tm, shift=1, tps=L // tm,
                        final_norm=final_norm)
        y = y.reshape(B, L, D)
        ffn_new = jnp.concatenate(parts, axis=-1)
    return y, (conv_new, s_new, c_new, n_new, m_new.reshape(B, Hm), ffn_new), p


def _regroup_kernel(w_ref, big_ref, small_ref, *, c0, c1, c2):
    n = w_ref.shape[0]
    big_ref[:c0, :] = w_ref[:c0, :].astype(BF16)
    big_ref[c0:, :] = w_ref[c1:c2, :].astype(BF16)
    narrow = jnp.concatenate(
        [w_ref[c0:c1, :], w_ref[c2:, :], jnp.zeros((LANES - (c1 - c0) - (n - c2), w_ref.shape[1]), F32)],
        axis=0)
    small_ref[...] = narrow.astype(BF16)


def _regroup(w_t, c0, c1, c2):
    n, k = w_t.shape
    tk = _pick(k, (256, 128))
    nb = c0 + c2 - c1
    assert c0 % 16 == 0 and c1 % SUBLANES == 0 and c2 % SUBLANES == 0
    return pl.pallas_call(
        functools.partial(_regroup_kernel, c0=c0, c1=c1, c2=c2),
        grid=(k // tk,),
        in_specs=[pl.BlockSpec((n, tk), lambda i: (0, i))],
        out_specs=[pl.BlockSpec((nb, tk), lambda i: (0, i)), pl.BlockSpec((LANES, tk), lambda i: (0, i))],
        out_shape=[jax.ShapeDtypeStruct((nb, k), BF16), jax.ShapeDtypeStruct((LANES, k), BF16)],
        compiler_params=_cparams(("parallel",)),
        name="regroup",
    )(w_t)


def _prep_params(l, Hd, Hm, dn_qkv, dn_vw, ml_qk, ml_vw, norm_mix_g, w_in, dn_conv_w, dn_A_log,
                 dn_dt_bias, dn_norm_g, ml_i_bias, ml_f_bias, ml_norm_g, w_out, norm_ffn_g, w_up,
                 ffn_conv_w, ffn_conv_b, w_down):
    w = w_in[l]
    c0 = dn_qkv + dn_vw
    c1 = c0 + 2 * Hd
    c2 = c1 + 2 * ml_qk + 2 * ml_vw
    nsmall = 2 * Hd + 2 * Hm
    assert nsmall <= LANES and w.shape[1] == c2 + 2 * Hm
    w_big, w_small = _regroup(w.T, c0, c1, c2)
    z = lambda n: jnp.zeros((n,), F32)
    bias = jnp.concatenate([z(Hd), dn_dt_bias[l], ml_i_bias[l], ml_f_bias[l], z(LANES - nsmall)])
    alog = jnp.concatenate([z(Hd), dn_A_log[l], z(LANES - 2 * Hd)])
    par = jnp.zeros((SUBLANES, LANES), F32).at[0].set(bias).at[1].set(alog)
    return dict(norm_mix_g=norm_mix_g[l], w_big=w_big, w_small=w_small, dn_conv_w=dn_conv_w[l],
                par=par, dn_norm_g=dn_norm_g[l], ml_norm_g=ml_norm_g[l], w_out=w_out[l],
                norm_ffn_g=norm_ffn_g[l], w_up=w_up[l], ffn_conv_w=ffn_conv_w[l],
                ffn_conv_b=ffn_conv_b[l], w_down=w_down[l])


def kernel(x_prompt, x_sample, state_dn_conv, state_dn_S, state_ml_C, state_ml_n, state_ml_m,
           state_ffn_conv, norm_mix_g, w_in, dn_conv_w, dn_A_log, dn_dt_bias, dn_norm_g,
           ml_i_bias, ml_f_bias, ml_norm_g, w_out, norm_ffn_g, w_up, ffn_conv_w, ffn_conv_b,
           w_down, norm_final_g):
    states = (state_dn_conv, state_dn_S, state_ml_C, state_ml_n, state_ml_m, state_ffn_conv)
    depth = w_in.shape[0]
    batch = x_prompt.shape[0]
    Hd, dk, dv = state_dn_S.shape[2:]
    Hm, mdk, mdv = state_ml_C.shape[2:]
    xp, xs = x_prompt, x_sample
    p_new = [[] for _ in states]
    s_new = [[] for _ in states]
    for l in range(depth):
        p = _prep_params(l, Hd, Hm, state_dn_conv.shape[-1], Hd * dv, Hm * mdk, Hm * mdv,
                         norm_mix_g, w_in, dn_conv_w, dn_A_log, dn_dt_bias, dn_norm_g, ml_i_bias,
                         ml_f_bias, ml_norm_g, w_out, norm_ffn_g, w_up, ffn_conv_w, ffn_conv_b,
                         w_down)
        last = l == depth - 1
        st_p = tuple(jnp.zeros((batch,) + s.shape[2:], s.dtype) for s in states)
        st_s = tuple(s[l] for s in states)
        xp, np_st, p = _layer(xp, st_p, p, norm_final_g, last, decode=False)
        xs, ns_st, p = _layer(xs, st_s, p, norm_final_g, last, decode=True)
        for i in range(len(states)):
            p_new[i].append(np_st[i])
            s_new[i].append(ns_st[i])
    outs_p = tuple(jnp.stack(a, axis=0) for a in p_new)
    outs_s = tuple(jnp.stack(a, axis=0) for a in s_new)
    return (xp, xs) + outs_p + outs_s
```

```python
import functools

import jax
import jax.numpy as jnp
from jax import lax
from jax.experimental import pallas as pl
from jax.experimental.pallas import tpu as pltpu

EPS = 1e-6
CHUNK = 64
MIX_CHUNK = 128
F32 = jnp.float32
BF16 = jnp.bfloat16
LANES = 128
SUBLANES = 8
VMEM_LIMIT = 56 * 1024 * 1024
WIDE_STATE_BYTES = 2 * 1024 * 1024


def _cparams(sem):
    return pltpu.CompilerParams(dimension_semantics=sem, vmem_limit_bytes=VMEM_LIMIT)


def _dot(a, b):
    return jnp.dot(a.astype(BF16), b.astype(BF16), preferred_element_type=F32)


def _dot_nt(a, b):
    return lax.dot_general(a.astype(BF16), b.astype(BF16), (((1,), (1,)), ((), ())),
                           preferred_element_type=F32)


def _dot_tn(a, b):
    return lax.dot_general(a.astype(BF16), b.astype(BF16), (((0,), (0,)), ((), ())),
                           preferred_element_type=F32)


def _mask_dot_exact(mask_bf16, x):
    hi = x.astype(BF16)
    r1 = x - hi.astype(F32)
    mid = r1.astype(BF16)
    lo = (r1 - mid.astype(F32)).astype(BF16)
    d = functools.partial(jnp.dot, preferred_element_type=F32)
    return (d(mask_bf16, hi) + d(mask_bf16, mid)) + d(mask_bf16, lo)


def _softplus(x):
    return jnp.maximum(x, 0.0) + jnp.log1p(jnp.exp(-jnp.abs(x)))


def _sigmoid(x):
    return 1.0 / (1.0 + jnp.exp(-x))


def _rms(x, g):
    return x * lax.rsqrt(jnp.mean(x * x, axis=-1, keepdims=True) + EPS) * g


def _log2(n):
    assert n & (n - 1) == 0 and n > 0, n
    return n.bit_length() - 1


def _pick(n, cands):
    for c in cands:
        if n % c == 0:
            return c
    return n


def _inproj_kernel(x_ref, g_ref, wb_ref, ws_ref, ob_ref, os_ref, h_ref):
    nt = functools.partial(lax.dot_general, dimension_numbers=(((1,), (1,)), ((), ())),
                           preferred_element_type=F32)

    @pl.when(pl.program_id(1) == 0)
    def _():
        h = _rms(x_ref[...], g_ref[...]).astype(BF16)
        h_ref[...] = h
        os_ref[...] = nt(h, ws_ref[...])

    ob_ref[...] = nt(h_ref[...], wb_ref[...])


def _inproj(x, g, w_big_t, w_small_t):
    m, d = x.shape
    nb = w_big_t.shape[0]
    tm = _pick(m, (1024, 512, 256, 128, 64, 32, 16, 8))
    tn = _pick(nb, (1792, 1024, 512, 256, 128))
    return pl.pallas_call(
        _inproj_kernel,
        grid=(m // tm, nb // tn),
        in_specs=[
            pl.BlockSpec((tm, d), lambda i, j: (i, 0)),
            pl.BlockSpec((1, d), lambda i, j: (0, 0)),
            pl.BlockSpec((tn, d), lambda i, j: (j, 0)),
            pl.BlockSpec((LANES, d), lambda i, j: (0, 0)),
        ],
        out_specs=[
            pl.BlockSpec((tm, tn), lambda i, j: (i, j)),
            pl.BlockSpec((tm, LANES), lambda i, j: (i, 0)),
        ],
        out_shape=[jax.ShapeDtypeStruct((m, nb), F32), jax.ShapeDtypeStruct((m, LANES), F32)],
        scratch_shapes=[pltpu.VMEM((tm, d), BF16)],
        compiler_params=_cparams(("parallel", "arbitrary")),
        name="inproj",
    )(x, g.reshape(1, d), w_big_t, w_small_t)


def _seq_masks(R, T):
    RT = R * T
    sh = _log2(T)
    ri = lax.broadcasted_iota(jnp.int32, (RT, RT), 0)
    ci = lax.broadcasted_iota(jnp.int32, (RT, RT), 1)
    same = (ri >> sh) == (ci >> sh)
    tril = same & (ci <= ri)
    strict = same & (ci < ri)
    eye = ri == ci
    lastsel = same & ((ci & (T - 1)) == T - 1)
    rowseq = lax.broadcasted_iota(jnp.int32, (RT, 1), 0) >> sh
    return ri, ci, tril, strict, eye, lastsel, rowseq


def _row_form(col, eye):
    return jnp.sum(jnp.where(eye, col, 0.0), axis=0, keepdims=True)


def _at_last(row, lastsel):
    return jnp.sum(jnp.where(lastsel, row, 0.0), axis=1, keepdims=True)


def _unit_lower_inverses(lms, ri, ci, T):
    ds = None
    s = 1
    while s < T:
        sh = _log2(2 * s)
        blk = ((ri >> sh) == (ci >> sh)) & ((ri & (2 * s - 1)) >= s) & ((ci & (2 * s - 1)) < s)
        ms = [jnp.where(blk, lm, 0.0) for lm in lms]
        if ds is None:
            eye = jnp.where(ri == ci, 1.0, 0.0)
            ds = [eye - m for m in ms]
        else:
            dms = [_dot(d, m) for d, m in zip(ds, ms)]
            yield
            ds = [d - _dot(dm, d) for d, dm in zip(ds, dms)]
        yield
        s *= 2
    return ds


def _causal_conv(ext_ref, x_ref, cw_ref, T, width, cols):
    lo = SUBLANES - (width - 1)
    ext_ref[:, SUBLANES:SUBLANES + T, cols] = x_ref[:, :, cols]
    cw = cw_ref[:, cols]
    acc = None
    for j in range(width):
        term = ext_ref[:, lo + j:lo + j + T, cols] * cw[j:j + 1, :]
        acc = term if acc is None else acc + term
    ext_ref[:, lo:SUBLANES, cols] = ext_ref[:, lo + T:SUBLANES + T, cols]
    return acc


def _gdn_steps(qkv_ref, z_ref, sm_ref, cw_ref, par_ref, ng_ref, o_ref, s_ref, ext_ref, masks,
               *, R, T, H, DK, DV, width):
    RT = R * T
    QK = H * DK

    groups = []
    for c0, c1 in ((0, QK), (QK, 2 * QK), (2 * QK, 2 * QK + H * DV)):
        a = _causal_conv(ext_ref, qkv_ref, cw_ref, T, width, slice(c0, c1)).reshape(RT, c1 - c0)
        groups.append(a * _sigmoid(a))
        if R > 1:
            yield
    yq, yk, yv = groups
    if R == 1:
        yield
    z = z_ref[...].reshape(RT, H * DV)

    sm = sm_ref[...].reshape(RT, LANES)
    beta_all = _sigmoid(sm)
    g_all = -jnp.exp(par_ref[1:2, :]) * _softplus(sm + par_ref[0:1, :])

    ri, ci, tril, strict, eye, lastsel, rowseq = masks
    g_cum = _mask_dot_exact(jnp.where(tril, 1.0, 0.0).astype(BF16), g_all)
    yield

    hs = range(H)
    q = [yq[:, h * DK:(h + 1) * DK] for h in hs]
    k = [yk[:, h * DK:(h + 1) * DK] for h in hs]
    v = [yv[:, h * DV:(h + 1) * DV] for h in hs]
    q = [a * lax.rsqrt(jnp.sum(a * a, axis=-1, keepdims=True) + EPS) * (DK ** -0.5) for a in q]
    yield
    k = [a * lax.rsqrt(jnp.sum(a * a, axis=-1, keepdims=True) + EPS) for a in k]
    yield
    beta = [beta_all[:, h:h + 1] for h in hs]
    gc = [g_cum[:, H + h:H + h + 1] for h in hs]
    gr = [_row_form(a, eye) for a in gc]
    if R == 1:
        glast = [a[RT - 1:RT, :] for a in gc]
    else:
        glast = [_at_last(a, lastsel) for a in gr]
    yield
    decay = [jnp.exp(jnp.where(tril, c - r, -jnp.inf)) for c, r in zip(gc, gr)]
    kb = [a * b for a, b in zip(k, beta)]
    yield
    lm = [jnp.where(strict, _dot_nt(a, b) * d, 0.0) for a, b, d in zip(kb, k, decay)]
    yield
    tinv = yield from _unit_lower_inverses(lm, ri, ci, T)
    eg = [jnp.exp(a) for a in gc]
    uw = [_dot(t, jnp.concatenate([a * b, c * e], axis=1))
          for t, a, b, c, e in zip(tinv, v, beta, kb, eg)]
    yield
    u = [a[:, :DV] for a in uw]
    w = [a[:, DV:] for a in uw]
    qk = [_dot_nt(a, b) * d for a, b, d in zip(q, k, decay)]
    yield
    qe = [a * e for a, e in zip(q, eg)]
    kd = [a * jnp.exp(l - c) for a, l, c in zip(k, glast, gc)]
    gl = [jnp.exp(a) for a in glast]
    yield

    if R == 1:
        s_old = [s_ref[0, h] for h in hs]
        ws = [_dot(jnp.concatenate([a, b], axis=0), c) for a, b, c in zip(w, qe, s_old)]
        yield
        v_new = [a - b[:RT] for a, b in zip(u, ws)]
        o = [a[RT:] + _dot(b, c) for a, b, c in zip(ws, qk, v_new)]
        yield
        for h in hs:
            s_ref[0, h] = s_old[h] * gl[h][0:1, :] + _dot_tn(kd[h], v_new[h])
        yield
    else:
        ws = [[_dot(jnp.concatenate([w[h][s * T:(s + 1) * T], qe[h][s * T:(s + 1) * T]], axis=0),
                    s_ref[s, h]) for s in range(R)] for h in hs]
        yield
        v_new = [u[h] - jnp.concatenate([a[:T] for a in ws[h]], axis=0) for h in hs]
        o = [jnp.concatenate([a[T:] for a in ws[h]], axis=0) + _dot(qk[h], v_new[h]) for h in hs]
        yield
        for h in hs:
            for s in range(R):
                kd_s = jnp.where(rowseq == s, kd[h], 0.0)
                s_ref[s, h] = s_ref[s, h] * gl[h][s * T:s * T + 1, :] + _dot_tn(kd_s, v_new[h])
            yield

    for h in hs:
        zh = z[:, h * DV:(h + 1) * DV]
        out = _rms(o[h], ng_ref[...]) * (zh * _sigmoid(zh))
        o_ref[:, h * DV:(h + 1) * DV] = out.astype(o_ref.dtype)
    yield


def _mlstm_steps(q_ref, k_ref, v_ref, og_ref, sm_ref, par_ref, ng_ref, o_ref, c_ref, n_ref, m_ref,
                 masks, *, R, T, H, DK, DV, off_i, off_f):
    RT = R * T
    ri, ci, tril, strict, eye, lastsel, rowseq = masks

    pre = sm_ref[...].reshape(RT, LANES) + par_ref[0:1, :]
    logf_all = -_softplus(-pre)
    b_all = _mask_dot_exact(jnp.where(tril, 1.0, 0.0).astype(BF16), logf_all)
    yield

    qa = q_ref[...].reshape(RT, H * DK)
    ka = k_ref[...].reshape(RT, H * DK)
    va = v_ref[...].reshape(RT, H * DV)
    oa = og_ref[...].reshape(RT, H * DV)

    hs = range(H)
    q = [qa[:, h * DK:(h + 1) * DK] * (DK ** -0.5) for h in hs]
    k = [ka[:, h * DK:(h + 1) * DK] for h in hs]
    v = [va[:, h * DV:(h + 1) * DV] for h in hs]
    bc = [b_all[:, off_f + h:off_f + h + 1] for h in hs]
    ic = [pre[:, off_i + h:off_i + h + 1] for h in hs]
    dm = [jnp.where(tril, b + _row_form(i - b, eye), -jnp.inf) for b, i in zip(bc, ic)]
    yield

    if R == 1:
        m_rows = [m_ref[0, :, h:h + 1] for h in hs]
    else:
        m_rows = []
        for h in hs:
            mr = jnp.zeros((RT, 1), F32)
            for s in range(R):
                mr = jnp.where(rowseq == s, m_ref[s, :, h:h + 1], mr)
            m_rows.append(mr)
    m_new = [jnp.maximum(b + m, jnp.max(d, axis=1, keepdims=True)) for b, m, d in zip(bc, m_rows, dm)]
    inter = [jnp.exp(b + m - mn) for b, m, mn in zip(bc, m_rows, m_new)]
    yield
    smat = [_dot_nt(a, b) * jnp.exp(d - mn) for a, b, d, mn in zip(q, k, dm, m_new)]
    yield

    if R == 1:
        qc = [_dot(q[h], c_ref[0, h]) for h in hs]
        qn = [jnp.sum(q[h] * n_ref[0, h:h + 1, :], axis=1, keepdims=True) for h in hs]
    else:
        qc, qn = [], []
        for h in hs:
            rows = [slice(s * T, (s + 1) * T) for s in range(R)]
            qc.append(jnp.concatenate([_dot(q[h][r], c_ref[s, h]) for s, r in enumerate(rows)], axis=0))
            qn.append(jnp.concatenate(
                [jnp.sum(q[h][r] * n_ref[s, h:h + 1, :], axis=1, keepdims=True)
                 for s, r in enumerate(rows)], axis=0))
            yield
    yield
    num = [i * c + _dot(s, a) for i, c, s, a in zip(inter, qc, smat, v)]
    den = [i * n + jnp.sum(s, axis=1, keepdims=True) for i, n, s in zip(inter, qn, smat)]
    yield
    hh = [a / jnp.maximum(jnp.abs(b), jnp.exp(-mn)) for a, b, mn in zip(num, den, m_new)]

    if R == 1:
        b_last = [b[RT - 1:RT, :] for b in bc]
        m_end = [mn[RT - 1:RT, :] for mn in m_new]
    else:
        b_last = [_at_last(_row_form(b, eye), lastsel) for b in bc]
        m_end = [_at_last(_row_form(mn, eye), lastsel) for mn in m_new]
    yield
    kw = [a * jnp.exp(bl - b + i - me) for a, bl, b, i, me in zip(k, b_last, bc, ic, m_end)]
    cs = [jnp.exp(bl + m - me) for bl, m, me in zip(b_last, m_rows, m_end)]
    yield
    for h in hs:
        for s in range(R):
            kw_s = kw[h] if R == 1 else jnp.where(rowseq == s, kw[h], 0.0)
            cs_s = cs[h][s * T:s * T + 1, :]
            c_ref[s, h] = cs_s * c_ref[s, h] + _dot_tn(kw_s, v[h])
            n_ref[s, h:h + 1, :] = cs_s * n_ref[s, h:h + 1, :] + jnp.sum(kw_s, axis=0, keepdims=True)
            m_ref[s, :, h:h + 1] = m_end[h][s * T:s * T + 1, :]
        yield

    for h in hs:
        og = oa[:, h * DV:(h + 1) * DV]
        out = _rms(hh[h], ng_ref[:, h * DV:(h + 1) * DV]) * _sigmoid(og)
        o_ref[:, h * DV:(h + 1) * DV] = out.astype(o_ref.dtype)
    yield


def _cast_steps(pairs):
    for src_ref, dst_ref in pairs:
        dst_ref[...] = src_ref[...].astype(BF16)
        yield


def _mixers_kernel(*refs, n_cast, R, T, Hd, Hm, DK, DV, MDK, MDV, width, off_i, off_f):
    (qkv_ref, z_ref, mq_ref, mk_ref, mv_ref, mo_ref, sm_ref, cst_ref, s0_ref, c0_ref, n0_ref, m0_ref,
     cw_ref, par_ref, dng_ref, mng_ref) = refs[:16]
    cast_in = refs[16:16 + n_cast]
    odn_ref, s_ref, oml_ref, c_ref, n_ref, m_ref = refs[16 + n_cast:22 + n_cast]
    cast_out = refs[22 + n_cast:22 + 2 * n_cast]
    ext_ref, = refs[22 + 2 * n_cast:]

    @pl.when(pl.program_id(1) == 0)
    def _():
        ext_ref[:, SUBLANES - (width - 1):SUBLANES, :] = cst_ref[...]
        s_ref[...] = s0_ref[...]
        c_ref[...] = c0_ref[...]
        n_ref[...] = n0_ref[...]
        m_ref[...] = m0_ref[...]

    masks = _seq_masks(R, T)
    streams = [
        _gdn_steps(qkv_ref, z_ref, sm_ref, cw_ref, par_ref, dng_ref, odn_ref, s_ref, ext_ref, masks,
                   R=R, T=T, H=Hd, DK=DK, DV=DV, width=width),
        _mlstm_steps(mq_ref, mk_ref, mv_ref, mo_ref, sm_ref, par_ref, mng_ref, oml_ref, c_ref, n_ref,
                     m_ref, masks, R=R, T=T, H=Hm, DK=MDK, DV=MDV, off_i=off_i, off_f=off_f),
        _cast_steps(list(zip(cast_in, cast_out))),
    ]
    turns = [2, 1, 1]
    while streams:
        for gen, n in list(zip(streams, turns)):
            for _ in range(n):
                if gen in streams and next(gen, "done") == "done":
                    turns.pop(streams.index(gen))
                    streams.remove(gen)


def _cast_slab_rows(k, steps):
    return next((r for r in range(16, k + 1, 16) if k % r == 0 and k // r <= steps), None)


def _mixers(proj3, small3, conv_state, s0, c0, n0, m0, conv_w, par, dn_norm_g, ml_norm_g,
            *, R, T, N, off_i, off_f, casts=()):
    B, Hd, DK, DV = s0.shape
    _, Hm, MDK, MDV = c0.shape
    QKV = conv_state.shape[-1]
    VW, MQK, MVW = Hd * DV, Hm * MDK, Hm * MDV
    width = conv_w.shape[0]
    col0 = QKV + VW
    assert QKV == 2 * Hd * DK + VW and QKV % VW == 0 and VW % LANES == 0
    assert col0 % MQK == 0 and (col0 + 2 * MQK) % MVW == 0 and MQK % LANES == 0
    rows = proj3.shape[0] * T
    qb, vb = col0 // MQK, (col0 + 2 * MQK) // MVW
    kern = functools.partial(_mixers_kernel, n_cast=len(casts), R=R, T=T, Hd=Hd, Hm=Hm, DK=DK, DV=DV,
                             MDK=MDK, MDV=MDV, width=width, off_i=off_i, off_f=off_f)
    tok = lambda w, c: pl.BlockSpec((R, T, w), lambda b, n: (b * N + n, 0, c))
    const = lambda shape: pl.BlockSpec(shape, lambda b, n: (0,) * len(shape))
    per_seq = lambda *dims: pl.BlockSpec((R,) + dims, lambda b, n: (b,) + (0,) * len(dims))
    state_specs = [per_seq(Hd, DK, DV), per_seq(Hm, MDK, MDV), per_seq(Hm, MDK), per_seq(1, Hm)]

    def slab(w):
        r = _cast_slab_rows(w.shape[0], (B // R) * N)
        return pl.BlockSpec((r, w.shape[1]), lambda b, n: (jnp.minimum(b * N + n, w.shape[0] // r - 1), 0))

    cast_specs = [slab(w) for w in casts]
    odn, s_new, oml, c_new, n_new, m_new, *cast = pl.pallas_call(
        kern,
        grid=(B // R, N),
        in_specs=[
            tok(QKV, 0), tok(VW, QKV // VW), tok(MQK, qb), tok(MQK, qb + 1), tok(MVW, vb),
            tok(MVW, vb + 1), tok(LANES, 0),
            per_seq(width - 1, QKV), *state_specs,
            const((width, QKV)), const((SUBLANES, LANES)), const((1, DV)), const((1, MVW)),
            *cast_specs,
        ],
        out_specs=[pl.BlockSpec((R * T, VW), lambda b, n: (b * N + n, 0)), state_specs[0],
                   pl.BlockSpec((R * T, MVW), lambda b, n: (b * N + n, 0)), *state_specs[1:],
                   *cast_specs],
        out_shape=[jax.ShapeDtypeStruct((rows, VW), BF16), jax.ShapeDtypeStruct(s0.shape, F32),
                   jax.ShapeDtypeStruct((rows, MVW), BF16), jax.ShapeDtypeStruct(c0.shape, F32),
                   jax.ShapeDtypeStruct(n0.shape, F32), jax.ShapeDtypeStruct(m0.shape, F32),
                   *[jax.ShapeDtypeStruct(w.shape, BF16) for w in casts]],
        scratch_shapes=[pltpu.VMEM((R, SUBLANES + T, QKV), F32)],
        compiler_params=_cparams(("arbitrary", "arbitrary")),
        name="mixers",
    )(proj3, proj3, proj3, proj3, proj3, proj3, small3, conv_state, s0, c0, n0, m0, conv_w, par,
      dn_norm_g.reshape(1, DV), ml_norm_g.reshape(1, MVW), *casts)
    return odn, s_new, oml, c_new, n_new, m_new, cast


def _outproj_kernel(x_ref, a_ref, b_ref, wa_ref, wb_ref, o_ref):
    o_ref[...] = (x_ref[...] + jnp.dot(a_ref[...], wa_ref[...], preferred_element_type=F32)
                  + jnp.dot(b_ref[...], wb_ref[...], preferred_element_type=F32))


def _outproj(x, a, b, w_out):
    m, d = x.shape
    ka, kb = a.shape[1], b.shape[1]
    assert ka == kb
    tm = _pick(m, (512, 256, 128, 64, 32, 16, 8))
    return pl.pallas_call(
        _outproj_kernel,
        grid=(m // tm,),
        in_specs=[
            pl.BlockSpec((tm, d), lambda i: (i, 0)),
            pl.BlockSpec((tm, ka), lambda i: (i, 0)),
            pl.BlockSpec((tm, kb), lambda i: (i, 0)),
            pl.BlockSpec((ka, d), lambda i: (0, 0), pipeline_mode=pl.Buffered(1)),
            pl.BlockSpec((kb, d), lambda i: (1, 0), pipeline_mode=pl.Buffered(1)),
        ],
        out_specs=pl.BlockSpec((tm, d), lambda i: (i, 0)),
        out_shape=jax.ShapeDtypeStruct((m, d), F32),
        compiler_params=_cparams(("parallel",)),
        name="outproj",
    )(x, a, b, w_out, w_out)


def _ffn_kernel(x_ref, g_ref, wg_ref, wu_ref, cw_ref, cb_ref, wd_ref, stg_ref, stu_ref, gf_ref,
                y_ref, nsg_ref, nsu_ref, h_ref, act_ref, ext_ref, *carry,
                tm, tf, ff, shift, pad, tps, width, nj, wide_state, final_norm):
    i = pl.program_id(0)
    j = pl.program_id(1)
    hist = (width - 1) * shift

    def activate_tile():
        conv = []
        for idx, (w_ref, st_ref, ns_ref) in enumerate(
                ((wg_ref, stg_ref, nsg_ref), (wu_ref, stu_ref, nsu_ref))):
            col = pl.ds(pl.multiple_of(idx * ff + j * tf, LANES), tf)
            st_col = col if wide_state else slice(None)
            u = jnp.dot(h_ref[...], w_ref[...], preferred_element_type=F32)
            ext_ref[idx, pad:pad + tm, :] = u
            if tps == 1:
                ext_ref[idx, pad - hist:pad, :] = st_ref[0, :, st_col]
            else:
                ext_ref[idx, pad - hist:pad, :] = jnp.where(i % tps == 0, st_ref[0, :, st_col],
                                                            carry[0][j, idx])
            cw = cw_ref[:, col]
            c = u * cw[width - 1:width, :] + cb_ref[:, col]
            for t in range(width - 1):
                off = pad - hist + t * shift
                c = c + ext_ref[idx, off:off + tm, :] * cw[t:t + 1, :]
            new = u[tm - hist:, :]
            ns_ref[0, :, st_col] = new
            if tps != 1:
                carry[0][j, idx] = new
            conv.append(c)
        act_ref[...] = (conv[0] * _sigmoid(conv[0]) * conv[1]).astype(BF16)

    def down_tile():
        return jnp.dot(act_ref[...], wd_ref[...], preferred_element_type=F32)

    @pl.when(j == 0)
    def _():
        h_ref[...] = _rms(x_ref[...], g_ref[...]).astype(BF16)
        y_ref[...] = jnp.zeros_like(y_ref)
        activate_tile()

    @pl.when((j > 0) & (j < nj))
    def _():
        y_ref[...] += down_tile()
        activate_tile()

    @pl.when(j == nj)
    def _():
        xo = x_ref[...] + y_ref[...] + down_tile()
        y_ref[...] = _rms(xo, gf_ref[...]) if final_norm else xo


def _ffn(x, norm_g, w_up, conv_w, conv_b, w_down, state, final_g, *, tm, shift, tps, final_norm):
    m, d = x.shape
    ff = w_down.shape[0]
    width = conv_w.shape[0]
    hist = (width - 1) * shift
    pad = -(-hist // SUBLANES) * SUBLANES
    tf = _pick(ff, (512, 256, 128))
    nj = ff // tf
    n_tiles = m // tm
    wide_state = hist * 2 * ff * 4 <= WIDE_STATE_BYTES
    kern = functools.partial(_ffn_kernel, tm=tm, tf=tf, ff=ff, shift=shift, pad=pad, tps=tps,
                             width=width, nj=nj, wide_state=wide_state, final_norm=final_norm)
    scratch = [pltpu.VMEM((tm, d), BF16), pltpu.VMEM((tm, tf), BF16), pltpu.VMEM((2, pad + tm, tf), F32)]
    cur = lambda j: jnp.minimum(j, nj - 1)
    prv = lambda j: jnp.maximum(j - 1, 0)
    if tps != 1:
        scratch.append(pltpu.VMEM((nj, 2, hist, tf), F32))
    if wide_state:
        st_in = [pl.BlockSpec((1, hist, 2 * ff), lambda i, j: (i // tps, 0, 0))] * 2
        st_out = [pl.BlockSpec((1, hist, 2 * ff), lambda i, j: (i, 0, 0))]
        st_shape = [jax.ShapeDtypeStruct((n_tiles, hist, 2 * ff), F32)]
    else:
        st_in = [pl.BlockSpec((1, hist, tf), lambda i, j: (i // tps, 0, cur(j))),
                 pl.BlockSpec((1, hist, tf), lambda i, j: (i // tps, 0, nj + cur(j)))]
        st_out = [pl.BlockSpec((1, hist, tf), lambda i, j: (i, 0, cur(j)))] * 2
        st_shape = [jax.ShapeDtypeStruct((n_tiles, hist, ff), F32)] * 2
    call = pl.pallas_call(
        kern if not wide_state else _single_history(kern),
        grid=(n_tiles, nj + 1),
        in_specs=[
            pl.BlockSpec((tm, d), lambda i, j: (i, 0), pipeline_mode=pl.Buffered(1)),
            pl.BlockSpec((1, d), lambda i, j: (0, 0)),
            pl.BlockSpec((d, tf), lambda i, j: (0, cur(j))),
            pl.BlockSpec((d, tf), lambda i, j: (0, nj + cur(j))),
            pl.BlockSpec((width, 2 * ff), lambda i, j: (0, 0)),
            pl.BlockSpec((1, 2 * ff), lambda i, j: (0, 0)),
            pl.BlockSpec((tf, d), lambda i, j: (prv(j), 0)),
            *st_in,
            pl.BlockSpec((1, d), lambda i, j: (0, 0)),
        ],
        out_specs=[pl.BlockSpec((tm, d), lambda i, j: (i, 0)), *st_out],
        out_shape=[jax.ShapeDtypeStruct((m, d), F32), *st_shape],
        scratch_shapes=scratch,
        compiler_params=_cparams(("arbitrary", "arbitrary")),
        name="convffn",
    )
    y, *ns = call(x, norm_g.reshape(1, d), w_up, w_up, conv_w, conv_b.reshape(1, 2 * ff), w_down,
                  state, state, final_g.reshape(1, d))
    return y, [a[tps - 1::tps] for a in ns]


def _single_history(kern):
    def wrapped(*refs):
        ins, (y_ref, ns_ref), scratch = refs[:10], refs[10:12], refs[12:]
        return kern(*ins, y_ref, ns_ref, ns_ref, *scratch)
    return wrapped


def _layer(x, st, p, final_g, final_norm, decode):
    conv_buf, s0, c0, n0, m0, ffn_buf = st
    B, L, D = x.shape
    Hd, Hm = s0.shape[1], c0.shape[1]
    dn_qkv = conv_buf.shape[-1]
    dn_vw = Hd * s0.shape[3]
    ff = p["w_down"].shape[0]

    T = next((c for c in (MIX_CHUNK, CHUNK) if L % c == 0), L)
    N = L // T
    R = _pick(B, (8, 4, 2, 1)) if decode else 1
    if R * T > 128:
        R = 1

    x2 = x.reshape(B * L, D)
    big, small = _inproj(x2, p["norm_mix_g"], p["w_big"], p["w_small"])
    big3 = big.reshape(B * N, T, big.shape[1])
    small3 = small.reshape(B * N, T, LANES)

    pending = [k for k in ("w_out", "w_up", "w_down") if p[k].dtype != BF16]
    if not all(_cast_slab_rows(p[k].shape[0], (B // R) * N) for k in pending):
        p = dict(p, **{k: p[k].astype(BF16) for k in pending})
        pending = []
    o_dn, s_new, o_ml, c_new, n_new, m_new, rounded = _mixers(
        big3, small3, conv_buf, s0, c0, n0, m0.reshape(B, 1, Hm), p["dn_conv_w"], p["par"],
        p["dn_norm_g"], p["ml_norm_g"], R=R, T=T, N=N, off_i=2 * Hd, off_f=2 * Hd + Hm,
        casts=tuple(p[k] for k in pending))
    p = dict(p, **dict(zip(pending, rounded)))
    width = p["dn_conv_w"].shape[0]
    big4 = big.reshape(B, L, big.shape[1])
    if L >= width - 1:
        conv_new = big4[:, L - (width - 1):, :dn_qkv]
    else:
        conv_new = jnp.concatenate([conv_buf, big4[:, :, :dn_qkv]], axis=1)[:, L:]

    x1 = _outproj(x2, o_dn, o_ml, p["w_out"])

    fw = p["ffn_conv_w"].shape[0]
    if decode:
        x1t = x1.reshape(B, L, D).transpose(1, 0, 2).reshape(L * B, D)
        state = ffn_buf.transpose(1, 0, 2).reshape(1, (fw - 1) * B, 2 * ff)
        y, parts = _ffn(x1t, p["norm_ffn_g"], p["w_up"], p["ffn_conv_w"], p["ffn_conv_b"],
                        p["w_down"], state, final_g, tm=L * B, shift=B, tps=1, final_norm=final_norm)
        y = y.reshape(L, B, D).transpose(1, 0, 2)
        ffn_new = jnp.concatenate(
            [a.reshape(fw - 1, B, a.shape[-1]).transpose(1, 0, 2) for a in parts], axis=-1)
    else:
        tm = _pick(L, (1024, 512, 256, 128, 64, 32, 16, 8))
        y, parts = _ffn(x1, p["norm_ffn_g"], p["w_up"], p["ffn_conv_w"], p["ffn_conv_b"],
                        p["w_down"], ffn_buf, final_g, tm=tm, shift=1, tps=L // tm,
                        final_norm=final_norm)
        y = y.reshape(B, L, D)
        ffn_new = jnp.concatenate(parts, axis=-1)
    return y, (conv_new, s_new, c_new, n_new, m_new.reshape(B, Hm), ffn_new), p


def _regroup_kernel(w_ref, big_ref, small_ref, *, c0, c1, c2):
    n = w_ref.shape[0]
    big_ref[:c0, :] = w_ref[:c0, :].astype(BF16)
    big_ref[c0:, :] = w_ref[c1:c2, :].astype(BF16)
    narrow = jnp.concatenate(
        [w_ref[c0:c1, :], w_ref[c2:, :], jnp.zeros((LANES - (c1 - c0) - (n - c2), w_ref.shape[1]), F32)],
        axis=0)
    small_ref[...] = narrow.astype(BF16)


def _regroup(w_t, c0, c1, c2):
    n, k = w_t.shape
    tk = _pick(k, (256, 128))
    nb = c0 + c2 - c1
    assert c0 % 16 == 0 and c1 % SUBLANES == 0 and c2 % SUBLANES == 0
    return pl.pallas_call(
        functools.partial(_regroup_kernel, c0=c0, c1=c1, c2=c2),
        grid=(k // tk,),
        in_specs=[pl.BlockSpec((n, tk), lambda i: (0, i))],
        out_specs=[pl.BlockSpec((nb, tk), lambda i: (0, i)), pl.BlockSpec((LANES, tk), lambda i: (0, i))],
        out_shape=[jax.ShapeDtypeStruct((nb, k), BF16), jax.ShapeDtypeStruct((LANES, k), BF16)],
        compiler_params=_cparams(("parallel",)),
        name="regroup",
    )(w_t)


def _prep_params(l, Hd, Hm, dn_qkv, dn_vw, ml_qk, ml_vw, norm_mix_g, w_in, dn_conv_w, dn_A_log,
                 dn_dt_bias, dn_norm_g, ml_i_bias, ml_f_bias, ml_norm_g, w_out, norm_ffn_g, w_up,
                 ffn_conv_w, ffn_conv_b, w_down):
    w = w_in[l]
    c0 = dn_qkv + dn_vw
    c1 = c0 + 2 * Hd
    c2 = c1 + 2 * ml_qk + 2 * ml_vw
    nsmall = 2 * Hd + 2 * Hm
    assert nsmall <= LANES and w.shape[1] == c2 + 2 * Hm
    w_big, w_small = _regroup(w.T, c0, c1, c2)
    z = lambda n: jnp.zeros((n,), F32)
    bias = jnp.concatenate([z(Hd), dn_dt_bias[l], ml_i_bias[l], ml_f_bias[l], z(LANES - nsmall)])
    alog = jnp.concatenate([z(Hd), dn_A_log[l], z(LANES - 2 * Hd)])
    par = jnp.zeros((SUBLANES, LANES), F32).at[0].set(bias).at[1].set(alog)
    return dict(norm_mix_g=norm_mix_g[l], w_big=w_big, w_small=w_small, dn_conv_w=dn_conv_w[l],
                par=par, dn_norm_g=dn_norm_g[l], ml_norm_g=ml_norm_g[l], w_out=w_out[l],
                norm_ffn_g=norm_ffn_g[l], w_up=w_up[l], ffn_conv_w=ffn_conv_w[l],
                ffn_conv_b=ffn_conv_b[l], w_down=w_down[l])


def kernel(x_prompt, x_sample, state_dn_conv, state_dn_S, state_ml_C, state_ml_n, state_ml_m,
           state_ffn_conv, norm_mix_g, w_in, dn_conv_w, dn_A_log, dn_dt_bias, dn_norm_g,
           ml_i_bias, ml_f_bias, ml_norm_g, w_out, norm_ffn_g, w_up, ffn_conv_w, ffn_conv_b,
           w_down, norm_final_g):
    states = (state_dn_conv, state_dn_S, state_ml_C, state_ml_n, state_ml_m, state_ffn_conv)
    depth = w_in.shape[0]
    batch = x_prompt.shape[0]
    Hd, dk, dv = state_dn_S.shape[2:]
    Hm, mdk, mdv = state_ml_C.shape[2:]
    xp, xs = x_prompt, x_sample
    p_new = [[] for _ in states]
    s_new = [[] for _ in states]
    for l in range(depth):
        p = _prep_params(l, Hd, Hm, state_dn_conv.shape[-1], Hd * dv, Hm * mdk, Hm * mdv,
                         norm_mix_g, w_in, dn_conv_w, dn_A_log, dn_dt_bias, dn_norm_g, ml_i_bias,
                         ml_f_bias, ml_norm_g, w_out, norm_ffn_g, w_up, ffn_conv_w, ffn_conv_b,
                         w_down)
        last = l == depth - 1
        st_p = tuple(jnp.zeros((batch,) + s.shape[2:], s.dtype) for s in states)
        st_s = tuple(s[l] for s in states)
        xp, np_st, p = _layer(xp, st_p, p, norm_final_g, last, decode=False)
        xs, ns_st, p = _layer(xs, st_s, p, norm_final_g, last, decode=True)
        for i in range(len(states)):
            p_new[i].append(np_st[i])
            s_new[i].append(ns_st[i])
    outs_p = tuple(jnp.stack(a, axis=0) for a in p_new)
    outs_s = tuple(jnp.stack(a, axis=0) for a in s_new)
    return (xp, xs) + outs_p + outs_s
```

```python
import functools

import jax
import jax.numpy as jnp
from jax import lax
from jax.experimental import pallas as pl
from jax.experimental.pallas import tpu as pltpu

EPS = 1e-6
CHUNK = 64
MIX_CHUNK = 128
F32 = jnp.float32
BF16 = jnp.bfloat16
LANES = 128
SUBLANES = 8
VMEM_LIMIT = 56 * 1024 * 1024
WIDE_STATE_BYTES = 2 * 1024 * 1024


def _cparams(sem):
    return pltpu.CompilerParams(dimension_semantics=sem, vmem_limit_bytes=VMEM_LIMIT)


def _dot(a, b):
    return jnp.dot(a.astype(BF16), b.astype(BF16), preferred_element_type=F32)


def _dot_nt(a, b):
    return lax.dot_general(a.astype(BF16), b.astype(BF16), (((1,), (1,)), ((), ())),
                           preferred_element_type=F32)


def _dot_tn(a, b):
    return lax.dot_general(a.astype(BF16), b.astype(BF16), (((0,), (0,)), ((), ())),
                           preferred_element_type=F32)


def _mask_dot_exact(mask_bf16, x):
    hi = x.astype(BF16)
    r1 = x - hi.astype(F32)
    mid = r1.astype(BF16)
    lo = (r1 - mid.astype(F32)).astype(BF16)
    d = functools.partial(jnp.dot, preferred_element_type=F32)
    return (d(mask_bf16, hi) + d(mask_bf16, mid)) + d(mask_bf16, lo)


def _softplus(x):
    return jnp.maximum(x, 0.0) + jnp.log1p(jnp.exp(-jnp.abs(x)))


def _sigmoid(x):
    return 1.0 / (1.0 + jnp.exp(-x))


def _rms(x, g):
    return x * lax.rsqrt(jnp.mean(x * x, axis=-1, keepdims=True) + EPS) * g


def _log2(n):
    assert n & (n - 1) == 0 and n > 0, n
    return n.bit_length() - 1


def _pick(n, cands):
    for c in cands:
        if n % c == 0:
            return c
    return n


def _inproj_kernel(x_ref, g_ref, wb_ref, ws_ref, ob_ref, os_ref, h_ref):
    nt = functools.partial(lax.dot_general, dimension_numbers=(((1,), (1,)), ((), ())),
                           preferred_element_type=F32)

    @pl.when(pl.program_id(1) == 0)
    def _():
        h = _rms(x_ref[...], g_ref[...]).astype(BF16)
        h_ref[...] = h
        os_ref[...] = nt(h, ws_ref[...])

    ob_ref[...] = nt(h_ref[...], wb_ref[...])


def _inproj(x, g, w_big_t, w_small_t):
    m, d = x.shape
    nb = w_big_t.shape[0]
    tm = _pick(m, (1024, 512, 256, 128, 64, 32, 16, 8))
    tn = _pick(nb, (1792, 1024, 512, 256, 128))
    return pl.pallas_call(
        _inproj_kernel,
        grid=(m // tm, nb // tn),
        in_specs=[
            pl.BlockSpec((tm, d), lambda i, j: (i, 0)),
            pl.BlockSpec((1, d), lambda i, j: (0, 0)),
            pl.BlockSpec((tn, d), lambda i, j: (j, 0)),
            pl.BlockSpec((LANES, d), lambda i, j: (0, 0)),
        ],
        out_specs=[
            pl.BlockSpec((tm, tn), lambda i, j: (i, j)),
            pl.BlockSpec((tm, LANES), lambda i, j: (i, 0)),
        ],
        out_shape=[jax.ShapeDtypeStruct((m, nb), F32), jax.ShapeDtypeStruct((m, LANES), F32)],
        scratch_shapes=[pltpu.VMEM((tm, d), BF16)],
        compiler_params=_cparams(("parallel", "arbitrary")),
        name="inproj",
    )(x, g.reshape(1, d), w_big_t, w_small_t)


def _seq_masks(R, T):
    RT = R * T
    sh = _log2(T)
    ri = lax.broadcasted_iota(jnp.int32, (RT, RT), 0)
    ci = lax.broadcasted_iota(jnp.int32, (RT, RT), 1)
    same = (ri >> sh) == (ci >> sh)
    tril = same & (ci <= ri)
    strict = same & (ci < ri)
    eye = ri == ci
    lastsel = same & ((ci & (T - 1)) == T - 1)
    rowseq = lax.broadcasted_iota(jnp.int32, (RT, 1), 0) >> sh
    return ri, ci, tril, strict, eye, lastsel, rowseq


def _row_form(col, eye):
    return jnp.sum(jnp.where(eye, col, 0.0), axis=0, keepdims=True)


def _at_last(row, lastsel):
    return jnp.sum(jnp.where(lastsel, row, 0.0), axis=1, keepdims=True)


def _unit_lower_inverses(lms, ri, ci, T):
    ds = None
    s = 1
    while s < T:
        sh = _log2(2 * s)
        blk = ((ri >> sh) == (ci >> sh)) & ((ri & (2 * s - 1)) >= s) & ((ci & (2 * s - 1)) < s)
        ms = [jnp.where(blk, lm, 0.0) for lm in lms]
        if ds is None:
            eye = jnp.where(ri == ci, 1.0, 0.0)
            ds = [eye - m for m in ms]
        else:
            dms = [_dot(d, m) for d, m in zip(ds, ms)]
            yield
            ds = [d - _dot(dm, d) for d, dm in zip(ds, dms)]
        yield
        s *= 2
    return ds


def _causal_conv(ext_ref, x_ref, cw_ref, T, width, cols):
    lo = SUBLANES - (width - 1)
    ext_ref[:, SUBLANES:SUBLANES + T, cols] = x_ref[:, :, cols]
    cw = cw_ref[:, cols]
    acc = None
    for j in range(width):
        term = ext_ref[:, lo + j:lo + j + T, cols] * cw[j:j + 1, :]
        acc = term if acc is None else acc + term
    ext_ref[:, lo:SUBLANES, cols] = ext_ref[:, lo + T:SUBLANES + T, cols]
    return acc


def _gdn_steps(qkv_ref, z_ref, sm_ref, cw_ref, par_ref, ng_ref, o_ref, s_ref, ext_ref, masks,
               *, R, T, H, DK, DV, width):
    RT = R * T
    QK = H * DK

    groups = []
    for c0, c1 in ((0, QK), (QK, 2 * QK), (2 * QK, 2 * QK + H * DV)):
        a = _causal_conv(ext_ref, qkv_ref, cw_ref, T, width, slice(c0, c1)).reshape(RT, c1 - c0)
        groups.append(a * _sigmoid(a))
        if R > 1:
            yield
    yq, yk, yv = groups
    if R == 1:
        yield
    z = z_ref[...].reshape(RT, H * DV)

    sm = sm_ref[...].reshape(RT, LANES)
    beta_all = _sigmoid(sm)
    g_all = -jnp.exp(par_ref[1:2, :]) * _softplus(sm + par_ref[0:1, :])

    ri, ci, tril, strict, eye, lastsel, rowseq = masks
    g_cum = _mask_dot_exact(jnp.where(tril, 1.0, 0.0).astype(BF16), g_all)
    yield

    hs = range(H)
    q = [yq[:, h * DK:(h + 1) * DK] for h in hs]
    k = [yk[:, h * DK:(h + 1) * DK] for h in hs]
    v = [yv[:, h * DV:(h + 1) * DV] for h in hs]
    q = [a * lax.rsqrt(jnp.sum(a * a, axis=-1, keepdims=True) + EPS) * (DK ** -0.5) for a in q]
    yield
    k = [a * lax.rsqrt(jnp.sum(a * a, axis=-1, keepdims=True) + EPS) for a in k]
    yield
    beta = [beta_all[:, h:h + 1] for h in hs]
    gc = [g_cum[:, H + h:H + h + 1] for h in hs]
    gr = [_row_form(a, eye) for a in gc]
    if R == 1:
        glast = [a[RT - 1:RT, :] for a in gc]
    else:
        glast = [_at_last(a, lastsel) for a in gr]
    yield
    decay = [jnp.exp(jnp.where(tril, c - r, -jnp.inf)) for c, r in zip(gc, gr)]
    kb = [a * b for a, b in zip(k, beta)]
    yield
    lm = [jnp.where(strict, _dot_nt(a, b) * d, 0.0) for a, b, d in zip(kb, k, decay)]
    yield
    tinv = yield from _unit_lower_inverses(lm, ri, ci, T)
    eg = [jnp.exp(a) for a in gc]
    uw = [_dot(t, jnp.concatenate([a * b, c * e], axis=1))
          for t, a, b, c, e in zip(tinv, v, beta, kb, eg)]
    yield
    u = [a[:, :DV] for a in uw]
    w = [a[:, DV:] for a in uw]
    qk = [_dot_nt(a, b) * d for a, b, d in zip(q, k, decay)]
    yield
    qe = [a * e for a, e in zip(q, eg)]
    kd = [a * jnp.exp(l - c) for a, l, c in zip(k, glast, gc)]
    gl = [jnp.exp(a) for a in glast]
    yield

    if R == 1:
        s_old = [s_ref[0, h] for h in hs]
        ws = [_dot(jnp.concatenate([a, b], axis=0), c) for a, b, c in zip(w, qe, s_old)]
        yield
        v_new = [a - b[:RT] for a, b in zip(u, ws)]
        o = [a[RT:] + _dot(b, c) for a, b, c in zip(ws, qk, v_new)]
        yield
        for h in hs:
            s_ref[0, h] = s_old[h] * gl[h][0:1, :] + _dot_tn(kd[h], v_new[h])
        yield
    else:
        ws = [[_dot(jnp.concatenate([w[h][s * T:(s + 1) * T], qe[h][s * T:(s + 1) * T]], axis=0),
                    s_ref[s, h]) for s in range(R)] for h in hs]
        yield
        v_new = [u[h] - jnp.concatenate([a[:T] for a in ws[h]], axis=0) for h in hs]
        o = [jnp.concatenate([a[T:] for a in ws[h]], axis=0) + _dot(qk[h], v_new[h]) for h in hs]
        yield
        for h in hs:
            for s in range(R):
                kd_s = jnp.where(rowseq == s, kd[h], 0.0)
                s_ref[s, h] = s_ref[s, h] * gl[h][s * T:s * T + 1, :] + _dot_tn(kd_s, v_new[h])
            yield

    for h in hs:
        zh = z[:, h * DV:(h + 1) * DV]
        out = _rms(o[h], ng_ref[...]) * (zh * _sigmoid(zh))
        o_ref[:, h * DV:(h + 1) * DV] = out.astype(o_ref.dtype)
    yield


def _mlstm_steps(q_ref, k_ref, v_ref, og_ref, sm_ref, par_ref, ng_ref, o_ref, c_ref, n_ref, m_ref,
                 masks, *, R, T, H, DK, DV, off_i, off_f):
    RT = R * T
    ri, ci, tril, strict, eye, lastsel, rowseq = masks

    pre = sm_ref[...].reshape(RT, LANES) + par_ref[0:1, :]
    logf_all = -_softplus(-pre)
    b_all = _mask_dot_exact(jnp.where(tril, 1.0, 0.0).astype(BF16), logf_all)
    yield

    qa = q_ref[...].reshape(RT, H * DK)
    ka = k_ref[...].reshape(RT, H * DK)
    va = v_ref[...].reshape(RT, H * DV)
    oa = og_ref[...].reshape(RT, H * DV)

    hs = range(H)
    q = [qa[:, h * DK:(h + 1) * DK] * (DK ** -0.5) for h in hs]
    k = [ka[:, h * DK:(h + 1) * DK] for h in hs]
    v = [va[:, h * DV:(h + 1) * DV] for h in hs]
    bc = [b_all[:, off_f + h:off_f + h + 1] for h in hs]
    ic = [pre[:, off_i + h:off_i + h + 1] for h in hs]
    dm = [jnp.where(tril, b + _row_form(i - b, eye), -jnp.inf) for b, i in zip(bc, ic)]
    yield

    if R == 1:
        m_rows = [m_ref[0, :, h:h + 1] for h in hs]
    else:
        m_rows = []
        for h in hs:
            mr = jnp.zeros((RT, 1), F32)
            for s in range(R):
                mr = jnp.where(rowseq == s, m_ref[s, :, h:h + 1], mr)
            m_rows.append(mr)
    m_new = [jnp.maximum(b + m, jnp.max(d, axis=1, keepdims=True)) for b, m, d in zip(bc, m_rows, dm)]
    inter = [jnp.exp(b + m - mn) for b, m, mn in zip(bc, m_rows, m_new)]
    yield
    smat = [_dot_nt(a, b) * jnp.exp(d - mn) for a, b, d, mn in zip(q, k, dm, m_new)]
    yield

    if R == 1:
        qc = [_dot(q[h], c_ref[0, h]) for h in hs]
        qn = [jnp.sum(q[h] * n_ref[0, h:h + 1, :], axis=1, keepdims=True) for h in hs]
    else:
        qc, qn = [], []
        for h in hs:
            rows = [slice(s * T, (s + 1) * T) for s in range(R)]
            qc.append(jnp.concatenate([_dot(q[h][r], c_ref[s, h]) for s, r in enumerate(rows)], axis=0))
            qn.append(jnp.concatenate(
                [jnp.sum(q[h][r] * n_ref[s, h:h + 1, :], axis=1, keepdims=True)
                 for s, r in enumerate(rows)], axis=0))
            yield
    yield
    num = [i * c + _dot(s, a) for i, c, s, a in zip(inter, qc, smat, v)]
    den = [i * n + jnp.sum(s, axis=1, keepdims=True) for i, n, s in zip(inter, qn, smat)]
    yield
    hh = [a / jnp.maximum(jnp.abs(b), jnp.exp(-mn)) for a, b, mn in zip(num, den, m_new)]

    if R == 1:
        b_last = [b[RT - 1:RT, :] for b in bc]
        m_end = [mn[RT - 1:RT, :] for mn in m_new]
    else:
        b_last = [_at_last(_row_form(b, eye), lastsel) for b in bc]
        m_end = [_at_last(_row_form(mn, eye), lastsel) for mn in m_new]
    yield
    kw = [a * jnp.exp(bl - b + i - me) for a, bl, b, i, me in zip(k, b_last, bc, ic, m_end)]
    cs = [jnp.exp(bl + m - me) for bl, m, me in zip(b_last, m_rows, m_end)]
    yield
    for h in hs:
        for s in range(R):
            kw_s = kw[h] if R == 1 else jnp.where(rowseq == s, kw[h], 0.0)
            cs_s = cs[h][s * T:s * T + 1, :]
            c_ref[s, h] = cs_s * c_ref[s, h] + _dot_tn(kw_s, v[h])
            n_ref[s, h:h + 1, :] = cs_s * n_ref[s, h:h + 1, :] + jnp.sum(kw_s, axis=0, keepdims=True)
            m_ref[s, :, h:h + 1] = m_end[h][s * T:s * T + 1, :]
        yield

    for h in hs:
        og = oa[:, h * DV:(h + 1) * DV]
        out = _rms(hh[h], ng_ref[:, h * DV:(h + 1) * DV]) * _sigmoid(og)
        o_ref[:, h * DV:(h + 1) * DV] = out.astype(o_ref.dtype)
    yield


def _cast_steps(pairs):
    for src_ref, dst_ref in pairs:
        dst_ref[...] = src_ref[...].astype(BF16)
        yield


def _mixers_kernel(*refs, n_cast, R, T, Hd, Hm, DK, DV, MDK, MDV, width, off_i, off_f):
    (qkv_ref, z_ref, mq_ref, mk_ref, mv_ref, mo_ref, sm_ref, cst_ref, s0_ref, c0_ref, n0_ref, m0_ref,
     cw_ref, par_ref, dng_ref, mng_ref) = refs[:16]
    cast_in = refs[16:16 + n_cast]
    odn_ref, s_ref, oml_ref, c_ref, n_ref, m_ref = refs[16 + n_cast:22 + n_cast]
    cast_out = refs[22 + n_cast:22 + 2 * n_cast]
    ext_ref, = refs[22 + 2 * n_cast:]

    @pl.when(pl.program_id(1) == 0)
    def _():
        ext_ref[:, SUBLANES - (width - 1):SUBLANES, :] = cst_ref[...]
        s_ref[...] = s0_ref[...]
        c_ref[...] = c0_ref[...]
        n_ref[...] = n0_ref[...]
        m_ref[...] = m0_ref[...]

    masks = _seq_masks(R, T)
    streams = [
        _gdn_steps(qkv_ref, z_ref, sm_ref, cw_ref, par_ref, dng_ref, odn_ref, s_ref, ext_ref, masks,
                   R=R, T=T, H=Hd, DK=DK, DV=DV, width=width),
        _mlstm_steps(mq_ref, mk_ref, mv_ref, mo_ref, sm_ref, par_ref, mng_ref, oml_ref, c_ref, n_ref,
                     m_ref, masks, R=R, T=T, H=Hm, DK=MDK, DV=MDV, off_i=off_i, off_f=off_f),
        _cast_steps(list(zip(cast_in, cast_out))),
    ]
    turns = [3 if R == 1 else 2, 1, 1]
    while streams:
        for gen, n in list(zip(streams, turns)):
            for _ in range(n):
                if gen in streams and next(gen, "done") == "done":
                    turns.pop(streams.index(gen))
                    streams.remove(gen)


def _cast_slab_rows(k, steps):
    return next((r for r in range(16, k + 1, 16) if k % r == 0 and k // r <= steps), None)


def _mixers(proj3, small3, conv_state, s0, c0, n0, m0, conv_w, par, dn_norm_g, ml_norm_g,
            *, R, T, N, off_i, off_f, casts=()):
    B, Hd, DK, DV = s0.shape
    _, Hm, MDK, MDV = c0.shape
    QKV = conv_state.shape[-1]
    VW, MQK, MVW = Hd * DV, Hm * MDK, Hm * MDV
    width = conv_w.shape[0]
    col0 = QKV + VW
    assert QKV == 2 * Hd * DK + VW and QKV % VW == 0 and VW % LANES == 0
    assert col0 % MQK == 0 and (col0 + 2 * MQK) % MVW == 0 and MQK % LANES == 0
    rows = proj3.shape[0] * T
    qb, vb = col0 // MQK, (col0 + 2 * MQK) // MVW
    kern = functools.partial(_mixers_kernel, n_cast=len(casts), R=R, T=T, Hd=Hd, Hm=Hm, DK=DK, DV=DV,
                             MDK=MDK, MDV=MDV, width=width, off_i=off_i, off_f=off_f)
    tok = lambda w, c: pl.BlockSpec((R, T, w), lambda b, n: (b * N + n, 0, c))
    const = lambda shape: pl.BlockSpec(shape, lambda b, n: (0,) * len(shape))
    per_seq = lambda *dims: pl.BlockSpec((R,) + dims, lambda b, n: (b,) + (0,) * len(dims))
    state_specs = [per_seq(Hd, DK, DV), per_seq(Hm, MDK, MDV), per_seq(Hm, MDK), per_seq(1, Hm)]

    def slab(w):
        r = _cast_slab_rows(w.shape[0], (B // R) * N)
        return pl.BlockSpec((r, w.shape[1]), lambda b, n: (jnp.minimum(b * N + n, w.shape[0] // r - 1), 0))

    cast_specs = [slab(w) for w in casts]
    odn, s_new, oml, c_new, n_new, m_new, *cast = pl.pallas_call(
        kern,
        grid=(B // R, N),
        in_specs=[
            tok(QKV, 0), tok(VW, QKV // VW), tok(MQK, qb), tok(MQK, qb + 1), tok(MVW, vb),
            tok(MVW, vb + 1), tok(LANES, 0),
            per_seq(width - 1, QKV), *state_specs,
            const((width, QKV)), const((SUBLANES, LANES)), const((1, DV)), const((1, MVW)),
            *cast_specs,
        ],
        out_specs=[pl.BlockSpec((R * T, VW), lambda b, n: (b * N + n, 0)), state_specs[0],
                   pl.BlockSpec((R * T, MVW), lambda b, n: (b * N + n, 0)), *state_specs[1:],
                   *cast_specs],
        out_shape=[jax.ShapeDtypeStruct((rows, VW), BF16), jax.ShapeDtypeStruct(s0.shape, F32),
                   jax.ShapeDtypeStruct((rows, MVW), BF16), jax.ShapeDtypeStruct(c0.shape, F32),
                   jax.ShapeDtypeStruct(n0.shape, F32), jax.ShapeDtypeStruct(m0.shape, F32),
                   *[jax.ShapeDtypeStruct(w.shape, BF16) for w in casts]],
        scratch_shapes=[pltpu.VMEM((R, SUBLANES + T, QKV), F32)],
        compiler_params=_cparams(("arbitrary", "arbitrary")),
        name="mixers",
    )(proj3, proj3, proj3, proj3, proj3, proj3, small3, conv_state, s0, c0, n0, m0, conv_w, par,
      dn_norm_g.reshape(1, DV), ml_norm_g.reshape(1, MVW), *casts)
    return odn, s_new, oml, c_new, n_new, m_new, cast


def _outproj_kernel(x_ref, a_ref, b_ref, wa_ref, wb_ref, o_ref):
    o_ref[...] = (x_ref[...] + jnp.dot(a_ref[...], wa_ref[...], preferred_element_type=F32)
                  + jnp.dot(b_ref[...], wb_ref[...], preferred_element_type=F32))


def _outproj(x, a, b, w_out):
    m, d = x.shape
    ka, kb = a.shape[1], b.shape[1]
    assert ka == kb
    tm = _pick(m, (512, 256, 128, 64, 32, 16, 8))
    return pl.pallas_call(
        _outproj_kernel,
        grid=(m // tm,),
        in_specs=[
            pl.BlockSpec((tm, d), lambda i: (i, 0)),
            pl.BlockSpec((tm, ka), lambda i: (i, 0)),
            pl.BlockSpec((tm, kb), lambda i: (i, 0)),
            pl.BlockSpec((ka, d), lambda i: (0, 0), pipeline_mode=pl.Buffered(1)),
            pl.BlockSpec((kb, d), lambda i: (1, 0), pipeline_mode=pl.Buffered(1)),
        ],
        out_specs=pl.BlockSpec((tm, d), lambda i: (i, 0)),
        out_shape=jax.ShapeDtypeStruct((m, d), F32),
        compiler_params=_cparams(("parallel",)),
        name="outproj",
    )(x, a, b, w_out, w_out)


def _ffn_kernel(x_ref, g_ref, wg_ref, wu_ref, cw_ref, cb_ref, wd_ref, stg_ref, stu_ref, gf_ref,
                y_ref, nsg_ref, nsu_ref, h_ref, act_ref, ext_ref, *carry,
                tm, tf, ff, shift, pad, tps, width, nj, wide_state, final_norm):
    i = pl.program_id(0)
    j = pl.program_id(1)
    hist = (width - 1) * shift

    def activate_tile():
        conv = []
        for idx, (w_ref, st_ref, ns_ref) in enumerate(
                ((wg_ref, stg_ref, nsg_ref), (wu_ref, stu_ref, nsu_ref))):
            col = pl.ds(pl.multiple_of(idx * ff + j * tf, LANES), tf)
            st_col = col if wide_state else slice(None)
            u = jnp.dot(h_ref[...], w_ref[...], preferred_element_type=F32)
            ext_ref[idx, pad:pad + tm, :] = u
            if tps == 1:
                ext_ref[idx, pad - hist:pad, :] = st_ref[0, :, st_col]
            else:
                ext_ref[idx, pad - hist:pad, :] = jnp.where(i % tps == 0, st_ref[0, :, st_col],
                                                            carry[0][j, idx])
            cw = cw_ref[:, col]
            c = u * cw[width - 1:width, :] + cb_ref[:, col]
            for t in range(width - 1):
                off = pad - hist + t * shift
                c = c + ext_ref[idx, off:off + tm, :] * cw[t:t + 1, :]
            new = u[tm - hist:, :]
            ns_ref[0, :, st_col] = new
            if tps != 1:
                carry[0][j, idx] = new
            conv.append(c)
        act_ref[...] = (conv[0] * _sigmoid(conv[0]) * conv[1]).astype(BF16)

    def down_tile():
        return jnp.dot(act_ref[...], wd_ref[...], preferred_element_type=F32)

    @pl.when(j == 0)
    def _():
        h_ref[...] = _rms(x_ref[...], g_ref[...]).astype(BF16)
        y_ref[...] = jnp.zeros_like(y_ref)
        activate_tile()

    @pl.when((j > 0) & (j < nj))
    def _():
        y_ref[...] += down_tile()
        activate_tile()

    @pl.when(j == nj)
    def _():
        xo = x_ref[...] + y_ref[...] + down_tile()
        y_ref[...] = _rms(xo, gf_ref[...]) if final_norm else xo


def _ffn(x, norm_g, w_up, conv_w, conv_b, w_down, state, final_g, *, tm, shift, tps, final_norm):
    m, d = x.shape
    ff = w_down.shape[0]
    width = conv_w.shape[0]
    hist = (width - 1) * shift
    pad = -(-hist // SUBLANES) * SUBLANES
    tf = _pick(ff, (512, 256, 128))
    nj = ff // tf
    n_tiles = m // tm
    wide_state = hist * 2 * ff * 4 <= WIDE_STATE_BYTES
    kern = functools.partial(_ffn_kernel, tm=tm, tf=tf, ff=ff, shift=shift, pad=pad, tps=tps,
                             width=width, nj=nj, wide_state=wide_state, final_norm=final_norm)
    scratch = [pltpu.VMEM((tm, d), BF16), pltpu.VMEM((tm, tf), BF16), pltpu.VMEM((2, pad + tm, tf), F32)]
    cur = lambda j: jnp.minimum(j, nj - 1)
    prv = lambda j: jnp.maximum(j - 1, 0)
    if tps != 1:
        scratch.append(pltpu.VMEM((nj, 2, hist, tf), F32))
    if wide_state:
        st_in = [pl.BlockSpec((1, hist, 2 * ff), lambda i, j: (i // tps, 0, 0))] * 2
        st_out = [pl.BlockSpec((1, hist, 2 * ff), lambda i, j: (i, 0, 0))]
        st_shape = [jax.ShapeDtypeStruct((n_tiles, hist, 2 * ff), F32)]
    else:
        st_in = [pl.BlockSpec((1, hist, tf), lambda i, j: (i // tps, 0, cur(j))),
                 pl.BlockSpec((1, hist, tf), lambda i, j: (i // tps, 0, nj + cur(j)))]
        st_out = [pl.BlockSpec((1, hist, tf), lambda i, j: (i, 0, cur(j)))] * 2
        st_shape = [jax.ShapeDtypeStruct((n_tiles, hist, ff), F32)] * 2
    call = pl.pallas_call(
        kern if not wide_state else _single_history(kern),
        grid=(n_tiles, nj + 1),
        in_specs=[
            pl.BlockSpec((tm, d), lambda i, j: (i, 0), pipeline_mode=pl.Buffered(1)),
            pl.BlockSpec((1, d), lambda i, j: (0, 0)),
            pl.BlockSpec((d, tf), lambda i, j: (0, cur(j))),
            pl.BlockSpec((d, tf), lambda i, j: (0, nj + cur(j))),
            pl.BlockSpec((width, 2 * ff), lambda i, j: (0, 0)),
            pl.BlockSpec((1, 2 * ff), lambda i, j: (0, 0)),
            pl.BlockSpec((tf, d), lambda i, j: (prv(j), 0)),
            *st_in,
            pl.BlockSpec((1, d), lambda i, j: (0, 0)),
        ],
        out_specs=[pl.BlockSpec((tm, d), lambda i, j: (i, 0)), *st_out],
        out_shape=[jax.ShapeDtypeStruct((m, d), F32), *st_shape],
        scratch_shapes=scratch,
        compiler_params=_cparams(("arbitrary", "arbitrary")),
        name="convffn",
    )
    y, *ns = call(x, norm_g.reshape(1, d), w_up, w_up, conv_w, conv_b.reshape(1, 2 * ff), w_down,
                  state, state, final_g.reshape(1, d))
    return y, [a[tps - 1::tps] for a in ns]


def _single_history(kern):
    def wrapped(*refs):
        ins, (y_ref, ns_ref), scratch = refs[:10], refs[10:12], refs[12:]
        return kern(*ins, y_ref, ns_ref, ns_ref, *scratch)
    return wrapped


def _layer(x, st, p, final_g, final_norm, decode):
    conv_buf, s0, c0, n0, m0, ffn_buf = st
    B, L, D = x.shape
    Hd, Hm = s0.shape[1], c0.shape[1]
    dn_qkv = conv_buf.shape[-1]
    dn_vw = Hd * s0.shape[3]
    ff = p["w_down"].shape[0]

    T = next((c for c in (MIX_CHUNK, CHUNK) if L % c == 0), L)
    N = L // T
    R = _pick(B, (8, 4, 2, 1)) if decode else 1
    if R * T > 128:
        R = 1

    x2 = x.reshape(B * L, D)
    big, small = _inproj(x2, p["norm_mix_g"], p["w_big"], p["w_small"])
    big3 = big.reshape(B * N, T, big.shape[1])
    small3 = small.reshape(B * N, T, LANES)

    pending = [k for k in ("w_out", "w_up", "w_down") if p[k].dtype != BF16]
    if not all(_cast_slab_rows(p[k].shape[0], (B // R) * N) for k in pending):
        p = dict(p, **{k: p[k].astype(BF16) for k in pending})
        pending = []
    o_dn, s_new, o_ml, c_new, n_new, m_new, rounded = _mixers(
        big3, small3, conv_buf, s0, c0, n0, m0.reshape(B, 1, Hm), p["dn_conv_w"], p["par"],
        p["dn_norm_g"], p["ml_norm_g"], R=R, T=T, N=N, off_i=2 * Hd, off_f=2 * Hd + Hm,
        casts=tuple(p[k] for k in pending))
    p = dict(p, **dict(zip(pending, rounded)))
    width = p["dn_conv_w"].shape[0]
    big4 = big.reshape(B, L, big.shape[1])
    if L >= width - 1:
        conv_new = big4[:, L - (width - 1):, :dn_qkv]
    else:
        conv_new = jnp.concatenate([conv_buf, big4[:, :, :dn_qkv]], axis=1)[:, L:]

    x1 = _outproj(x2, o_dn, o_ml, p["w_out"])

    fw = p["ffn_conv_w"].shape[0]
    if decode:
        x1t = x1.reshape(B, L, D).transpose(1, 0, 2).reshape(L * B, D)
        state = ffn_buf.transpose(1, 0, 2).reshape(1, (fw - 1) * B, 2 * ff)
        y, parts = _ffn(x1t, p["norm_ffn_g"], p["w_up"], p["ffn_conv_w"], p["ffn_conv_b"],
                        p["w_down"], state, final_g, tm=L * B, shift=B, tps=1, final_norm=final_norm)
        y = y.reshape(L, B, D).transpose(1, 0, 2)
        ffn_new = jnp.concatenate(
            [a.reshape(fw - 1, B, a.shape[-1]).transpose(1, 0, 2) for a in parts], axis=-1)
    else:
        tm = _pick(L, (1024, 512, 256, 128, 64, 32, 16, 8))
        y, parts = _ffn(x1, p["norm_ffn_g"], p["w_up"], p["ffn_conv_w"], p["ffn_conv_b"],
                        p["w_down"], ffn_buf, final_g, tm=tm, shift=1, tps=L // tm,
                        final_norm=final_norm)
        y = y.reshape(B, L, D)
        ffn_new = jnp.concatenate(parts, axis=-1)
    return y, (conv_new, s_new, c_new, n_new, m_new.reshape(B, Hm), ffn_new), p


def _regroup_kernel(w_ref, big_ref, small_ref, *, c0, c1, c2):
    n = w_ref.shape[0]
    big_ref[:c0, :] = w_ref[:c0, :].astype(BF16)
    big_ref[c0:, :] = w_ref[c1:c2, :].astype(BF16)
    narrow = jnp.concatenate(
        [w_ref[c0:c1, :], w_ref[c2:, :], jnp.zeros((LANES - (c1 - c0) - (n - c2), w_ref.shape[1]), F32)],
        axis=0)
    small_ref[...] = narrow.astype(BF16)


def _regroup(w_t, c0, c1, c2):
    n, k = w_t.shape
    tk = _pick(k, (256, 128))
    nb = c0 + c2 - c1
    assert c0 % 16 == 0 and c1 % SUBLANES == 0 and c2 % SUBLANES == 0
    return pl.pallas_call(
        functools.partial(_regroup_kernel, c0=c0, c1=c1, c2=c2),
        grid=(k // tk,),
        in_specs=[pl.BlockSpec((n, tk), lambda i: (0, i))],
        out_specs=[pl.BlockSpec((nb, tk), lambda i: (0, i)), pl.BlockSpec((LANES, tk), lambda i: (0, i))],
        out_shape=[jax.ShapeDtypeStruct((nb, k), BF16), jax.ShapeDtypeStruct((LANES, k), BF16)],
        compiler_params=_cparams(("parallel",)),
        name="regroup",
    )(w_t)


def _prep_params(l, Hd, Hm, dn_qkv, dn_vw, ml_qk, ml_vw, norm_mix_g, w_in, dn_conv_w, dn_A_log,
                 dn_dt_bias, dn_norm_g, ml_i_bias, ml_f_bias, ml_norm_g, w_out, norm_ffn_g, w_up,
                 ffn_conv_w, ffn_conv_b, w_down):
    w = w_in[l]
    c0 = dn_qkv + dn_vw
    c1 = c0 + 2 * Hd
    c2 = c1 + 2 * ml_qk + 2 * ml_vw
    nsmall = 2 * Hd + 2 * Hm
    assert nsmall <= LANES and w.shape[1] == c2 + 2 * Hm
    w_big, w_small = _regroup(w.T, c0, c1, c2)
    z = lambda n: jnp.zeros((n,), F32)
    bias = jnp.concatenate([z(Hd), dn_dt_bias[l], ml_i_bias[l], ml_f_bias[l], z(LANES - nsmall)])
    alog = jnp.concatenate([z(Hd), dn_A_log[l], z(LANES - 2 * Hd)])
    par = jnp.zeros((SUBLANES, LANES), F32).at[0].set(bias).at[1].set(alog)
    return dict(norm_mix_g=norm_mix_g[l], w_big=w_big, w_small=w_small, dn_conv_w=dn_conv_w[l],
                par=par, dn_norm_g=dn_norm_g[l], ml_norm_g=ml_norm_g[l], w_out=w_out[l],
                norm_ffn_g=norm_ffn_g[l], w_up=w_up[l], ffn_conv_w=ffn_conv_w[l],
                ffn_conv_b=ffn_conv_b[l], w_down=w_down[l])


def kernel(x_prompt, x_sample, state_dn_conv, state_dn_S, state_ml_C, state_ml_n, state_ml_m,
           state_ffn_conv, norm_mix_g, w_in, dn_conv_w, dn_A_log, dn_dt_bias, dn_norm_g,
           ml_i_bias, ml_f_bias, ml_norm_g, w_out, norm_ffn_g, w_up, ffn_conv_w, ffn_conv_b,
           w_down, norm_final_g):
    states = (state_dn_conv, state_dn_S, state_ml_C, state_ml_n, state_ml_m, state_ffn_conv)
    depth = w_in.shape[0]
    batch = x_prompt.shape[0]
    Hd, dk, dv = state_dn_S.shape[2:]
    Hm, mdk, mdv = state_ml_C.shape[2:]
    xp, xs = x_prompt, x_sample
    p_new = [[] for _ in states]
    s_new = [[] for _ in states]
    for l in range(depth):
        p = _prep_params(l, Hd, Hm, state_dn_conv.shape[-1], Hd * dv, Hm * mdk, Hm * mdv,
                         norm_mix_g, w_in, dn_conv_w, dn_A_log, dn_dt_bias, dn_norm_g, ml_i_bias,
                         ml_f_bias, ml_norm_g, w_out, norm_ffn_g, w_up, ffn_conv_w, ffn_conv_b,
                         w_down)
        last = l == depth - 1
        st_p = tuple(jnp.zeros((batch,) + s.shape[2:], s.dtype) for s in states)
        st_s = tuple(s[l] for s in states)
        xp, np_st, p = _layer(xp, st_p, p, norm_final_g, last, decode=False)
        xs, ns_st, p = _layer(xs, st_s, p, norm_final_g, last, decode=True)
        for i in range(len(states)):
            p_new[i].append(np_st[i])
            s_new[i].append(ns_st[i])
    outs_p = tuple(jnp.stack(a, axis=0) for a in p_new)
    outs_s = tuple(jnp.stack(a, axis=0) for a in s_new)
    return (xp, xs) + outs_p + outs_s
```

```python
import functools

import jax
import jax.numpy as jnp
from jax import lax
from jax.experimental import pallas as pl
from jax.experimental.pallas import tpu as pltpu

EPS = 1e-6
CHUNK = 64
MIX_CHUNK = 128
F32 = jnp.float32
BF16 = jnp.bfloat16
LANES = 128
SUBLANES = 8
VMEM_LIMIT = 56 * 1024 * 1024
WIDE_STATE_BYTES = 2 * 1024 * 1024


def _cparams(sem):
    return pltpu.CompilerParams(dimension_semantics=sem, vmem_limit_bytes=VMEM_LIMIT)


def _dot(a, b):
    return jnp.dot(a.astype(BF16), b.astype(BF16), preferred_element_type=F32)


def _dot_nt(a, b):
    return lax.dot_general(a.astype(BF16), b.astype(BF16), (((1,), (1,)), ((), ())),
                           preferred_element_type=F32)


def _dot_tn(a, b):
    return lax.dot_general(a.astype(BF16), b.astype(BF16), (((0,), (0,)), ((), ())),
                           preferred_element_type=F32)


def _mask_dot_exact(mask_bf16, x):
    hi = x.astype(BF16)
    r1 = x - hi.astype(F32)
    mid = r1.astype(BF16)
    lo = (r1 - mid.astype(F32)).astype(BF16)
    d = functools.partial(jnp.dot, preferred_element_type=F32)
    return (d(mask_bf16, hi) + d(mask_bf16, mid)) + d(mask_bf16, lo)


def _softplus(x):
    return jnp.maximum(x, 0.0) + jnp.log1p(jnp.exp(-jnp.abs(x)))


def _sigmoid(x):
    return 1.0 / (1.0 + jnp.exp(-x))


def _rms(x, g):
    return x * lax.rsqrt(jnp.mean(x * x, axis=-1, keepdims=True) + EPS) * g


def _log2(n):
    assert n & (n - 1) == 0 and n > 0, n
    return n.bit_length() - 1


def _pick(n, cands):
    for c in cands:
        if n % c == 0:
            return c
    return n


def _inproj_kernel(x_ref, g_ref, wb_ref, ws_ref, ob_ref, os_ref, h_ref):
    nt = functools.partial(lax.dot_general, dimension_numbers=(((1,), (1,)), ((), ())),
                           preferred_element_type=F32)

    @pl.when(pl.program_id(1) == 0)
    def _():
        h = _rms(x_ref[...], g_ref[...]).astype(BF16)
        h_ref[...] = h
        os_ref[...] = nt(h, ws_ref[...])

    ob_ref[...] = nt(h_ref[...], wb_ref[...])


def _inproj(x, g, w_big_t, w_small_t):
    m, d = x.shape
    nb = w_big_t.shape[0]
    tm = _pick(m, (1024, 512, 256, 128, 64, 32, 16, 8))
    tn = _pick(nb, (1792, 1024, 512, 256, 128))
    return pl.pallas_call(
        _inproj_kernel,
        grid=(m // tm, nb // tn),
        in_specs=[
            pl.BlockSpec((tm, d), lambda i, j: (i, 0)),
            pl.BlockSpec((1, d), lambda i, j: (0, 0)),
            pl.BlockSpec((tn, d), lambda i, j: (j, 0)),
            pl.BlockSpec((LANES, d), lambda i, j: (0, 0)),
        ],
        out_specs=[
            pl.BlockSpec((tm, tn), lambda i, j: (i, j)),
            pl.BlockSpec((tm, LANES), lambda i, j: (i, 0)),
        ],
        out_shape=[jax.ShapeDtypeStruct((m, nb), F32), jax.ShapeDtypeStruct((m, LANES), F32)],
        scratch_shapes=[pltpu.VMEM((tm, d), BF16)],
        compiler_params=_cparams(("parallel", "arbitrary")),
        name="inproj",
    )(x, g.reshape(1, d), w_big_t, w_small_t)


def _seq_masks(R, T):
    RT = R * T
    sh = _log2(T)
    ri = lax.broadcasted_iota(jnp.int32, (RT, RT), 0)
    ci = lax.broadcasted_iota(jnp.int32, (RT, RT), 1)
    same = (ri >> sh) == (ci >> sh)
    tril = same & (ci <= ri)
    strict = same & (ci < ri)
    eye = ri == ci
    lastsel = same & ((ci & (T - 1)) == T - 1)
    rowseq = lax.broadcasted_iota(jnp.int32, (RT, 1), 0) >> sh
    return ri, ci, tril, strict, eye, lastsel, rowseq


def _row_form(col, eye):
    return jnp.sum(jnp.where(eye, col, 0.0), axis=0, keepdims=True)


def _at_last(row, lastsel):
    return jnp.sum(jnp.where(lastsel, row, 0.0), axis=1, keepdims=True)


def _unit_lower_inverses(lms, ri, ci, T):
    ds = None
    s = 1
    while s < T:
        sh = _log2(2 * s)
        blk = ((ri >> sh) == (ci >> sh)) & ((ri & (2 * s - 1)) >= s) & ((ci & (2 * s - 1)) < s)
        ms = [jnp.where(blk, lm, 0.0) for lm in lms]
        if ds is None:
            eye = jnp.where(ri == ci, 1.0, 0.0)
            ds = [eye - m for m in ms]
        else:
            dms = [_dot(d, m) for d, m in zip(ds, ms)]
            yield
            ds = [d - _dot(dm, d) for d, dm in zip(ds, dms)]
        yield
        s *= 2
    return ds


def _causal_conv(ext_ref, x_ref, cw_ref, T, width, cols):
    lo = SUBLANES - (width - 1)
    ext_ref[:, SUBLANES:SUBLANES + T, cols] = x_ref[:, :, cols]
    cw = cw_ref[:, cols]
    acc = None
    for j in range(width):
        term = ext_ref[:, lo + j:lo + j + T, cols] * cw[j:j + 1, :]
        acc = term if acc is None else acc + term
    ext_ref[:, lo:SUBLANES, cols] = ext_ref[:, lo + T:SUBLANES + T, cols]
    return acc


def _gdn_steps(qkv_ref, z_ref, sm_ref, cw_ref, par_ref, ng_ref, o_ref, s_ref, cs_ref, ext_ref, masks,
               *, R, T, H, DK, DV, width):
    RT = R * T
    QK = H * DK

    groups = []
    for c0, c1 in ((0, QK), (QK, 2 * QK), (2 * QK, 2 * QK + H * DV)):
        a = _causal_conv(ext_ref, qkv_ref, cw_ref, T, width, slice(c0, c1)).reshape(RT, c1 - c0)
        groups.append(a * _sigmoid(a))
        if R > 1:
            yield
    yq, yk, yv = groups
    cs_ref[...] = ext_ref[:, SUBLANES - (width - 1):SUBLANES, :]
    if R == 1:
        yield
    z = z_ref[...].reshape(RT, H * DV)

    sm = sm_ref[...].reshape(RT, LANES)
    beta_all = _sigmoid(sm)
    g_all = -jnp.exp(par_ref[1:2, :]) * _softplus(sm + par_ref[0:1, :])

    ri, ci, tril, strict, eye, lastsel, rowseq = masks
    g_cum = _mask_dot_exact(jnp.where(tril, 1.0, 0.0).astype(BF16), g_all)
    yield

    hs = range(H)
    q = [yq[:, h * DK:(h + 1) * DK] for h in hs]
    k = [yk[:, h * DK:(h + 1) * DK] for h in hs]
    v = [yv[:, h * DV:(h + 1) * DV] for h in hs]
    q = [a * lax.rsqrt(jnp.sum(a * a, axis=-1, keepdims=True) + EPS) * (DK ** -0.5) for a in q]
    yield
    k = [a * lax.rsqrt(jnp.sum(a * a, axis=-1, keepdims=True) + EPS) for a in k]
    yield
    beta = [beta_all[:, h:h + 1] for h in hs]
    gc = [g_cum[:, H + h:H + h + 1] for h in hs]
    gr = [_row_form(a, eye) for a in gc]
    if R == 1:
        glast = [a[RT - 1:RT, :] for a in gc]
    else:
        glast = [_at_last(a, lastsel) for a in gr]
    yield
    decay = [jnp.exp(jnp.where(tril, c - r, -jnp.inf)) for c, r in zip(gc, gr)]
    kb = [a * b for a, b in zip(k, beta)]
    yield
    lm = [jnp.where(strict, _dot_nt(a, b) * d, 0.0) for a, b, d in zip(kb, k, decay)]
    yield
    tinv = yield from _unit_lower_inverses(lm, ri, ci, T)
    eg = [jnp.exp(a) for a in gc]
    uw = [_dot(t, jnp.concatenate([a * b, c * e], axis=1))
          for t, a, b, c, e in zip(tinv, v, beta, kb, eg)]
    yield
    u = [a[:, :DV] for a in uw]
    w = [a[:, DV:] for a in uw]
    qk = [_dot_nt(a, b) * d for a, b, d in zip(q, k, decay)]
    yield
    qe = [a * e for a, e in zip(q, eg)]
    kd = [a * jnp.exp(l - c) for a, l, c in zip(k, glast, gc)]
    gl = [jnp.exp(a) for a in glast]
    yield

    if R == 1:
        s_old = [s_ref[0, h] for h in hs]
        ws = [_dot(jnp.concatenate([a, b], axis=0), c) for a, b, c in zip(w, qe, s_old)]
        yield
        v_new = [a - b[:RT] for a, b in zip(u, ws)]
        o = [a[RT:] + _dot(b, c) for a, b, c in zip(ws, qk, v_new)]
        yield
        for h in hs:
            s_ref[0, h] = s_old[h] * gl[h][0:1, :] + _dot_tn(kd[h], v_new[h])
        yield
    else:
        ws = [[_dot(jnp.concatenate([w[h][s * T:(s + 1) * T], qe[h][s * T:(s + 1) * T]], axis=0),
                    s_ref[s, h]) for s in range(R)] for h in hs]
        yield
        v_new = [u[h] - jnp.concatenate([a[:T] for a in ws[h]], axis=0) for h in hs]
        o = [jnp.concatenate([a[T:] for a in ws[h]], axis=0) + _dot(qk[h], v_new[h]) for h in hs]
        yield
        for h in hs:
            for s in range(R):
                kd_s = jnp.where(rowseq == s, kd[h], 0.0)
                s_ref[s, h] = s_ref[s, h] * gl[h][s * T:s * T + 1, :] + _dot_tn(kd_s, v_new[h])
            yield

    for h in hs:
        zh = z[:, h * DV:(h + 1) * DV]
        out = _rms(o[h], ng_ref[...]) * (zh * _sigmoid(zh))
        o_ref[:, h * DV:(h + 1) * DV] = out.astype(o_ref.dtype)
    yield


def _mlstm_steps(q_ref, k_ref, v_ref, og_ref, sm_ref, par_ref, ng_ref, o_ref, c_ref, n_ref, m_ref,
                 masks, *, R, T, H, DK, DV, off_i, off_f):
    RT = R * T
    ri, ci, tril, strict, eye, lastsel, rowseq = masks

    pre = sm_ref[...].reshape(RT, LANES) + par_ref[0:1, :]
    logf_all = -_softplus(-pre)
    b_all = _mask_dot_exact(jnp.where(tril, 1.0, 0.0).astype(BF16), logf_all)
    yield

    qa = q_ref[...].reshape(RT, H * DK)
    ka = k_ref[...].reshape(RT, H * DK)
    va = v_ref[...].reshape(RT, H * DV)
    oa = og_ref[...].reshape(RT, H * DV)

    hs = range(H)
    q = [qa[:, h * DK:(h + 1) * DK] * (DK ** -0.5) for h in hs]
    k = [ka[:, h * DK:(h + 1) * DK] for h in hs]
    v = [va[:, h * DV:(h + 1) * DV] for h in hs]
    bc = [b_all[:, off_f + h:off_f + h + 1] for h in hs]
    ic = [pre[:, off_i + h:off_i + h + 1] for h in hs]
    dm = [jnp.where(tril, b + _row_form(i - b, eye), -jnp.inf) for b, i in zip(bc, ic)]
    yield

    if R == 1:
        m_rows = [m_ref[0, :, h:h + 1] for h in hs]
    else:
        m_rows = []
        for h in hs:
            mr = jnp.zeros((RT, 1), F32)
            for s in range(R):
                mr = jnp.where(rowseq == s, m_ref[s, :, h:h + 1], mr)
            m_rows.append(mr)
    m_new = [jnp.maximum(b + m, jnp.max(d, axis=1, keepdims=True)) for b, m, d in zip(bc, m_rows, dm)]
    inter = [jnp.exp(b + m - mn) for b, m, mn in zip(bc, m_rows, m_new)]
    yield
    smat = [_dot_nt(a, b) * jnp.exp(d - mn) for a, b, d, mn in zip(q, k, dm, m_new)]
    yield

    if R == 1:
        qc = [_dot(q[h], c_ref[0, h]) for h in hs]
        qn = [jnp.sum(q[h] * n_ref[0, h:h + 1, :], axis=1, keepdims=True) for h in hs]
    else:
        qc, qn = [], []
        for h in hs:
            rows = [slice(s * T, (s + 1) * T) for s in range(R)]
            qc.append(jnp.concatenate([_dot(q[h][r], c_ref[s, h]) for s, r in enumerate(rows)], axis=0))
            qn.append(jnp.concatenate(
                [jnp.sum(q[h][r] * n_ref[s, h:h + 1, :], axis=1, keepdims=True)
                 for s, r in enumerate(rows)], axis=0))
            yield
    yield
    num = [i * c + _dot(s, a) for i, c, s, a in zip(inter, qc, smat, v)]
    den = [i * n + jnp.sum(s, axis=1, keepdims=True) for i, n, s in zip(inter, qn, smat)]
    yield
    hh = [a / jnp.maximum(jnp.abs(b), jnp.exp(-mn)) for a, b, mn in zip(num, den, m_new)]

    if R == 1:
        b_last = [b[RT - 1:RT, :] for b in bc]
        m_end = [mn[RT - 1:RT, :] for mn in m_new]
    else:
        b_last = [_at_last(_row_form(b, eye), lastsel) for b in bc]
        m_end = [_at_last(_row_form(mn, eye), lastsel) for mn in m_new]
    yield
    kw = [a * jnp.exp(bl - b + i - me) for a, bl, b, i, me in zip(k, b_last, bc, ic, m_end)]
    cs = [jnp.exp(bl + m - me) for bl, m, me in zip(b_last, m_rows, m_end)]
    yield
    for h in hs:
        for s in range(R):
            kw_s = kw[h] if R == 1 else jnp.where(rowseq == s, kw[h], 0.0)
            cs_s = cs[h][s * T:s * T + 1, :]
            c_ref[s, h] = cs_s * c_ref[s, h] + _dot_tn(kw_s, v[h])
            n_ref[s, h:h + 1, :] = cs_s * n_ref[s, h:h + 1, :] + jnp.sum(kw_s, axis=0, keepdims=True)
            m_ref[s, :, h:h + 1] = m_end[h][s * T:s * T + 1, :]
        yield

    for h in hs:
        og = oa[:, h * DV:(h + 1) * DV]
        out = _rms(hh[h], ng_ref[:, h * DV:(h + 1) * DV]) * _sigmoid(og)
        o_ref[:, h * DV:(h + 1) * DV] = out.astype(o_ref.dtype)
    yield


def _cast_steps(pairs):
    for src_ref, dst_ref in pairs:
        dst_ref[...] = src_ref[...].astype(BF16)
        yield


def _mixers_kernel(*refs, n_cast, R, T, Hd, Hm, DK, DV, MDK, MDV, width, off_i, off_f):
    (qkv_ref, z_ref, mq_ref, mk_ref, mv_ref, mo_ref, sm_ref, cst_ref, s0_ref, c0_ref, n0_ref, m0_ref,
     cw_ref, par_ref, dng_ref, mng_ref) = refs[:16]
    cast_in = refs[16:16 + n_cast]
    odn_ref, s_ref, oml_ref, c_ref, n_ref, m_ref, cs_ref = refs[16 + n_cast:23 + n_cast]
    cast_out = refs[23 + n_cast:23 + 2 * n_cast]
    ext_ref, = refs[23 + 2 * n_cast:]

    @pl.when(pl.program_id(1) == 0)
    def _():
        ext_ref[:, SUBLANES - (width - 1):SUBLANES, :] = cst_ref[...]
        s_ref[...] = s0_ref[...]
        c_ref[...] = c0_ref[...]
        n_ref[...] = n0_ref[...]
        m_ref[...] = m0_ref[...]

    masks = _seq_masks(R, T)
    streams = [
        _gdn_steps(qkv_ref, z_ref, sm_ref, cw_ref, par_ref, dng_ref, odn_ref, s_ref, cs_ref, ext_ref,
                   masks, R=R, T=T, H=Hd, DK=DK, DV=DV, width=width),
        _mlstm_steps(mq_ref, mk_ref, mv_ref, mo_ref, sm_ref, par_ref, mng_ref, oml_ref, c_ref, n_ref,
                     m_ref, masks, R=R, T=T, H=Hm, DK=MDK, DV=MDV, off_i=off_i, off_f=off_f),
        _cast_steps(list(zip(cast_in, cast_out))),
    ]
    turns = [2, 1, 1]
    while streams:
        for gen, n in list(zip(streams, turns)):
            for _ in range(n):
                if gen in streams and next(gen, "done") == "done":
                    turns.pop(streams.index(gen))
                    streams.remove(gen)


def _cast_slab_rows(k, steps):
    return next((r for r in range(16, k + 1, 16) if k % r == 0 and k // r <= steps), None)


def _mixers(proj3, small3, conv_state, s0, c0, n0, m0, conv_w, par, dn_norm_g, ml_norm_g,
            *, R, T, N, off_i, off_f, casts=()):
    B, Hd, DK, DV = s0.shape
    _, Hm, MDK, MDV = c0.shape
    QKV = conv_state.shape[-1]
    VW, MQK, MVW = Hd * DV, Hm * MDK, Hm * MDV
    width = conv_w.shape[0]
    col0 = QKV + VW
    assert QKV == 2 * Hd * DK + VW and QKV % VW == 0 and VW % LANES == 0
    assert col0 % MQK == 0 and (col0 + 2 * MQK) % MVW == 0 and MQK % LANES == 0
    rows = proj3.shape[0] * T
    qb, vb = col0 // MQK, (col0 + 2 * MQK) // MVW
    kern = functools.partial(_mixers_kernel, n_cast=len(casts), R=R, T=T, Hd=Hd, Hm=Hm, DK=DK, DV=DV,
                             MDK=MDK, MDV=MDV, width=width, off_i=off_i, off_f=off_f)
    tok = lambda w, c: pl.BlockSpec((R, T, w), lambda b, n: (b * N + n, 0, c))
    const = lambda shape: pl.BlockSpec(shape, lambda b, n: (0,) * len(shape))
    per_seq = lambda *dims: pl.BlockSpec((R,) + dims, lambda b, n: (b,) + (0,) * len(dims))
    state_specs = [per_seq(Hd, DK, DV), per_seq(Hm, MDK, MDV), per_seq(Hm, MDK), per_seq(1, Hm)]

    def slab(w):
        r = _cast_slab_rows(w.shape[0], (B // R) * N)
        return pl.BlockSpec((r, w.shape[1]), lambda b, n: (jnp.minimum(b * N + n, w.shape[0] // r - 1), 0))

    cast_specs = [slab(w) for w in casts]
    odn, s_new, oml, c_new, n_new, m_new, conv_new, *cast = pl.pallas_call(
        kern,
        grid=(B // R, N),
        in_specs=[
            tok(QKV, 0), tok(VW, QKV // VW), tok(MQK, qb), tok(MQK, qb + 1), tok(MVW, vb),
            tok(MVW, vb + 1), tok(LANES, 0),
            per_seq(width - 1, QKV), *state_specs,
            const((width, QKV)), const((SUBLANES, LANES)), const((1, DV)), const((1, MVW)),
            *cast_specs,
        ],
        out_specs=[pl.BlockSpec((R * T, VW), lambda b, n: (b * N + n, 0)), state_specs[0],
                   pl.BlockSpec((R * T, MVW), lambda b, n: (b * N + n, 0)), *state_specs[1:],
                   per_seq(width - 1, QKV), *cast_specs],
        out_shape=[jax.ShapeDtypeStruct((rows, VW), BF16), jax.ShapeDtypeStruct(s0.shape, F32),
                   jax.ShapeDtypeStruct((rows, MVW), BF16), jax.ShapeDtypeStruct(c0.shape, F32),
                   jax.ShapeDtypeStruct(n0.shape, F32), jax.ShapeDtypeStruct(m0.shape, F32),
                   jax.ShapeDtypeStruct(conv_state.shape, F32),
                   *[jax.ShapeDtypeStruct(w.shape, BF16) for w in casts]],
        scratch_shapes=[pltpu.VMEM((R, SUBLANES + T, QKV), F32)],
        compiler_params=_cparams(("arbitrary", "arbitrary")),
        name="mixers",
    )(proj3, proj3, proj3, proj3, proj3, proj3, small3, conv_state, s0, c0, n0, m0, conv_w, par,
      dn_norm_g.reshape(1, DV), ml_norm_g.reshape(1, MVW), *casts)
    return odn, s_new, oml, c_new, n_new, m_new, conv_new, cast


def _outproj_kernel(x_ref, a_ref, b_ref, wa_ref, wb_ref, o_ref):
    o_ref[...] = (x_ref[...] + jnp.dot(a_ref[...], wa_ref[...], preferred_element_type=F32)
                  + jnp.dot(b_ref[...], wb_ref[...], preferred_element_type=F32))


def _outproj(x, a, b, w_out):
    m, d = x.shape
    ka, kb = a.shape[1], b.shape[1]
    assert ka == kb
    tm = _pick(m, (512, 256, 128, 64, 32, 16, 8))
    return pl.pallas_call(
        _outproj_kernel,
        grid=(m // tm,),
        in_specs=[
            pl.BlockSpec((tm, d), lambda i: (i, 0)),
            pl.BlockSpec((tm, ka), lambda i: (i, 0)),
            pl.BlockSpec((tm, kb), lambda i: (i, 0)),
            pl.BlockSpec((ka, d), lambda i: (0, 0), pipeline_mode=pl.Buffered(1)),
            pl.BlockSpec((kb, d), lambda i: (1, 0), pipeline_mode=pl.Buffered(1)),
        ],
        out_specs=pl.BlockSpec((tm, d), lambda i: (i, 0)),
        out_shape=jax.ShapeDtypeStruct((m, d), F32),
        compiler_params=_cparams(("parallel",)),
        name="outproj",
    )(x, a, b, w_out, w_out)


def _ffn_kernel(x_ref, g_ref, wg_ref, wu_ref, cw_ref, cb_ref, wd_ref, stg_ref, stu_ref, gf_ref,
                y_ref, nsg_ref, nsu_ref, h_ref, act_ref, ext_ref, *carry,
                tm, tf, ff, shift, pad, tps, width, nj, wide_state, final_norm):
    i = pl.program_id(0)
    j = pl.program_id(1)
    hist = (width - 1) * shift

    def activate_tile():
        conv = []
        for idx, (w_ref, st_ref, ns_ref) in enumerate(
                ((wg_ref, stg_ref, nsg_ref), (wu_ref, stu_ref, nsu_ref))):
            col = pl.ds(pl.multiple_of(idx * ff + j * tf, LANES), tf)
            st_col = col if wide_state else slice(None)
            u = jnp.dot(h_ref[...], w_ref[...], preferred_element_type=F32)
            ext_ref[idx, pad:pad + tm, :] = u
            if tps == 1:
                ext_ref[idx, pad - hist:pad, :] = st_ref[0, :, st_col]
            else:
                ext_ref[idx, pad - hist:pad, :] = jnp.where(i % tps == 0, st_ref[0, :, st_col],
                                                            carry[0][j, idx])
            cw = cw_ref[:, col]
            c = u * cw[width - 1:width, :] + cb_ref[:, col]
            for t in range(width - 1):
                off = pad - hist + t * shift
                c = c + ext_ref[idx, off:off + tm, :] * cw[t:t + 1, :]
            new = u[tm - hist:, :]
            ns_ref[0, :, st_col] = new
            if tps != 1:
                carry[0][j, idx] = new
            conv.append(c)
        act_ref[...] = (conv[0] * _sigmoid(conv[0]) * conv[1]).astype(BF16)

    def down_tile():
        return jnp.dot(act_ref[...], wd_ref[...], preferred_element_type=F32)

    @pl.when(j == 0)
    def _():
        h_ref[...] = _rms(x_ref[...], g_ref[...]).astype(BF16)
        y_ref[...] = jnp.zeros_like(y_ref)
        activate_tile()

    @pl.when((j > 0) & (j < nj))
    def _():
        y_ref[...] += down_tile()
        activate_tile()

    @pl.when(j == nj)
    def _():
        xo = x_ref[...] + y_ref[...] + down_tile()
        y_ref[...] = _rms(xo, gf_ref[...]) if final_norm else xo


def _ffn(x, norm_g, w_up, conv_w, conv_b, w_down, state, final_g, *, tm, shift, tps, final_norm):
    m, d = x.shape
    ff = w_down.shape[0]
    width = conv_w.shape[0]
    hist = (width - 1) * shift
    pad = -(-hist // SUBLANES) * SUBLANES
    tf = _pick(ff, (512, 256, 128))
    nj = ff // tf
    n_tiles = m // tm
    wide_state = hist * 2 * ff * 4 <= WIDE_STATE_BYTES
    kern = functools.partial(_ffn_kernel, tm=tm, tf=tf, ff=ff, shift=shift, pad=pad, tps=tps,
                             width=width, nj=nj, wide_state=wide_state, final_norm=final_norm)
    scratch = [pltpu.VMEM((tm, d), BF16), pltpu.VMEM((tm, tf), BF16), pltpu.VMEM((2, pad + tm, tf), F32)]
    cur = lambda j: jnp.minimum(j, nj - 1)
    prv = lambda j: jnp.maximum(j - 1, 0)
    if tps != 1:
        scratch.append(pltpu.VMEM((nj, 2, hist, tf), F32))
    if wide_state:
        st_in = [pl.BlockSpec((1, hist, 2 * ff), lambda i, j: (i // tps, 0, 0))] * 2
        st_out = [pl.BlockSpec((1, hist, 2 * ff), lambda i, j: (i, 0, 0))]
        st_shape = [jax.ShapeDtypeStruct((n_tiles, hist, 2 * ff), F32)]
    else:
        st_in = [pl.BlockSpec((1, hist, tf), lambda i, j: (i // tps, 0, cur(j))),
                 pl.BlockSpec((1, hist, tf), lambda i, j: (i // tps, 0, nj + cur(j)))]
        st_out = [pl.BlockSpec((1, hist, tf), lambda i, j: (i, 0, cur(j)))] * 2
        st_shape = [jax.ShapeDtypeStruct((n_tiles, hist, ff), F32)] * 2
    call = pl.pallas_call(
        kern if not wide_state else _single_history(kern),
        grid=(n_tiles, nj + 1),
        in_specs=[
            pl.BlockSpec((tm, d), lambda i, j: (i, 0), pipeline_mode=pl.Buffered(1)),
            pl.BlockSpec((1, d), lambda i, j: (0, 0)),
            pl.BlockSpec((d, tf), lambda i, j: (0, cur(j))),
            pl.BlockSpec((d, tf), lambda i, j: (0, nj + cur(j))),
            pl.BlockSpec((width, 2 * ff), lambda i, j: (0, 0)),
            pl.BlockSpec((1, 2 * ff), lambda i, j: (0, 0)),
            pl.BlockSpec((tf, d), lambda i, j: (prv(j), 0)),
            *st_in,
            pl.BlockSpec((1, d), lambda i, j: (0, 0)),
        ],
        out_specs=[pl.BlockSpec((tm, d), lambda i, j: (i, 0)), *st_out],
        out_shape=[jax.ShapeDtypeStruct((m, d), F32), *st_shape],
        scratch_shapes=scratch,
        compiler_params=_cparams(("arbitrary", "arbitrary")),
        name="convffn",
    )
    y, *ns = call(x, norm_g.reshape(1, d), w_up, w_up, conv_w, conv_b.reshape(1, 2 * ff), w_down,
                  state, state, final_g.reshape(1, d))
    return y, [a[tps - 1::tps] for a in ns]


def _single_history(kern):
    def wrapped(*refs):
        ins, (y_ref, ns_ref), scratch = refs[:10], refs[10:12], refs[12:]
        return kern(*ins, y_ref, ns_ref, ns_ref, *scratch)
    return wrapped


def _layer(x, st, p, final_g, final_norm, decode):
    conv_buf, s0, c0, n0, m0, ffn_buf = st
    B, L, D = x.shape
    Hd, Hm = s0.shape[1], c0.shape[1]
    dn_qkv = conv_buf.shape[-1]
    dn_vw = Hd * s0.shape[3]
    ff = p["w_down"].shape[0]

    T = next((c for c in (MIX_CHUNK, CHUNK) if L % c == 0), L)
    N = L // T
    R = _pick(B, (8, 4, 2, 1)) if decode else 1
    if R * T > 128:
        R = 1

    x2 = x.reshape(B * L, D)
    big, small = _inproj(x2, p["norm_mix_g"], p["w_big"], p["w_small"])
    big3 = big.reshape(B * N, T, big.shape[1])
    small3 = small.reshape(B * N, T, LANES)

    pending = [k for k in ("w_out", "w_up", "w_down") if p[k].dtype != BF16]
    if not all(_cast_slab_rows(p[k].shape[0], (B // R) * N) for k in pending):
        p = dict(p, **{k: p[k].astype(BF16) for k in pending})
        pending = []
    assert T >= p["dn_conv_w"].shape[0] - 1
    o_dn, s_new, o_ml, c_new, n_new, m_new, conv_new, rounded = _mixers(
        big3, small3, conv_buf, s0, c0, n0, m0.reshape(B, 1, Hm), p["dn_conv_w"], p["par"],
        p["dn_norm_g"], p["ml_norm_g"], R=R, T=T, N=N, off_i=2 * Hd, off_f=2 * Hd + Hm,
        casts=tuple(p[k] for k in pending))
    p = dict(p, **dict(zip(pending, rounded)))

    x1 = _outproj(x2, o_dn, o_ml, p["w_out"])

    fw = p["ffn_conv_w"].shape[0]
    if decode:
        x1t = x1.reshape(B, L, D).transpose(1, 0, 2).reshape(L * B, D)
        state = ffn_buf.transpose(1, 0, 2).reshape(1, (fw - 1) * B, 2 * ff)
        y, parts = _ffn(x1t, p["norm_ffn_g"], p["w_up"], p["ffn_conv_w"], p["ffn_conv_b"],
                        p["w_down"], state, final_g, tm=L * B, shift=B, tps=1, final_norm=final_norm)
        y = y.reshape(L, B, D).transpose(1, 0, 2)
        ffn_new = jnp.concatenate(
            [a.reshape(fw - 1, B, a.shape[-1]).transpose(1, 0, 2) for a in parts], axis=-1)
    else:
        tm = _pick(L, (1024, 512, 256, 128, 64, 32, 16, 8))
        y, parts = _ffn(x1, p["norm_ffn_g"], p["w_up"], p["ffn_conv_w"], p["ffn_conv_b"],
                        p["w_down"], ffn_buf, final_g, tm=tm, shift=1, tps=L // tm,
                        final_norm=final_norm)
        y = y.reshape(B, L, D)
        ffn_new = jnp.concatenate(parts, axis=-1)
    return y, (conv_new, s_new, c_new, n_new, m_new.reshape(B, Hm), ffn_new), p


def _regroup_kernel(w_ref, big_ref, small_ref, *, c0, c1, c2):
    n = w_ref.shape[0]
    big_ref[:c0, :] = w_ref[:c0, :].astype(BF16)
    big_ref[c0:, :] = w_ref[c1:c2, :].astype(BF16)
    narrow = jnp.concatenate(
        [w_ref[c0:c1, :], w_ref[c2:, :], jnp.zeros((LANES - (c1 - c0) - (n - c2), w_ref.shape[1]), F32)],
        axis=0)
    small_ref[...] = narrow.astype(BF16)


def _regroup(w_t, c0, c1, c2):
    n, k = w_t.shape
    tk = _pick(k, (256, 128))
    nb = c0 + c2 - c1
    assert c0 % 16 == 0 and c1 % SUBLANES == 0 and c2 % SUBLANES == 0
    return pl.pallas_call(
        functools.partial(_regroup_kernel, c0=c0, c1=c1, c2=c2),
        grid=(k // tk,),
        in_specs=[pl.BlockSpec((n, tk), lambda i: (0, i))],
        out_specs=[pl.BlockSpec((nb, tk), lambda i: (0, i)), pl.BlockSpec((LANES, tk), lambda i: (0, i))],
        out_shape=[jax.ShapeDtypeStruct((nb, k), BF16), jax.ShapeDtypeStruct((LANES, k), BF16)],
        compiler_params=_cparams(("parallel",)),
        name="regroup",
    )(w_t)


def _prep_params(l, Hd, Hm, dn_qkv, dn_vw, ml_qk, ml_vw, norm_mix_g, w_in, dn_conv_w, dn_A_log,
                 dn_dt_bias, dn_norm_g, ml_i_bias, ml_f_bias, ml_norm_g, w_out, norm_ffn_g, w_up,
                 ffn_conv_w, ffn_conv_b, w_down):
    w = w_in[l]
    c0 = dn_qkv + dn_vw
    c1 = c0 + 2 * Hd
    c2 = c1 + 2 * ml_qk + 2 * ml_vw
    nsmall = 2 * Hd + 2 * Hm
    assert nsmall <= LANES and w.shape[1] == c2 + 2 * Hm
    w_big, w_small = _regroup(w.T, c0, c1, c2)
    z = lambda n: jnp.zeros((n,), F32)
    bias = jnp.concatenate([z(Hd), dn_dt_bias[l], ml_i_bias[l], ml_f_bias[l], z(LANES - nsmall)])
    alog = jnp.concatenate([z(Hd), dn_A_log[l], z(LANES - 2 * Hd)])
    par = jnp.zeros((SUBLANES, LANES), F32).at[0].set(bias).at[1].set(alog)
    return dict(norm_mix_g=norm_mix_g[l], w_big=w_big, w_small=w_small, dn_conv_w=dn_conv_w[l],
                par=par, dn_norm_g=dn_norm_g[l], ml_norm_g=ml_norm_g[l], w_out=w_out[l],
                norm_ffn_g=norm_ffn_g[l], w_up=w_up[l], ffn_conv_w=ffn_conv_w[l],
                ffn_conv_b=ffn_conv_b[l], w_down=w_down[l])


def kernel(x_prompt, x_sample, state_dn_conv, state_dn_S, state_ml_C, state_ml_n, state_ml_m,
           state_ffn_conv, norm_mix_g, w_in, dn_conv_w, dn_A_log, dn_dt_bias, dn_norm_g,
           ml_i_bias, ml_f_bias, ml_norm_g, w_out, norm_ffn_g, w_up, ffn_conv_w, ffn_conv_b,
           w_down, norm_final_g):
    states = (state_dn_conv, state_dn_S, state_ml_C, state_ml_n, state_ml_m, state_ffn_conv)
    depth = w_in.shape[0]
    batch = x_prompt.shape[0]
    Hd, dk, dv = state_dn_S.shape[2:]
    Hm, mdk, mdv = state_ml_C.shape[2:]
    xp, xs = x_prompt, x_sample
    p_new = [[] for _ in states]
    s_new = [[] for _ in states]
    for l in range(depth):
        p = _prep_params(l, Hd, Hm, state_dn_conv.shape[-1], Hd * dv, Hm * mdk, Hm * mdv,
                         norm_mix_g, w_in, dn_conv_w, dn_A_log, dn_dt_bias, dn_norm_g, ml_i_bias,
                         ml_f_bias, ml_norm_g, w_out, norm_ffn_g, w_up, ffn_conv_w, ffn_conv_b,
                         w_down)
        last = l == depth - 1
        st_p = tuple(jnp.zeros((batch,) + s.shape[2:], s.dtype) for s in states)
        st_s = tuple(s[l] for s in states)
        xp, np_st, p = _layer(xp, st_p, p, norm_final_g, last, decode=False)
        xs, ns_st, p = _layer(xs, st_s, p, norm_final_g, last, decode=True)
        for i in range(len(states)):
            p_new[i].append(np_st[i])
            s_new[i].append(ns_st[i])
    outs_p = tuple(jnp.stack(a, axis=0) for a in p_new)
    outs_s = tuple(jnp.stack(a, axis=0) for a in s_new)
    return (xp, xs) + outs_p + outs_s
```
